```python
import functools
import jax, jax.numpy as jnp
from jax import lax
import numpy as np

D_MODEL = 1024
BATCH = 1
SEQ = 16384
DEPTH = 1
DEC_BATCH = 128
DEC_SEQ = 8
PAST_LEN = 16384
PAGE_SIZE = 128

HEAD_DIM = 64
D_ATTN = D_MODEL // 2
N_HEADS = D_ATTN // HEAD_DIM
N_KV_HEADS = 2
GQA_GROUP = N_HEADS // N_KV_HEADS
D_KV = N_KV_HEADS * HEAD_DIM
D_CONV = D_MODEL // 2
D_MIX = D_ATTN + D_CONV
WINDOW = 128
CONV_W = 31
EPS = 1e-5
SPLIT_SIZES = (D_ATTN, D_KV, D_KV, D_ATTN, D_CONV, D_CONV, D_CONV)
D_IN_PROJ = D_ATTN + 2 * D_KV + D_ATTN + 3 * D_CONV

kernel_name = "hymba_swa_sink_conformer_step"


def rms_norm(x, g):
    xf = x.astype(jnp.float32)
    y = xf * lax.rsqrt(jnp.mean(xf * xf, axis=-1, keepdims=True) + EPS)
    return (y * g.astype(jnp.float32)).astype(x.dtype)


def layer_norm(x, g, b):
    xf = x.astype(jnp.float32)
    mu = jnp.mean(xf, axis=-1, keepdims=True)
    xc = xf - mu
    y = xc * lax.rsqrt(jnp.mean(xc * xc, axis=-1, keepdims=True) + EPS)
    return (y * g.astype(jnp.float32) + b.astype(jnp.float32)).astype(x.dtype)


def alibi_slopes():
    i = jnp.arange(1, N_HEADS + 1, dtype=jnp.float32)
    return jnp.exp2(-8.0 * i / N_HEADS)


def _attend(q, k, v, q_pos, k_pos, sinks):
    s = jnp.einsum('bnqhgd,bnshd->bnhgqs', q, k).astype(jnp.float32) * (HEAD_DIM ** -0.5)
    dist = q_pos[:, :, None] - k_pos[:, None, :]
    valid = (dist >= 0) & (dist < WINDOW) & (k_pos[:, None, :] >= 0)
    slopes = alibi_slopes().reshape(N_KV_HEADS, GQA_GROUP)[:, :, None, None]
    s = s - slopes * dist[:, None, None].astype(jnp.float32)
    s = jnp.where(valid[:, None, None], s, -jnp.inf)
    sink = sinks.astype(jnp.float32).reshape(N_KV_HEADS, GQA_GROUP)[:, :, None, None]
    m = jnp.maximum(jnp.max(s, axis=-1, keepdims=True), sink)
    p = jnp.exp(s - m)
    w = p / (jnp.sum(p, axis=-1, keepdims=True) + jnp.exp(sink - m))
    return jnp.einsum('bnhgqs,bnshd->bnqhgd', w.astype(v.dtype), v)


def prompt_attention(q, k, v, sinks):
    B, T = q.shape[0], q.shape[1]
    nb = T // WINDOW
    qb = q.reshape(B, nb, WINDOW, N_KV_HEADS, GQA_GROUP, HEAD_DIM)
    kb = k.reshape(B, nb, WINDOW, N_KV_HEADS, HEAD_DIM)
    vb = v.reshape(B, nb, WINDOW, N_KV_HEADS, HEAD_DIM)
    prev = lambda t: jnp.concatenate([jnp.zeros_like(t[:, :1]), t[:, :-1]], axis=1)
    kk = jnp.concatenate([prev(kb), kb], axis=2)
    vv = jnp.concatenate([prev(vb), vb], axis=2)
    pos = jnp.arange(T, dtype=jnp.int32).reshape(nb, WINDOW)
    k_pos = jnp.concatenate([pos - WINDOW, pos], axis=1)
    o = _attend(qb, kk, vv, pos, k_pos, sinks).reshape(B, T, D_ATTN)
    return o, k[:, -WINDOW:], v[:, -WINDOW:]


def sample_attention(q, k, v, sinks, cache_k, cache_v):
    B, T = q.shape[0], q.shape[1]
    kk = jnp.concatenate([cache_k.astype(k.dtype), k], axis=1)
    vv = jnp.concatenate([cache_v.astype(v.dtype), v], axis=1)
    q_pos = PAST_LEN + jnp.arange(T, dtype=jnp.int32)
    k_pos = jnp.concatenate([PAST_LEN - WINDOW + jnp.arange(WINDOW, dtype=jnp.int32), q_pos])
    qb = q.reshape(B, 1, T, N_KV_HEADS, GQA_GROUP, HEAD_DIM)
    o = _attend(qb, kk[:, None], vv[:, None], q_pos[None], k_pos[None], sinks).reshape(B, T, D_ATTN)
    return o, kk[:, -WINDOW:], vv[:, -WINDOW:]


def causal_depthwise_conv(u_padded, w, b):
    out = lax.conv_general_dilated(
        u_padded, w[:, None, :].astype(u_padded.dtype), window_strides=(1,), padding='VALID',
        dimension_numbers=('NWC', 'WIO', 'NWC'), feature_group_count=D_CONV)
    return out + b.astype(out.dtype)


def mixer_layer(x, attend_fn, conv_prefix, norm_g, w_in, sinks, dw_w, dw_b, ln_g, ln_b, w_pw2, w_out):
    B, T, _ = x.shape
    h = rms_norm(x, norm_g)
    z = h @ w_in.astype(h.dtype)
    q, k, v, g_a, u_a, u_b, g_c = jnp.split(z, list(np.cumsum(SPLIT_SIZES)[:-1]), axis=-1)
    q = q.reshape(B, T, N_HEADS, HEAD_DIM)
    k = k.reshape(B, T, N_KV_HEADS, HEAD_DIM)
    v = v.reshape(B, T, N_KV_HEADS, HEAD_DIM)
    attn, k_state, v_state = attend_fn(q, k, v, sinks)
    u = u_a * jax.nn.sigmoid(u_b)
    up = jnp.concatenate([conv_prefix.astype(u.dtype), u], axis=1)
    c = causal_depthwise_conv(up, dw_w, dw_b)
    c = jax.nn.silu(layer_norm(c, ln_g, ln_b)) @ w_pw2.astype(u.dtype)
    mixed = jnp.concatenate([attn * jax.nn.silu(g_a), c * jax.nn.silu(g_c)], axis=-1)
    y = x + mixed @ w_out.astype(mixed.dtype)
    return y, k_state, v_state, up[:, -(CONV_W - 1):]


def setup_inputs(seed: int = 0) -> dict:
    key = jax.random.key(seed)
    ks = jax.random.split(key, 16)
    f32 = jnp.float32
    nrm = lambda k, shape, scale: jax.random.normal(k, shape, f32) * scale
    return {
        "x_prompt": nrm(ks[0], (BATCH, SEQ, D_MODEL), 1.0),
        "x_sample": nrm(ks[1], (DEC_BATCH, DEC_SEQ, D_MODEL), 1.0),
        "cache_k": nrm(ks[2], (DEPTH, DEC_BATCH, WINDOW, N_KV_HEADS, HEAD_DIM), 1.0),
        "cache_v": nrm(ks[3], (DEPTH, DEC_BATCH, WINDOW, N_KV_HEADS, HEAD_DIM), 1.0),
        "state_conv": nrm(ks[4], (DEPTH, DEC_BATCH, CONV_W - 1, D_CONV), 0.5),
        "norm_g": 1.0 + nrm(ks[5], (DEPTH, D_MODEL), 0.02),
        "w_in": nrm(ks[6], (DEPTH, D_MODEL, D_IN_PROJ), D_MODEL ** -0.5),
        "attn_sinks": nrm(ks[7], (DEPTH, N_HEADS), 0.5),
        "dw_w": nrm(ks[8], (DEPTH, CONV_W, D_CONV), CONV_W ** -0.5),
        "dw_b": nrm(ks[9], (DEPTH, D_CONV), 0.02),
        "conv_ln_g": 1.0 + nrm(ks[10], (DEPTH, D_CONV), 0.02),
        "conv_ln_b": nrm(ks[11], (DEPTH, D_CONV), 0.02),
        "w_pw2": nrm(ks[12], (DEPTH, D_CONV, D_CONV), D_CONV ** -0.5),
        "w_out": nrm(ks[13], (DEPTH, D_MIX, D_MODEL), D_MIX ** -0.5),
        "final_norm_g": 1.0 + nrm(ks[14], (D_MODEL,), 0.02),
    }


def reference(x_prompt, x_sample, cache_k, cache_v, state_conv, norm_g, w_in, attn_sinks,
              dw_w, dw_b, conv_ln_g, conv_ln_b, w_pw2, w_out, final_norm_g):
    hp, hs = x_prompt, x_sample
    pk, pv, pc, sk, sv, sc = [], [], [], [], [], []
    zero_prefix = jnp.zeros((x_prompt.shape[0], CONV_W - 1, D_CONV), x_prompt.dtype)
    for l in range(DEPTH):
        lw = (norm_g[l], w_in[l], attn_sinks[l], dw_w[l], dw_b[l], conv_ln_g[l], conv_ln_b[l], w_pw2[l], w_out[l])
        hp, k_p, v_p, c_p = mixer_layer(hp, prompt_attention, zero_prefix, *lw)
        attend_s = functools.partial(sample_attention, cache_k=cache_k[l], cache_v=cache_v[l])
        hs, k_s, v_s, c_s = mixer_layer(hs, attend_s, state_conv[l], *lw)
        pk.append(k_p); pv.append(v_p); pc.append(c_p)
        sk.append(k_s); sv.append(v_s); sc.append(c_s)
    y_prompt = rms_norm(hp, final_norm_g)
    y_sample = rms_norm(hs, final_norm_g)
    return (y_prompt, y_sample, jnp.stack(pk), jnp.stack(pv), jnp.stack(pc),
            jnp.stack(sk), jnp.stack(sv), jnp.stack(sc))
```

```python
import functools

import jax
import jax.numpy as jnp
from jax import lax
from jax.experimental import pallas as pl
from jax.experimental.pallas import tpu as pltpu

HEAD_DIM = 64
N_HEADS = 8
N_KV_HEADS = 2
GQA_GROUP = N_HEADS // N_KV_HEADS
N_PAIRS = N_HEADS // 2
PAIRS_PER_KV = N_PAIRS // N_KV_HEADS
WINDOW = 128
CONV_W = 31
CONV_TAIL = CONV_W - 1
EPS = 1e-5
LANES = 128
SUBLANES = 8
HALF = LANES // 2
NEG_INF = float("-inf")

PROMPT_TILE = 256
CONV_ROWS = 16
CONV_PAD = 32
SAMPLE_GROUP = 16
VMEM_LIMIT_BYTES = 56 * 1024 * 1024

F32 = jnp.float32
BF16 = jnp.bfloat16


def _alibi_slope(h):
    return 2.0 ** (-8.0 * (h + 1) / N_HEADS)


def _rms_norm(x, g):
    ms = jnp.mean(x * x, axis=-1, keepdims=True)
    return x * lax.rsqrt(ms + EPS) * g


def _silu(x):
    return x * jax.nn.sigmoid(x)


def _dot(a, b):
    return jnp.dot(a, b, preferred_element_type=F32)


def _dot_nt(a, b):
    return lax.dot_general(a, b, (((1,), (1,)), ((), ())), preferred_element_type=F32)


def _conv_branch_tail(c, lng, lnb, wpw_ref):
    mu = jnp.mean(c, axis=-1, keepdims=True)
    xc = c - mu
    var = jnp.mean(xc * xc, axis=-1, keepdims=True)
    y = xc * lax.rsqrt(var + EPS) * lng + lnb
    return _dot(_silu(y).astype(BF16), wpw_ref[...])


def _mix_out(x, attn, g_a, c2, g_c, wout_ref, fg, d_attn, final_norm):
    m_a = (attn * _silu(g_a)).astype(BF16)
    m_c = (c2 * _silu(g_c)).astype(BF16)
    y = x + _dot(m_a, wout_ref[:d_attn, :]) + _dot(m_c, wout_ref[d_attn:, :])
    return _rms_norm(y, fg) if final_norm else y


def _softmax_sink(s, sink):
    m = jnp.maximum(jnp.max(s, axis=-1, keepdims=True), sink)
    p = jnp.exp(s - m)
    den = jnp.sum(p, axis=-1, keepdims=True) + jnp.exp(sink - m)
    return p, 1.0 / den


def _prompt_kernel(sinks_ref, x_ref, ng_ref, win_ref, dww_ref, dwb_ref, lng_ref, lnb_ref,
                   wpw_ref, wout_ref, fg_ref,
                   y_ref, kst_ref, vst_ref, cst_ref,
                   kv_ref, up_ref, shifted_ref, bias_ref, attn_ref, conv_ref, *, tile, final_norm):
    t = pl.program_id(1)
    n_t = pl.num_programs(1)
    d_attn = N_HEADS * HEAD_DIM
    d_kv = N_KV_HEADS * HEAD_DIM
    d_conv = up_ref.shape[1]
    n_blocks = tile // WINDOW

    @pl.when(t == 0)
    def _init():
        kv_ref[:, :WINDOW, :] = jnp.zeros((kv_ref.shape[0], WINDOW, LANES), BF16)
        up_ref[:CONV_PAD, :] = jnp.zeros((CONV_PAD, d_conv), F32)
        qi = lax.broadcasted_iota(jnp.int32, (WINDOW, 2 * WINDOW), 0)
        ji = lax.broadcasted_iota(jnp.int32, (WINDOW, 2 * WINDOW), 1)
        dist = qi + WINDOW - ji
        valid = (dist >= 0) & (dist < WINDOW)
        valid_first = valid & (ji >= WINDOW)
        distf = dist.astype(F32)
        for h in range(N_HEADS):
            pen = -_alibi_slope(h) * distf
            bias_ref[0, h] = jnp.where(valid, pen, NEG_INF)
            bias_ref[1, h] = jnp.where(valid_first, pen, NEG_INF)

    x = x_ref[...]
    h_in = _rms_norm(x, ng_ref[...]).astype(BF16)

    kv = _dot(h_in, win_ref[:, d_attn:d_attn + 2 * d_kv])
    k = kv[:, :d_kv]
    v = kv[:, d_kv:]
    lo = lax.broadcasted_iota(jnp.int32, (tile, LANES), 1) < HALF
    zero = jnp.zeros((tile, LANES), F32)
    for base, val in ((0, k), (4, v)):
        rolled = pltpu.roll(val, HALF, axis=1)
        kv_ref[base + 0, WINDOW:, :] = jnp.where(lo, val, zero).astype(BF16)
        kv_ref[base + 1, WINDOW:, :] = jnp.where(lo, zero, rolled).astype(BF16)
        kv_ref[base + 2, WINDOW:, :] = jnp.where(lo, rolled, zero).astype(BF16)
        kv_ref[base + 3, WINDOW:, :] = jnp.where(lo, zero, val).astype(BF16)

    q = (_dot(h_in, win_ref[:, :d_attn]) * (HEAD_DIM ** -0.5)).astype(BF16)
    lo_blk = lax.broadcasted_iota(jnp.int32, (WINDOW, LANES), 1) < HALF
    for blk in range(n_blocks):
        r0 = blk * WINDOW
        first = jnp.where(t == 0, 1, 0) if blk == 0 else 0
        for pair in range(N_PAIRS):
            kvh = pair // PAIRS_PER_KV
            qp = q[r0:r0 + WINDOW, pair * LANES:(pair + 1) * LANES]
            o = None
            rs = []
            for sub in range(2):
                h = 2 * pair + sub
                k_win = kv_ref[2 * kvh + sub, r0:r0 + 2 * WINDOW, :]
                v_win = kv_ref[4 + 2 * kvh + sub, r0:r0 + 2 * WINDOW, :]
                s = _dot_nt(qp, k_win) + bias_ref[first, h]
                p, r = _softmax_sink(s, sinks_ref[h])
                pv = _dot(p.astype(BF16), v_win)
                o = pv if o is None else o + pv
                rs.append(r)
            attn_ref[r0:r0 + WINDOW, pair * LANES:(pair + 1) * LANES] = (
                o * jnp.where(lo_blk, rs[0], rs[1]))

    kv_ref[:, :WINDOW, :] = kv_ref[:, tile:, :]

    u_off = d_attn + 2 * d_kv + d_attn
    u_ab = _dot(h_in, win_ref[:, u_off:u_off + 2 * d_conv])
    u = u_ab[:, :d_conv] * jax.nn.sigmoid(u_ab[:, d_conv:])
    up_ref[CONV_PAD:, :] = u

    n_shift = shifted_ref.shape[1]
    for s in range(1, SUBLANES):
        shifted_ref[s - 1] = up_ref[s:s + n_shift, :]

    def conv_chunk(i, carry):
        r = pl.multiple_of(i * CONV_ROWS, CONV_ROWS)
        acc = jnp.zeros((CONV_ROWS, d_conv), F32)
        for j in range(CONV_W):
            off = CONV_PAD - CONV_TAIL + j
            a, s = off // SUBLANES, off % SUBLANES
            rows = pl.ds(r + a * SUBLANES, CONV_ROWS)
            win = up_ref[rows, :] if s == 0 else shifted_ref[s - 1, rows, :]
            acc = acc + win * dww_ref[j:j + 1, :]
        conv_ref[pl.ds(r, CONV_ROWS), :] = acc
        return carry

    lax.fori_loop(0, tile // CONV_ROWS, conv_chunk, 0)
    c2 = _conv_branch_tail(conv_ref[...] + dwb_ref[...], lng_ref[...], lnb_ref[...], wpw_ref)

    @pl.when(t == n_t - 1)
    def _states():
        kst_ref[...] = k[tile - WINDOW:, :]
        vst_ref[...] = v[tile - WINDOW:, :]
        cst_ref[...] = up_ref[CONV_PAD + tile - CONV_TAIL:, :]

    up_ref[:CONV_PAD, :] = up_ref[tile:, :]

    g_a = _dot(h_in, win_ref[:, d_attn + 2 * d_kv:u_off])
    g_c = _dot(h_in, win_ref[:, u_off + 2 * d_conv:])
    y_ref[...] = _mix_out(x, attn_ref[...], g_a, c2, g_c, wout_ref, fg_ref[...], d_attn, final_norm)


def _prompt_layer(x, sinks, ng, win, dww, dwb, lng, lnb, wpw, wout, fg, final_norm):
    B, T, D = x.shape
    tile = PROMPT_TILE
    assert T % tile == 0 and tile % WINDOW == 0 and tile % CONV_ROWS == 0
    d_conv = dww.shape[1]
    d_kv = N_KV_HEADS * HEAD_DIM
    d_attn = N_HEADS * HEAD_DIM
    const = lambda shape: pl.BlockSpec(shape, lambda b, t, *_: (0,) * len(shape))
    grid_spec = pltpu.PrefetchScalarGridSpec(
        num_scalar_prefetch=1,
        grid=(B, T // tile),
        in_specs=[
            pl.BlockSpec((None, tile, D), lambda b, t, *_: (b, t, 0)),
            const(ng.shape), const(win.shape), const(dww.shape), const(dwb.shape),
            const(lng.shape), const(lnb.shape), const(wpw.shape), const(wout.shape), const(fg.shape),
        ],
        out_specs=[
            pl.BlockSpec((None, tile, D), lambda b, t, *_: (b, t, 0)),
            pl.BlockSpec((None, WINDOW, d_kv), lambda b, t, *_: (b, 0, 0)),
            pl.BlockSpec((None, WINDOW, d_kv), lambda b, t, *_: (b, 0, 0)),
            pl.BlockSpec((None, CONV_TAIL, d_conv), lambda b, t, *_: (b, 0, 0)),
        ],
        scratch_shapes=[
            pltpu.VMEM((8, WINDOW + tile, LANES), BF16),
            pltpu.VMEM((CONV_PAD + tile, d_conv), F32),
            pltpu.VMEM((SUBLANES - 1, CONV_PAD - SUBLANES + tile, d_conv), F32),
            pltpu.VMEM((2, N_HEADS, WINDOW, 2 * WINDOW), F32),
            pltpu.VMEM((tile, d_attn), F32),
            pltpu.VMEM((tile, d_conv), F32),
        ],
    )
    return pl.pallas_call(
        functools.partial(_prompt_kernel, tile=tile, final_norm=final_norm),
        grid_spec=grid_spec,
        out_shape=[
            jax.ShapeDtypeStruct((B, T, D), F32),
            jax.ShapeDtypeStruct((B, WINDOW, d_kv), F32),
            jax.ShapeDtypeStruct((B, WINDOW, d_kv), F32),
            jax.ShapeDtypeStruct((B, CONV_TAIL, d_conv), F32),
        ],
        compiler_params=pltpu.CompilerParams(
            dimension_semantics=("arbitrary", "arbitrary"), vmem_limit_bytes=VMEM_LIMIT_BYTES),
        name="prompt_layer",
    )(sinks, x, ng, win, dww, dwb, lng, lnb, wpw, wout, fg)


def _sample_kernel(x_ref, ck_ref, cv_ref, cs_ref, sink_ref, ng_ref, win_ref, dww_ref, dwb_ref,
                   lng_ref, lnb_ref, wpw_ref, wout_ref, fg_ref,
                   y_ref, ko_ref, vo_ref, co_ref,
                   q_ref, kn_ref, vn_ref, u_ref, ups_ref, bias_ref, attn_ref, conv_ref,
                   *, group, dec, final_norm):
    d_attn = N_HEADS * HEAD_DIM
    d_kv = N_KV_HEADS * HEAD_DIM
    d_conv = u_ref.shape[1]
    n_keys = WINDOW + dec
    rows_q = N_HEADS * dec

    @pl.when(pl.program_id(0) == 0)
    def _init():
        ri = lax.broadcasted_iota(jnp.int32, (rows_q, n_keys), 0)
        si = lax.broadcasted_iota(jnp.int32, (rows_q, n_keys), 1)
        ti = ri
        slope = jnp.zeros((rows_q, n_keys), F32)
        for h in range(N_HEADS):
            in_head = (ri >= h * dec) & (ri < (h + 1) * dec)
            ti = jnp.where(in_head, ri - h * dec, ti)
            slope = jnp.where(in_head, _alibi_slope(h), slope)
        dist = WINDOW + ti - si
        valid = (dist >= 0) & (dist < WINDOW)
        distf = dist.astype(F32)
        bias_ref[...] = jnp.where(valid, -slope * distf, NEG_INF)

    x = x_ref[...]
    h_in = _rms_norm(x, ng_ref[...]).astype(BF16)
    q_ref[...] = _dot(h_in, win_ref[:, :d_attn]) * (HEAD_DIM ** -0.5)
    kv = _dot(h_in, win_ref[:, d_attn:d_attn + 2 * d_kv])
    kn_ref[...] = kv[:, :d_kv]
    vn_ref[...] = kv[:, d_kv:]
    u_off = d_attn + 2 * d_kv + d_attn
    u_ab = _dot(h_in, win_ref[:, u_off:u_off + 2 * d_conv])
    u_ref[...] = u_ab[:, :d_conv] * jax.nn.sigmoid(u_ab[:, d_conv:])

    lo = lax.broadcasted_iota(jnp.int32, (dec, LANES), 1) < HALF
    zero = jnp.zeros((dec, LANES), F32)
    sink = sink_ref[...]
    bias = bias_ref[...]

    def per_element(b, carry):
        r = pl.multiple_of(b * dec, dec)
        k_new = kn_ref[pl.ds(r, dec), :]
        v_new = vn_ref[pl.ds(r, dec), :]
        k_all = jnp.concatenate([ck_ref[b], k_new], axis=0)
        v_all = jnp.concatenate([cv_ref[b], v_new], axis=0)
        ko_ref[b] = k_all[dec:, :]
        vo_ref[b] = v_all[dec:, :]

        q_b = q_ref[pl.ds(r, dec), :]
        rows = []
        for h in range(N_HEADS):
            tile_q = q_b[:, (h // 2) * LANES:(h // 2 + 1) * LANES]
            in_lo = h % 2 == 0
            want_lo = h // GQA_GROUP == 0
            src = tile_q if in_lo == want_lo else pltpu.roll(tile_q, HALF, axis=1)
            rows.append(jnp.where(lo, src, zero) if want_lo else jnp.where(lo, zero, src))
        q_rows = jnp.concatenate(rows, axis=0).astype(BF16)

        s = _dot_nt(q_rows, k_all.astype(BF16)) + bias
        p, rden = _softmax_sink(s, sink)
        o = _dot(p.astype(BF16), v_all.astype(BF16)) * rden

        tiles = []
        for pair in range(N_PAIRS):
            a = o[(2 * pair) * dec:(2 * pair + 1) * dec, :]
            c = o[(2 * pair + 1) * dec:(2 * pair + 2) * dec, :]
            if pair // PAIRS_PER_KV == 0:
                tiles.append(jnp.where(lo, a, pltpu.roll(c, HALF, axis=1)))
            else:
                tiles.append(jnp.where(lo, pltpu.roll(a, HALF, axis=1), c))
        attn_ref[pl.ds(r, dec), :] = jnp.concatenate(tiles, axis=1)

        ups_ref[CONV_PAD - CONV_TAIL:CONV_PAD, :] = cs_ref[b]
        ups_ref[CONV_PAD:, :] = u_ref[pl.ds(r, dec), :]
        acc = jnp.zeros((dec, d_conv), F32)
        for j in range(CONV_W):
            acc = acc + ups_ref[CONV_PAD - CONV_TAIL + j:CONV_PAD - CONV_TAIL + j + dec, :] * dww_ref[j:j + 1, :]
        conv_ref[pl.ds(r, dec), :] = acc
        co_ref[b] = ups_ref[CONV_PAD + dec - CONV_TAIL:, :]
        return carry

    lax.fori_loop(0, group, per_element, 0)

    c2 = _conv_branch_tail(conv_ref[...] + dwb_ref[...], lng_ref[...], lnb_ref[...], wpw_ref)
    g_a = _dot(h_in, win_ref[:, d_attn + 2 * d_kv:u_off])
    g_c = _dot(h_in, win_ref[:, u_off + 2 * d_conv:])
    y_ref[...] = _mix_out(x, attn_ref[...], g_a, c2, g_c, wout_ref, fg_ref[...], d_attn, final_norm)


def _sample_layer(x, ck, cv, cs, sink_col, ng, win, dww, dwb, lng, lnb, wpw, wout, fg, final_norm):
    Bs, dec, D = x.shape
    group = SAMPLE_GROUP
    assert Bs % group == 0 and dec % 8 == 0 and dec <= CONV_TAIL
    d_conv = dww.shape[1]
    d_kv = N_KV_HEADS * HEAD_DIM
    d_attn = N_HEADS * HEAD_DIM
    rows = group * dec
    x2 = x.reshape(Bs * dec, D)
    const = lambda shape: pl.BlockSpec(shape, lambda g: (0,) * len(shape))
    per_group = lambda shape: pl.BlockSpec((group,) + shape, lambda g: (g,) + (0,) * len(shape))
    y, ko, vo, co = pl.pallas_call(
        functools.partial(_sample_kernel, group=group, dec=dec, final_norm=final_norm),
        grid=(Bs // group,),
        in_specs=[
            pl.BlockSpec((rows, D), lambda g: (g, 0)),
            per_group((WINDOW, d_kv)), per_group((WINDOW, d_kv)), per_group((CONV_TAIL, d_conv)),
            const(sink_col.shape), const(ng.shape), const(win.shape), const(dww.shape), const(dwb.shape),
            const(lng.shape), const(lnb.shape), const(wpw.shape), const(wout.shape), const(fg.shape),
        ],
        out_specs=[
            pl.BlockSpec((rows, D), lambda g: (g, 0)),
            per_group((WINDOW, d_kv)), per_group((WINDOW, d_kv)), per_group((CONV_TAIL, d_conv)),
        ],
        out_shape=[
            jax.ShapeDtypeStruct((Bs * dec, D), F32),
            jax.ShapeDtypeStruct((Bs, WINDOW, d_kv), F32),
            jax.ShapeDtypeStruct((Bs, WINDOW, d_kv), F32),
            jax.ShapeDtypeStruct((Bs, CONV_TAIL, d_conv), F32),
        ],
        scratch_shapes=[
            pltpu.VMEM((rows, d_attn), F32),
            pltpu.VMEM((rows, d_kv), F32),
            pltpu.VMEM((rows, d_kv), F32),
            pltpu.VMEM((rows, d_conv), F32),
            pltpu.VMEM((CONV_PAD + dec, d_conv), F32),
            pltpu.VMEM((N_HEADS * dec, WINDOW + dec), F32),
            pltpu.VMEM((rows, d_attn), F32),
            pltpu.VMEM((rows, d_conv), F32),
        ],
        compiler_params=pltpu.CompilerParams(
            dimension_semantics=("arbitrary",), vmem_limit_bytes=VMEM_LIMIT_BYTES),
        name="sample_layer",
    )(x2, ck, cv, cs, sink_col, ng, win, dww, dwb, lng, lnb, wpw, wout, fg)
    return y.reshape(Bs, dec, D), ko, vo, co


def kernel(x_prompt, x_sample, cache_k, cache_v, state_conv, norm_g, w_in, attn_sinks, dw_w, dw_b,
           conv_ln_g, conv_ln_b, w_pw2, w_out, final_norm_g):
    depth = w_in.shape[0]
    B = x_prompt.shape[0]
    Bs, dec = x_sample.shape[0], x_sample.shape[1]
    d_kv = N_KV_HEADS * HEAD_DIM
    fg = final_norm_g.reshape(1, -1)
    hp, hs = x_prompt, x_sample
    pk, pv, pc, sk, sv, sc = [], [], [], [], [], []
    for l in range(depth):
        final_norm = l == depth - 1
        row = lambda a: a[l].reshape(1, -1)
        win = w_in[l].astype(BF16)
        wpw = w_pw2[l].astype(BF16)
        wout = w_out[l].astype(BF16)
        shared = (row(norm_g), win, dw_w[l], row(dw_b), row(conv_ln_g), row(conv_ln_b), wpw, wout, fg)
        hp, k_p, v_p, c_p = _prompt_layer(hp, attn_sinks[l], *shared, final_norm)
        sink_col = jnp.repeat(attn_sinks[l], dec).reshape(N_HEADS * dec, 1)
        hs, k_s, v_s, c_s = _sample_layer(
            hs, cache_k[l].reshape(Bs, WINDOW, d_kv), cache_v[l].reshape(Bs, WINDOW, d_kv),
            state_conv[l], sink_col, *shared, final_norm)
        pk.append(k_p.reshape(B, WINDOW, N_KV_HEADS, HEAD_DIM))
        pv.append(v_p.reshape(B, WINDOW, N_KV_HEADS, HEAD_DIM))
        pc.append(c_p)
        sk.append(k_s.reshape(Bs, WINDOW, N_KV_HEADS, HEAD_DIM))
        sv.append(v_s.reshape(Bs, WINDOW, N_KV_HEADS, HEAD_DIM))
        sc.append(c_s)
    return (hp, hs, jnp.stack(pk), jnp.stack(pv), jnp.stack(pc),
            jnp.stack(sk), jnp.stack(sv), jnp.stack(sc))
```

```python
import functools

import jax
import jax.numpy as jnp
from jax import lax
from jax.experimental import pallas as pl
from jax.experimental.pallas import tpu as pltpu

HEAD_DIM = 64
N_HEADS = 8
N_KV_HEADS = 2
GQA_GROUP = N_HEADS // N_KV_HEADS
N_PAIRS = N_HEADS // 2
PAIRS_PER_KV = N_PAIRS // N_KV_HEADS
WINDOW = 128
CONV_W = 31
CONV_TAIL = CONV_W - 1
EPS = 1e-5
LANES = 128
SUBLANES = 8
HALF = LANES // 2
NEG_INF = float("-inf")
LOG2E = 1.4426950408889634

PROMPT_TILE = 256
CONV_ROWS = 16
CONV_PAD = 32
SAMPLE_GROUP = 16
VMEM_LIMIT_BYTES = 56 * 1024 * 1024

F32 = jnp.float32
BF16 = jnp.bfloat16


def _alibi_slope(h):
    return 2.0 ** (-8.0 * (h + 1) / N_HEADS)


def _rms_norm(x, g):
    ms = jnp.mean(x * x, axis=-1, keepdims=True)
    return x * lax.rsqrt(ms + EPS) * g


def _silu(x):
    return x * jax.nn.sigmoid(x)


def _dot(a, b):
    return jnp.dot(a, b, preferred_element_type=F32)


def _dot_nt(a, b):
    return lax.dot_general(a, b, (((1,), (1,)), ((), ())), preferred_element_type=F32)


def _conv_branch_tail(c, lng, lnb, wpw_ref):
    mu = jnp.mean(c, axis=-1, keepdims=True)
    xc = c - mu
    var = jnp.mean(xc * xc, axis=-1, keepdims=True)
    y = xc * lax.rsqrt(var + EPS) * lng + lnb
    return _dot(_silu(y).astype(BF16), wpw_ref[...])


def _mix_out(x, attn, g_a, c2, g_c, wout_ref, fg, d_attn, final_norm):
    m_a = (attn * _silu(g_a)).astype(BF16)
    m_c = (c2 * _silu(g_c)).astype(BF16)
    y = x + _dot(m_a, wout_ref[:d_attn, :]) + _dot(m_c, wout_ref[d_attn:, :])
    return _rms_norm(y, fg) if final_norm else y


def _softmax_sink(s, sink):
    m = jnp.maximum(jnp.max(s, axis=-1, keepdims=True), sink)
    p = jnp.exp(s - m)
    den = jnp.sum(p, axis=-1, keepdims=True) + jnp.exp(sink - m)
    return p, 1.0 / den


def _prompt_kernel(x_ref, sink_ref, ng_ref, win_ref, dww_ref, dwb_ref, lng_ref, lnb_ref,
                   wpw_ref, wout_ref, fg_ref,
                   y_ref, kst_ref, vst_ref, cst_ref,
                   k_ref, vt_ref, up_ref, shifted_ref, wb_ref, bias_ref, attn_ref, conv_ref,
                   *, tile, final_norm):
    t = pl.program_id(1)
    n_t = pl.num_programs(1)
    d_attn = N_HEADS * HEAD_DIM
    d_kv = N_KV_HEADS * HEAD_DIM
    d_conv = up_ref.shape[1]
    n_blocks = tile // WINDOW
    two_w = 2 * WINDOW

    @pl.when(t == 0)
    def _init():
        k_ref[:WINDOW, :] = jnp.zeros((WINDOW, d_kv), BF16)
        vt_ref[:, :WINDOW] = jnp.zeros((d_kv, WINDOW), BF16)
        up_ref[:CONV_PAD, :] = jnp.zeros((CONV_PAD, d_conv), F32)
        for j in range(CONV_W):
            wb_ref[j] = jnp.broadcast_to(dww_ref[j:j + 1, :], (SUBLANES, d_conv))
        ji = lax.broadcasted_iota(jnp.int32, (two_w, two_w), 0)
        ci = lax.broadcasted_iota(jnp.int32, (two_w, two_w), 1)
        odd = ci >= WINDOW
        dist = jnp.where(odd, ci - WINDOW, ci) + WINDOW - ji
        valid = (dist >= 0) & (dist < WINDOW)
        valid_first = valid & (ji >= WINDOW)
        distf = dist.astype(F32) * LOG2E
        for pair in range(N_PAIRS):
            pen = -jnp.where(odd, _alibi_slope(2 * pair + 1), _alibi_slope(2 * pair)) * distf
            bias_ref[0, pair] = jnp.where(valid, pen, NEG_INF)
            bias_ref[1, pair] = jnp.where(valid_first, pen, NEG_INF)

    x = x_ref[...]
    h_in = _rms_norm(x, ng_ref[...]).astype(BF16)

    kv = _dot(h_in, win_ref[:, d_attn:d_attn + 2 * d_kv])
    k = kv[:, :d_kv]
    v = kv[:, d_kv:]
    k_ref[WINDOW:, :] = k.astype(BF16)
    vt_ref[:, WINDOW:] = v.T.astype(BF16)

    u_off = d_attn + 2 * d_kv + d_attn
    u_ab = _dot(h_in, win_ref[:, u_off:u_off + 2 * d_conv])
    u = u_ab[:, :d_conv] * jax.nn.sigmoid(u_ab[:, d_conv:])
    up_ref[CONV_PAD:, :] = u

    n_shift = shifted_ref.shape[1]
    for s in range(1, SUBLANES):
        shifted_ref[s - 1] = up_ref[s:s + n_shift, :]

    groups = CONV_ROWS // SUBLANES

    def conv_chunk(r):
        acc = [jnp.zeros((SUBLANES, d_conv), F32) for _ in range(groups)]
        for j in range(CONV_W):
            off = CONV_PAD - CONV_TAIL + j
            a, s = off // SUBLANES, off % SUBLANES
            wb = wb_ref[j]
            for g in range(groups):
                rows = slice(r + (a + g) * SUBLANES, r + (a + g + 1) * SUBLANES)
                win = up_ref[rows, :] if s == 0 else shifted_ref[s - 1, rows, :]
                acc[g] = acc[g] + win * wb
        for g in range(groups):
            conv_ref[r + g * SUBLANES:r + (g + 1) * SUBLANES, :] = acc[g]

    conv_starts = list(range(0, tile, CONV_ROWS))
    chunks_per_step = -(-len(conv_starts) // (n_blocks * N_PAIRS))

    q = _dot(h_in, win_ref[:, :d_attn]) * (HEAD_DIM ** -0.5 * LOG2E)
    g_a = _dot(h_in, win_ref[:, d_attn + 2 * d_kv:u_off])
    g_c = _dot(h_in, win_ref[:, u_off + 2 * d_conv:])
    lo = lax.broadcasted_iota(jnp.int32, (WINDOW, LANES), 1) < HALF
    zero = jnp.zeros((WINDOW, LANES), F32)

    def scores(blk, pair):
        r0 = blk * WINDOW
        first = jnp.where(t == 0, 1, 0) if blk == 0 else 0
        kvh = pair // PAIRS_PER_KV
        qp = q[r0:r0 + WINDOW, pair * LANES:(pair + 1) * LANES]
        rolled = pltpu.roll(qp, HALF, axis=1)
        if kvh == 0:
            q_even, q_odd = jnp.where(lo, qp, zero), jnp.where(lo, rolled, zero)
        else:
            q_even, q_odd = jnp.where(lo, zero, rolled), jnp.where(lo, zero, qp)
        qm = jnp.concatenate([q_even, q_odd], axis=0).astype(BF16)
        return _dot_nt(k_ref[r0:r0 + two_w, :], qm) + bias_ref[first, pair]

    def attend(blk, pair, s):
        r0 = blk * WINDOW
        kvh = pair // PAIRS_PER_KV
        sink = sink_ref[pair:pair + 1, :] * LOG2E
        m = jnp.maximum(jnp.max(s, axis=0, keepdims=True), sink)
        p = jnp.exp2(s - m)
        den = jnp.sum(p, axis=0, keepdims=True) + jnp.exp2(sink - m)
        vt_win = vt_ref[kvh * HEAD_DIM:(kvh + 1) * HEAD_DIM, r0:r0 + two_w]
        ot = _dot(vt_win, p.astype(BF16)) * (1.0 / den)
        o = jnp.concatenate([ot[:, :WINDOW], ot[:, WINDOW:]], axis=0).T
        attn_ref[r0:r0 + WINDOW, pair * LANES:(pair + 1) * LANES] = o

    steps = [(blk, pair) for blk in range(n_blocks) for pair in range(N_PAIRS)]
    s_next = scores(*steps[0])
    for i, step in enumerate(steps):
        s_cur = s_next
        if i + 1 < len(steps):
            s_next = scores(*steps[i + 1])
        attend(*step, s_cur)
        for _ in range(chunks_per_step):
            if conv_starts:
                conv_chunk(conv_starts.pop(0))

    k_ref[:WINDOW, :] = k_ref[tile:, :]
    vt_ref[:, :WINDOW] = vt_ref[:, tile:]

    while conv_starts:
        conv_chunk(conv_starts.pop(0))
    c2 = _conv_branch_tail(conv_ref[...] + dwb_ref[...], lng_ref[...], lnb_ref[...], wpw_ref)

    @pl.when(t == n_t - 1)
    def _states():
        kst_ref[...] = k[tile - WINDOW:, :]
        vst_ref[...] = v[tile - WINDOW:, :]
        cst_ref[...] = up_ref[CONV_PAD + tile - CONV_TAIL:, :]

    up_ref[:CONV_PAD, :] = up_ref[tile:, :]

    y_ref[...] = _mix_out(x, attn_ref[...], g_a, c2, g_c, wout_ref, fg_ref[...], d_attn, final_norm)


def _prompt_layer(x, sink_rows, ng, win, dww, dwb, lng, lnb, wpw, wout, fg, final_norm):
    B, T, D = x.shape
    tile = PROMPT_TILE
    assert T % tile == 0 and tile % WINDOW == 0 and tile % CONV_ROWS == 0
    d_conv = dww.shape[1]
    d_kv = N_KV_HEADS * HEAD_DIM
    d_attn = N_HEADS * HEAD_DIM
    const = lambda shape: pl.BlockSpec(shape, lambda b, t: (0,) * len(shape))
    return pl.pallas_call(
        functools.partial(_prompt_kernel, tile=tile, final_norm=final_norm),
        grid=(B, T // tile),
        in_specs=[
            pl.BlockSpec((None, tile, D), lambda b, t: (b, t, 0)),
            const(sink_rows.shape), const(ng.shape), const(win.shape), const(dww.shape), const(dwb.shape),
            const(lng.shape), const(lnb.shape), const(wpw.shape), const(wout.shape), const(fg.shape),
        ],
        out_specs=[
            pl.BlockSpec((None, tile, D), lambda b, t: (b, t, 0)),
            pl.BlockSpec((None, WINDOW, d_kv), lambda b, t: (b, 0, 0)),
            pl.BlockSpec((None, WINDOW, d_kv), lambda b, t: (b, 0, 0)),
            pl.BlockSpec((None, CONV_TAIL, d_conv), lambda b, t: (b, 0, 0)),
        ],
        scratch_shapes=[
            pltpu.VMEM((WINDOW + tile, d_kv), BF16),
            pltpu.VMEM((d_kv, WINDOW + tile), BF16),
            pltpu.VMEM((CONV_PAD + tile, d_conv), F32),
            pltpu.VMEM((SUBLANES - 1, CONV_PAD - SUBLANES + tile, d_conv), F32),
            pltpu.VMEM((CONV_W, SUBLANES, d_conv), F32),
            pltpu.VMEM((2, N_PAIRS, 2 * WINDOW, 2 * WINDOW), F32),
            pltpu.VMEM((tile, d_attn), F32),
            pltpu.VMEM((tile, d_conv), F32),
        ],
        out_shape=[
            jax.ShapeDtypeStruct((B, T, D), F32),
            jax.ShapeDtypeStruct((B, WINDOW, d_kv), F32),
            jax.ShapeDtypeStruct((B, WINDOW, d_kv), F32),
            jax.ShapeDtypeStruct((B, CONV_TAIL, d_conv), F32),
        ],
        compiler_params=pltpu.CompilerParams(
            dimension_semantics=("arbitrary", "arbitrary"), vmem_limit_bytes=VMEM_LIMIT_BYTES),
        name="prompt_layer",
    )(x, sink_rows, ng, win, dww, dwb, lng, lnb, wpw, wout, fg)


def _sample_kernel(x_ref, ck_ref, cv_ref, cs_ref, sink_ref, ng_ref, win_ref, dww_ref, dwb_ref,
                   lng_ref, lnb_ref, wpw_ref, wout_ref, fg_ref,
                   y_ref, ko_ref, vo_ref, co_ref,
                   q_ref, kn_ref, vn_ref, u_ref, ups_ref, bias_ref, attn_ref, conv_ref,
                   *, group, dec, final_norm):
    d_attn = N_HEADS * HEAD_DIM
    d_kv = N_KV_HEADS * HEAD_DIM
    d_conv = u_ref.shape[1]
    n_keys = WINDOW + dec
    rows_q = N_HEADS * dec

    @pl.when(pl.program_id(0) == 0)
    def _init():
        ri = lax.broadcasted_iota(jnp.int32, (rows_q, n_keys), 0)
        si = lax.broadcasted_iota(jnp.int32, (rows_q, n_keys), 1)
        ti = ri
        slope = jnp.zeros((rows_q, n_keys), F32)
        for h in range(N_HEADS):
            in_head = (ri >= h * dec) & (ri < (h + 1) * dec)
            ti = jnp.where(in_head, ri - h * dec, ti)
            slope = jnp.where(in_head, _alibi_slope(h), slope)
        dist = WINDOW + ti - si
        valid = (dist >= 0) & (dist < WINDOW)
        distf = dist.astype(F32)
        bias_ref[...] = jnp.where(valid, -slope * distf, NEG_INF)

    x = x_ref[...]
    h_in = _rms_norm(x, ng_ref[...]).astype(BF16)
    q_ref[...] = _dot(h_in, win_ref[:, :d_attn]) * (HEAD_DIM ** -0.5)
    kv = _dot(h_in, win_ref[:, d_attn:d_attn + 2 * d_kv])
    kn_ref[...] = kv[:, :d_kv]
    vn_ref[...] = kv[:, d_kv:]
    u_off = d_attn + 2 * d_kv + d_attn
    u_ab = _dot(h_in, win_ref[:, u_off:u_off + 2 * d_conv])
    u_ref[...] = u_ab[:, :d_conv] * jax.nn.sigmoid(u_ab[:, d_conv:])

    lo = lax.broadcasted_iota(jnp.int32, (dec, LANES), 1) < HALF
    zero = jnp.zeros((dec, LANES), F32)
    sink = sink_ref[...]
    bias = bias_ref[...]

    def per_element(b, carry):
        r = pl.multiple_of(b * dec, dec)
        k_new = kn_ref[pl.ds(r, dec), :]
        v_new = vn_ref[pl.ds(r, dec), :]
        k_all = jnp.concatenate([ck_ref[b], k_new], axis=0)
        v_all = jnp.concatenate([cv_ref[b], v_new], axis=0)
        ko_ref[b] = k_all[dec:, :]
        vo_ref[b] = v_all[dec:, :]

        q_b = q_ref[pl.ds(r, dec), :]
        rows = []
        for h in range(N_HEADS):
            tile_q = q_b[:, (h // 2) * LANES:(h // 2 + 1) * LANES]
            in_lo = h % 2 == 0
            want_lo = h // GQA_GROUP == 0
            src = tile_q if in_lo == want_lo else pltpu.roll(tile_q, HALF, axis=1)
            rows.append(jnp.where(lo, src, zero) if want_lo else jnp.where(lo, zero, src))
        q_rows = jnp.concatenate(rows, axis=0).astype(BF16)

        s = _dot_nt(q_rows, k_all.astype(BF16)) + bias
        p, rden = _softmax_sink(s, sink)
        o = _dot(p.astype(BF16), v_all.astype(BF16)) * rden

        tiles = []
        for pair in range(N_PAIRS):
            a = o[(2 * pair) * dec:(2 * pair + 1) * dec, :]
            c = o[(2 * pair + 1) * dec:(2 * pair + 2) * dec, :]
            if pair // PAIRS_PER_KV == 0:
                tiles.append(jnp.where(lo, a, pltpu.roll(c, HALF, axis=1)))
            else:
                tiles.append(jnp.where(lo, pltpu.roll(a, HALF, axis=1), c))
        attn_ref[pl.ds(r, dec), :] = jnp.concatenate(tiles, axis=1)

        ups_ref[CONV_PAD - CONV_TAIL:CONV_PAD, :] = cs_ref[b]
        ups_ref[CONV_PAD:, :] = u_ref[pl.ds(r, dec), :]
        acc = jnp.zeros((dec, d_conv), F32)
        for j in range(CONV_W):
            acc = acc + ups_ref[CONV_PAD - CONV_TAIL + j:CONV_PAD - CONV_TAIL + j + dec, :] * dww_ref[j:j + 1, :]
        conv_ref[pl.ds(r, dec), :] = acc
        co_ref[b] = ups_ref[CONV_PAD + dec - CONV_TAIL:, :]
        return carry

    lax.fori_loop(0, group, per_element, 0)

    c2 = _conv_branch_tail(conv_ref[...] + dwb_ref[...], lng_ref[...], lnb_ref[...], wpw_ref)
    g_a = _dot(h_in, win_ref[:, d_attn + 2 * d_kv:u_off])
    g_c = _dot(h_in, win_ref[:, u_off + 2 * d_conv:])
    y_ref[...] = _mix_out(x, attn_ref[...], g_a, c2, g_c, wout_ref, fg_ref[...], d_attn, final_norm)


def _sample_layer(x, ck, cv, cs, sink_col, ng, win, dww, dwb, lng, lnb, wpw, wout, fg, final_norm):
    Bs, dec, D = x.shape
    group = SAMPLE_GROUP
    assert Bs % group == 0 and dec % 8 == 0 and dec <= CONV_TAIL
    d_conv = dww.shape[1]
    d_kv = N_KV_HEADS * HEAD_DIM
    d_attn = N_HEADS * HEAD_DIM
    rows = group * dec
    x2 = x.reshape(Bs * dec, D)
    const = lambda shape: pl.BlockSpec(shape, lambda g: (0,) * len(shape))
    per_group = lambda shape: pl.BlockSpec((group,) + shape, lambda g: (g,) + (0,) * len(shape))
    y, ko, vo, co = pl.pallas_call(
        functools.partial(_sample_kernel, group=group, dec=dec, final_norm=final_norm),
        grid=(Bs // group,),
        in_specs=[
            pl.BlockSpec((rows, D), lambda g: (g, 0)),
            per_group((WINDOW, d_kv)), per_group((WINDOW, d_kv)), per_group((CONV_TAIL, d_conv)),
            const(sink_col.shape), const(ng.shape), const(win.shape), const(dww.shape), const(dwb.shape),
            const(lng.shape), const(lnb.shape), const(wpw.shape), const(wout.shape), const(fg.shape),
        ],
        out_specs=[
            pl.BlockSpec((rows, D), lambda g: (g, 0)),
            per_group((WINDOW, d_kv)), per_group((WINDOW, d_kv)), per_group((CONV_TAIL, d_conv)),
        ],
        out_shape=[
            jax.ShapeDtypeStruct((Bs * dec, D), F32),
            jax.ShapeDtypeStruct((Bs, WINDOW, d_kv), F32),
            jax.ShapeDtypeStruct((Bs, WINDOW, d_kv), F32),
            jax.ShapeDtypeStruct((Bs, CONV_TAIL, d_conv), F32),
        ],
        scratch_shapes=[
            pltpu.VMEM((rows, d_attn), F32),
            pltpu.VMEM((rows, d_kv), F32),
            pltpu.VMEM((rows, d_kv), F32),
            pltpu.VMEM((rows, d_conv), F32),
            pltpu.VMEM((CONV_PAD + dec, d_conv), F32),
            pltpu.VMEM((N_HEADS * dec, WINDOW + dec), F32),
            pltpu.VMEM((rows, d_attn), F32),
            pltpu.VMEM((rows, d_conv), F32),
        ],
        compiler_params=pltpu.CompilerParams(
            dimension_semantics=("arbitrary",), vmem_limit_bytes=VMEM_LIMIT_BYTES),
        name="sample_layer",
    )(x2, ck, cv, cs, sink_col, ng, win, dww, dwb, lng, lnb, wpw, wout, fg)
    return y.reshape(Bs, dec, D), ko, vo, co


def kernel(x_prompt, x_sample, cache_k, cache_v, state_conv, norm_g, w_in, attn_sinks, dw_w, dw_b,
           conv_ln_g, conv_ln_b, w_pw2, w_out, final_norm_g):
    depth = w_in.shape[0]
    B = x_prompt.shape[0]
    Bs, dec = x_sample.shape[0], x_sample.shape[1]
    d_kv = N_KV_HEADS * HEAD_DIM
    fg = final_norm_g.reshape(1, -1)
    hp, hs = x_prompt, x_sample
    pk, pv, pc, sk, sv, sc = [], [], [], [], [], []
    for l in range(depth):
        final_norm = l == depth - 1
        row = lambda a: a[l].reshape(1, -1)
        win = w_in[l].astype(BF16)
        wpw = w_pw2[l].astype(BF16)
        wout = w_out[l].astype(BF16)
        shared = (row(norm_g), win, dw_w[l], row(dw_b), row(conv_ln_g), row(conv_ln_b), wpw, wout, fg)
        sink_rows = jnp.repeat(attn_sinks[l], WINDOW).reshape(N_PAIRS, 2 * WINDOW)
        hp, k_p, v_p, c_p = _prompt_layer(hp, sink_rows, *shared, final_norm)
        sink_col = jnp.repeat(attn_sinks[l], dec).reshape(N_HEADS * dec, 1)
        hs, k_s, v_s, c_s = _sample_layer(
            hs, cache_k[l].reshape(Bs, WINDOW, d_kv), cache_v[l].reshape(Bs, WINDOW, d_kv),
            state_conv[l], sink_col, *shared, final_norm)
        pk.append(k_p.reshape(B, WINDOW, N_KV_HEADS, HEAD_DIM))
        pv.append(v_p.reshape(B, WINDOW, N_KV_HEADS, HEAD_DIM))
        pc.append(c_p)
        sk.append(k_s.reshape(Bs, WINDOW, N_KV_HEADS, HEAD_DIM))
        sv.append(v_s.reshape(Bs, WINDOW, N_KV_HEADS, HEAD_DIM))
        sc.append(c_s)
    return (hp, hs, jnp.stack(pk), jnp.stack(pv), jnp.stack(pc),
            jnp.stack(sk), jnp.stack(sv), jnp.stack(sc))
```

```python
import functools

import jax
import jax.numpy as jnp
from jax import lax
from jax.experimental import pallas as pl
from jax.experimental.pallas import tpu as pltpu

HEAD_DIM = 64
N_HEADS = 8
N_KV_HEADS = 2
GQA_GROUP = N_HEADS // N_KV_HEADS
N_PAIRS = N_HEADS // 2
PAIRS_PER_KV = N_PAIRS // N_KV_HEADS
WINDOW = 128
CONV_W = 31
CONV_TAIL = CONV_W - 1
EPS = 1e-5
LANES = 128
SUBLANES = 8
HALF = LANES // 2
NEG_INF = float("-inf")
LOG2E = 1.4426950408889634

PROMPT_TILE = 256
CONV_ROWS = 16
CONV_PAD = 32
SAMPLE_GROUP = 32
SAMPLE_UNROLL = 4
VMEM_LIMIT_BYTES = 56 * 1024 * 1024

F32 = jnp.float32
BF16 = jnp.bfloat16


def _alibi_slope(h):
    return 2.0 ** (-8.0 * (h + 1) / N_HEADS)


def _rms_norm(x, g):
    ms = jnp.mean(x * x, axis=-1, keepdims=True)
    return x * lax.rsqrt(ms + EPS) * g


def _silu(x):
    return x * jax.nn.sigmoid(x)


def _dot(a, b):
    return jnp.dot(a, b, preferred_element_type=F32)


def _dot_nt(a, b):
    return lax.dot_general(a, b, (((1,), (1,)), ((), ())), preferred_element_type=F32)


def _conv_branch_tail(c, lng, lnb, wpw_ref):
    mu = jnp.mean(c, axis=-1, keepdims=True)
    xc = c - mu
    var = jnp.mean(xc * xc, axis=-1, keepdims=True)
    y = xc * lax.rsqrt(var + EPS) * lng + lnb
    return _dot(_silu(y).astype(BF16), wpw_ref[...])


def _gate(val, g):
    return (val * _silu(g)).astype(BF16)


def _mix_out(x, m_a, m_c, wout_ref, fg, d_attn, final_norm):
    y = x + _dot(m_a, wout_ref[:d_attn, :]) + _dot(m_c, wout_ref[d_attn:, :])
    return _rms_norm(y, fg) if final_norm else y


def _softmax_sink(s, sink):
    m = jnp.maximum(jnp.max(s, axis=-1, keepdims=True), sink)
    p = jnp.exp(s - m)
    den = jnp.sum(p, axis=-1, keepdims=True) + jnp.exp(sink - m)
    return p, 1.0 / den


def _prompt_kernel(x_ref, sink_ref, ng_ref, win_ref, dww_ref, dwb_ref, lng_ref, lnb_ref,
                   wpw_ref, wout_ref, fg_ref,
                   y_ref, kst_ref, vst_ref, cst_ref,
                   k_ref, vt_ref, up_ref, shifted_ref, wb_ref, bias_ref, attn_ref, conv_ref,
                   *, tile, final_norm):
    t = pl.program_id(1)
    n_t = pl.num_programs(1)
    d_attn = N_HEADS * HEAD_DIM
    d_kv = N_KV_HEADS * HEAD_DIM
    d_conv = up_ref.shape[1]
    n_blocks = tile // WINDOW
    two_w = 2 * WINDOW

    @pl.when(t == 0)
    def _init():
        k_ref[:WINDOW, :] = jnp.zeros((WINDOW, d_kv), BF16)
        vt_ref[:, :WINDOW] = jnp.zeros((d_kv, WINDOW), BF16)
        up_ref[:CONV_PAD, :] = jnp.zeros((CONV_PAD, d_conv), F32)
        for j in range(CONV_W):
            wb_ref[j] = jnp.broadcast_to(dww_ref[j:j + 1, :], (SUBLANES, d_conv))
        ji = lax.broadcasted_iota(jnp.int32, (two_w, two_w), 0)
        ci = lax.broadcasted_iota(jnp.int32, (two_w, two_w), 1)
        odd = ci >= WINDOW
        dist = jnp.where(odd, ci - WINDOW, ci) + WINDOW - ji
        valid = (dist >= 0) & (dist < WINDOW)
        valid_first = valid & (ji >= WINDOW)
        distf = dist.astype(F32) * LOG2E
        for pair in range(N_PAIRS):
            pen = -jnp.where(odd, _alibi_slope(2 * pair + 1), _alibi_slope(2 * pair)) * distf
            bias_ref[0, pair] = jnp.where(valid, pen, NEG_INF)
            bias_ref[1, pair] = jnp.where(valid_first, pen, NEG_INF)

    x = x_ref[...]
    h_in = _rms_norm(x, ng_ref[...]).astype(BF16)

    kv = _dot(h_in, win_ref[:, d_attn:d_attn + 2 * d_kv])
    k = kv[:, :d_kv]
    v = kv[:, d_kv:]
    k_ref[WINDOW:, :] = k.astype(BF16)
    vt_ref[:, WINDOW:] = v.T.astype(BF16)

    u_off = d_attn + 2 * d_kv + d_attn
    u_ab = _dot(h_in, win_ref[:, u_off:u_off + 2 * d_conv])
    u = u_ab[:, :d_conv] * jax.nn.sigmoid(u_ab[:, d_conv:])
    up_ref[CONV_PAD:, :] = u

    n_shift = shifted_ref.shape[1]
    for s in range(1, SUBLANES):
        shifted_ref[s - 1] = up_ref[s:s + n_shift, :]

    groups = CONV_ROWS // SUBLANES

    def conv_chunk(r):
        acc = [jnp.zeros((SUBLANES, d_conv), F32) for _ in range(groups)]
        for j in range(CONV_W):
            off = CONV_PAD - CONV_TAIL + j
            a, s = off // SUBLANES, off % SUBLANES
            wb = wb_ref[j]
            for g in range(groups):
                rows = slice(r + (a + g) * SUBLANES, r + (a + g + 1) * SUBLANES)
                win = up_ref[rows, :] if s == 0 else shifted_ref[s - 1, rows, :]
                acc[g] = acc[g] + win * wb
        for g in range(groups):
            conv_ref[r + g * SUBLANES:r + (g + 1) * SUBLANES, :] = acc[g]

    conv_starts = list(range(0, tile, CONV_ROWS))
    chunks_per_step = -(-len(conv_starts) // (n_blocks * N_PAIRS))

    q = _dot(h_in, win_ref[:, :d_attn]) * (HEAD_DIM ** -0.5 * LOG2E)
    g_a = _dot(h_in, win_ref[:, d_attn + 2 * d_kv:u_off])
    g_c = _dot(h_in, win_ref[:, u_off + 2 * d_conv:])
    lo = lax.broadcasted_iota(jnp.int32, (WINDOW, LANES), 1) < HALF
    zero = jnp.zeros((WINDOW, LANES), F32)

    def scores(blk, pair):
        r0 = blk * WINDOW
        first = jnp.where(t == 0, 1, 0) if blk == 0 else 0
        kvh = pair // PAIRS_PER_KV
        qp = q[r0:r0 + WINDOW, pair * LANES:(pair + 1) * LANES]
        rolled = pltpu.roll(qp, HALF, axis=1)
        if kvh == 0:
            q_even, q_odd = jnp.where(lo, qp, zero), jnp.where(lo, rolled, zero)
        else:
            q_even, q_odd = jnp.where(lo, zero, rolled), jnp.where(lo, zero, qp)
        qm = jnp.concatenate([q_even, q_odd], axis=0).astype(BF16)
        return _dot_nt(k_ref[r0:r0 + two_w, :], qm) + bias_ref[first, pair]

    def attend(blk, pair, s):
        r0 = blk * WINDOW
        kvh = pair // PAIRS_PER_KV
        sink = sink_ref[pair:pair + 1, :] * LOG2E
        m = jnp.maximum(jnp.max(s, axis=0, keepdims=True), sink)
        p = jnp.exp2(s - m)
        den = jnp.sum(p, axis=0, keepdims=True) + jnp.exp2(sink - m)
        vt_win = vt_ref[kvh * HEAD_DIM:(kvh + 1) * HEAD_DIM, r0:r0 + two_w]
        ot = _dot(vt_win, p.astype(BF16)) * (1.0 / den)
        o = jnp.concatenate([ot[:, :WINDOW], ot[:, WINDOW:]], axis=0).T
        attn_ref[r0:r0 + WINDOW, pair * LANES:(pair + 1) * LANES] = o

    steps = [(blk, pair) for blk in range(n_blocks) for pair in range(N_PAIRS)]
    s_next = scores(*steps[0])
    for i, step in enumerate(steps):
        s_cur = s_next
        if i + 1 < len(steps):
            s_next = scores(*steps[i + 1])
        attend(*step, s_cur)
        for _ in range(chunks_per_step):
            if conv_starts:
                conv_chunk(conv_starts.pop(0))

    k_ref[:WINDOW, :] = k_ref[tile:, :]
    vt_ref[:, :WINDOW] = vt_ref[:, tile:]

    while conv_starts:
        conv_chunk(conv_starts.pop(0))
    c2 = _conv_branch_tail(conv_ref[...] + dwb_ref[...], lng_ref[...], lnb_ref[...], wpw_ref)

    @pl.when(t == n_t - 1)
    def _states():
        kst_ref[...] = k[tile - WINDOW:, :]
        vst_ref[...] = v[tile - WINDOW:, :]
        cst_ref[...] = up_ref[CONV_PAD + tile - CONV_TAIL:, :]

    up_ref[:CONV_PAD, :] = up_ref[tile:, :]

    y_ref[...] = _mix_out(x, _gate(attn_ref[...], g_a), _gate(c2, g_c), wout_ref, fg_ref[...],
                          d_attn, final_norm)


def _prompt_layer(x, sink_rows, ng, win, dww, dwb, lng, lnb, wpw, wout, fg, final_norm):
    B, T, D = x.shape
    tile = PROMPT_TILE
    assert T % tile == 0 and tile % WINDOW == 0 and tile % CONV_ROWS == 0
    d_conv = dww.shape[1]
    d_kv = N_KV_HEADS * HEAD_DIM
    d_attn = N_HEADS * HEAD_DIM
    const = lambda shape: pl.BlockSpec(shape, lambda b, t: (0,) * len(shape))
    return pl.pallas_call(
        functools.partial(_prompt_kernel, tile=tile, final_norm=final_norm),
        grid=(B, T // tile),
        in_specs=[
            pl.BlockSpec((None, tile, D), lambda b, t: (b, t, 0)),
            const(sink_rows.shape), const(ng.shape), const(win.shape), const(dww.shape), const(dwb.shape),
            const(lng.shape), const(lnb.shape), const(wpw.shape), const(wout.shape), const(fg.shape),
        ],
        out_specs=[
            pl.BlockSpec((None, tile, D), lambda b, t: (b, t, 0)),
            pl.BlockSpec((None, WINDOW, d_kv), lambda b, t: (b, 0, 0)),
            pl.BlockSpec((None, WINDOW, d_kv), lambda b, t: (b, 0, 0)),
            pl.BlockSpec((None, CONV_TAIL, d_conv), lambda b, t: (b, 0, 0)),
        ],
        scratch_shapes=[
            pltpu.VMEM((WINDOW + tile, d_kv), BF16),
            pltpu.VMEM((d_kv, WINDOW + tile), BF16),
            pltpu.VMEM((CONV_PAD + tile, d_conv), F32),
            pltpu.VMEM((SUBLANES - 1, CONV_PAD - SUBLANES + tile, d_conv), F32),
            pltpu.VMEM((CONV_W, SUBLANES, d_conv), F32),
            pltpu.VMEM((2, N_PAIRS, 2 * WINDOW, 2 * WINDOW), F32),
            pltpu.VMEM((tile, d_attn), F32),
            pltpu.VMEM((tile, d_conv), F32),
        ],
        out_shape=[
            jax.ShapeDtypeStruct((B, T, D), F32),
            jax.ShapeDtypeStruct((B, WINDOW, d_kv), F32),
            jax.ShapeDtypeStruct((B, WINDOW, d_kv), F32),
            jax.ShapeDtypeStruct((B, CONV_TAIL, d_conv), F32),
        ],
        compiler_params=pltpu.CompilerParams(
            dimension_semantics=("arbitrary", "arbitrary"), vmem_limit_bytes=VMEM_LIMIT_BYTES),
        name="prompt_layer",
    )(x, sink_rows, ng, win, dww, dwb, lng, lnb, wpw, wout, fg)


def _sample_kernel(x_ref, ck_ref, cv_ref, cs_ref, sink_ref, perm_ref, perm_t_ref, ng_ref, win_ref,
                   dww_ref, dwb_ref, lng_ref, lnb_ref, wpw_ref, wout_ref, fg_ref,
                   y_ref, ko_ref, vo_ref, co_ref,
                   q_ref, kn_ref, vn_ref, upt_ref, wb_ref, bias_ref, attn_ref, conv_ref,
                   *, group, dec, final_norm):
    d_attn = N_HEADS * HEAD_DIM
    d_kv = N_KV_HEADS * HEAD_DIM
    d_conv = conv_ref.shape[1]
    n_keys = WINDOW + dec
    rows_q = N_HEADS * dec

    @pl.when(pl.program_id(0) == 0)
    def _init():
        ri = lax.broadcasted_iota(jnp.int32, (rows_q, n_keys), 0)
        si = lax.broadcasted_iota(jnp.int32, (rows_q, n_keys), 1)
        ti = ri
        slope = jnp.zeros((rows_q, n_keys), F32)
        for h in range(N_HEADS):
            in_head = (ri >= h * dec) & (ri < (h + 1) * dec)
            ti = jnp.where(in_head, ri - h * dec, ti)
            slope = jnp.where(in_head, _alibi_slope(h), slope)
        dist = WINDOW + ti - si
        valid = (dist >= 0) & (dist < WINDOW)
        distf = dist.astype(F32)
        bias_ref[...] = jnp.where(valid, -slope * distf, NEG_INF)
        for j in range(CONV_W):
            wb_ref[j] = jnp.broadcast_to(dww_ref[j:j + 1, :], (SUBLANES, d_conv))

    x = x_ref[...]
    h_in = _rms_norm(x, ng_ref[...]).astype(BF16)
    q_ref[...] = _dot(h_in, win_ref[:, :d_attn]) * (HEAD_DIM ** -0.5)
    kv = _dot(h_in, win_ref[:, d_attn:d_attn + 2 * d_kv])
    kn_ref[...] = kv[:, :d_kv]
    vn_ref[...] = kv[:, d_kv:]
    g_a = _dot(h_in, win_ref[:, d_attn + 2 * d_kv:d_attn + 2 * d_kv + d_attn])

    h_tm = _dot(perm_ref[...], h_in).astype(BF16)
    u_off = d_attn + 2 * d_kv + d_attn
    u_ab = _dot(h_tm, win_ref[:, u_off:u_off + 2 * d_conv])
    u_tm = u_ab[:, :d_conv] * jax.nn.sigmoid(u_ab[:, d_conv:])
    g_c_tm = _dot(h_tm, win_ref[:, u_off + 2 * d_conv:])

    lo = lax.broadcasted_iota(jnp.int32, (dec, LANES), 1) < HALF
    zero = jnp.zeros((dec, LANES), F32)
    sink = sink_ref[...]
    bias = bias_ref[...]

    def scores(b):
        r = pl.multiple_of(b * dec, dec)
        k_all = jnp.concatenate([ck_ref[b], kn_ref[pl.ds(r, dec), :]], axis=0)
        ko_ref[b] = k_all[dec:, :]
        q_b = q_ref[pl.ds(r, dec), :]
        rows = []
        for h in range(N_HEADS):
            tile_q = q_b[:, (h // 2) * LANES:(h // 2 + 1) * LANES]
            in_lo = h % 2 == 0
            want_lo = h // GQA_GROUP == 0
            src = tile_q if in_lo == want_lo else pltpu.roll(tile_q, HALF, axis=1)
            rows.append(jnp.where(lo, src, zero) if want_lo else jnp.where(lo, zero, src))
        q_rows = jnp.concatenate(rows, axis=0).astype(BF16)
        return _dot_nt(q_rows, k_all.astype(BF16)) + bias

    def attend(b, s):
        r = pl.multiple_of(b * dec, dec)
        v_all = jnp.concatenate([cv_ref[b], vn_ref[pl.ds(r, dec), :]], axis=0)
        vo_ref[b] = v_all[dec:, :]
        p, rden = _softmax_sink(s, sink)
        o = _dot(p.astype(BF16), v_all.astype(BF16)) * rden
        tiles = []
        for pair in range(N_PAIRS):
            a = o[(2 * pair) * dec:(2 * pair + 1) * dec, :]
            c = o[(2 * pair + 1) * dec:(2 * pair + 2) * dec, :]
            if pair // PAIRS_PER_KV == 0:
                tiles.append(jnp.where(lo, a, pltpu.roll(c, HALF, axis=1)))
            else:
                tiles.append(jnp.where(lo, pltpu.roll(a, HALF, axis=1), c))
        attn_ref[pl.ds(r, dec), :] = jnp.concatenate(tiles, axis=1)

    def attention_batch(i, carry):
        elems = [i * SAMPLE_UNROLL + e for e in range(SAMPLE_UNROLL)]
        ss = [scores(b) for b in elems]
        for b, s in zip(elems, ss):
            attend(b, s)
        return carry

    lax.fori_loop(0, group // SAMPLE_UNROLL, attention_batch, 0)

    for tau in range(CONV_TAIL):
        upt_ref[tau] = cs_ref[:, tau, :]
    for tt in range(dec):
        upt_ref[CONV_TAIL + tt] = u_tm[tt * group:(tt + 1) * group, :]
    for tau in range(CONV_TAIL):
        co_ref[:, tau, :] = upt_ref[dec + tau]
    for tt in range(dec):
        for g0 in range(0, group, SUBLANES):
            acc = jnp.zeros((SUBLANES, d_conv), F32)
            for j in range(CONV_W):
                acc = acc + upt_ref[tt + j, g0:g0 + SUBLANES, :] * wb_ref[j]
            conv_ref[tt * group + g0:tt * group + g0 + SUBLANES, :] = acc

    c2_tm = _conv_branch_tail(conv_ref[...] + dwb_ref[...], lng_ref[...], lnb_ref[...], wpw_ref)
    m_c = _dot(perm_t_ref[...], _gate(c2_tm, g_c_tm)).astype(BF16)
    y_ref[...] = _mix_out(x, _gate(attn_ref[...], g_a), m_c, wout_ref, fg_ref[...], d_attn, final_norm)


def _sample_layer(x, ck, cv, cs, sink_col, ng, win, dww, dwb, lng, lnb, wpw, wout, fg, final_norm):
    Bs, dec, D = x.shape
    group = SAMPLE_GROUP
    assert Bs % group == 0 and dec % 8 == 0 and dec <= CONV_TAIL
    d_conv = dww.shape[1]
    d_kv = N_KV_HEADS * HEAD_DIM
    d_attn = N_HEADS * HEAD_DIM
    rows = group * dec
    x2 = x.reshape(Bs * dec, D)
    perm = jnp.eye(rows, dtype=BF16).reshape(group, dec, rows).swapaxes(0, 1).reshape(rows, rows)
    perm_t = perm.T
    const = lambda shape: pl.BlockSpec(shape, lambda g: (0,) * len(shape))
    per_group = lambda shape: pl.BlockSpec((group,) + shape, lambda g: (g,) + (0,) * len(shape))
    y, ko, vo, co = pl.pallas_call(
        functools.partial(_sample_kernel, group=group, dec=dec, final_norm=final_norm),
        grid=(Bs // group,),
        in_specs=[
            pl.BlockSpec((rows, D), lambda g: (g, 0)),
            per_group((WINDOW, d_kv)), per_group((WINDOW, d_kv)), per_group((CONV_TAIL, d_conv)),
            const(sink_col.shape), const(perm.shape), const(perm_t.shape),
            const(ng.shape), const(win.shape), const(dww.shape), const(dwb.shape),
            const(lng.shape), const(lnb.shape), const(wpw.shape), const(wout.shape), const(fg.shape),
        ],
        out_specs=[
            pl.BlockSpec((rows, D), lambda g: (g, 0)),
            per_group((WINDOW, d_kv)), per_group((WINDOW, d_kv)), per_group((CONV_TAIL, d_conv)),
        ],
        out_shape=[
            jax.ShapeDtypeStruct((Bs * dec, D), F32),
            jax.ShapeDtypeStruct((Bs, WINDOW, d_kv), F32),
            jax.ShapeDtypeStruct((Bs, WINDOW, d_kv), F32),
            jax.ShapeDtypeStruct((Bs, CONV_TAIL, d_conv), F32),
        ],
        scratch_shapes=[
            pltpu.VMEM((rows, d_attn), F32),
            pltpu.VMEM((rows, d_kv), F32),
            pltpu.VMEM((rows, d_kv), F32),
            pltpu.VMEM((CONV_TAIL + dec, group, d_conv), F32),
            pltpu.VMEM((CONV_W, SUBLANES, d_conv), F32),
            pltpu.VMEM((N_HEADS * dec, WINDOW + dec), F32),
            pltpu.VMEM((rows, d_attn), F32),
            pltpu.VMEM((rows, d_conv), F32),
        ],
        compiler_params=pltpu.CompilerParams(
            dimension_semantics=("arbitrary",), vmem_limit_bytes=VMEM_LIMIT_BYTES),
        name="sample_layer",
    )(x2, ck, cv, cs, sink_col, perm, perm_t, ng, win, dww, dwb, lng, lnb, wpw, wout, fg)
    return y.reshape(Bs, dec, D), ko, vo, co


def kernel(x_prompt, x_sample, cache_k, cache_v, state_conv, norm_g, w_in, attn_sinks, dw_w, dw_b,
           conv_ln_g, conv_ln_b, w_pw2, w_out, final_norm_g):
    depth = w_in.shape[0]
    B = x_prompt.shape[0]
    Bs, dec = x_sample.shape[0], x_sample.shape[1]
    d_kv = N_KV_HEADS * HEAD_DIM
    fg = final_norm_g.reshape(1, -1)
    hp, hs = x_prompt, x_sample
    pk, pv, pc, sk, sv, sc = [], [], [], [], [], []
    for l in range(depth):
        final_norm = l == depth - 1
        row = lambda a: a[l].reshape(1, -1)
        win = w_in[l].astype(BF16)
        wpw = w_pw2[l].astype(BF16)
        wout = w_out[l].astype(BF16)
        shared = (row(norm_g), win, dw_w[l], row(dw_b), row(conv_ln_g), row(conv_ln_b), wpw, wout, fg)
        sink_rows = jnp.repeat(attn_sinks[l], WINDOW).reshape(N_PAIRS, 2 * WINDOW)
        hp, k_p, v_p, c_p = _prompt_layer(hp, sink_rows, *shared, final_norm)
        sink_col = jnp.repeat(attn_sinks[l], dec).reshape(N_HEADS * dec, 1)
        hs, k_s, v_s, c_s = _sample_layer(
            hs, cache_k[l].reshape(Bs, WINDOW, d_kv), cache_v[l].reshape(Bs, WINDOW, d_kv),
            state_conv[l], sink_col, *shared, final_norm)
        pk.append(k_p.reshape(B, WINDOW, N_KV_HEADS, HEAD_DIM))
        pv.append(v_p.reshape(B, WINDOW, N_KV_HEADS, HEAD_DIM))
        pc.append(c_p)
        sk.append(k_s.reshape(Bs, WINDOW, N_KV_HEADS, HEAD_DIM))
        sv.append(v_s.reshape(Bs, WINDOW, N_KV_HEADS, HEAD_DIM))
        sc.append(c_s)
    return (hp, hs, jnp.stack(pk), jnp.stack(pv), jnp.stack(pc),
            jnp.stack(sk), jnp.stack(sv), jnp.stack(sc))
```

```python
import functools

import jax
import jax.numpy as jnp
from jax import lax
from jax.experimental import pallas as pl
from jax.experimental.pallas import tpu as pltpu

HEAD_DIM = 64
N_HEADS = 8
N_KV_HEADS = 2
GQA_GROUP = N_HEADS // N_KV_HEADS
N_PAIRS = N_HEADS // 2
PAIRS_PER_KV = N_PAIRS // N_KV_HEADS
WINDOW = 128
CONV_W = 31
CONV_TAIL = CONV_W - 1
EPS = 1e-5
LANES = 128
SUBLANES = 8
HALF = LANES // 2
NEG_INF = float("-inf")
LOG2E = 1.4426950408889634

PROMPT_TILE = 512
CONV_ROWS = 32
CONV_PAD = 32
SAMPLE_GROUP = 32
SAMPLE_UNROLL = 4
VMEM_LIMIT_BYTES = 56 * 1024 * 1024

F32 = jnp.float32
BF16 = jnp.bfloat16


def _alibi_slope(h):
    return 2.0 ** (-8.0 * (h + 1) / N_HEADS)


def _rms_norm(x, g):
    ms = jnp.mean(x * x, axis=-1, keepdims=True)
    return x * lax.rsqrt(ms + EPS) * g


def _silu(x):
    return x * jax.nn.sigmoid(x)


def _dot(a, b):
    return jnp.dot(a, b, preferred_element_type=F32)


def _dot_nt(a, b):
    return lax.dot_general(a, b, (((1,), (1,)), ((), ())), preferred_element_type=F32)


def _conv_branch_tail(c, lng, lnb, wpw_ref):
    mu = jnp.mean(c, axis=-1, keepdims=True)
    xc = c - mu
    var = jnp.mean(xc * xc, axis=-1, keepdims=True)
    y = xc * lax.rsqrt(var + EPS) * lng + lnb
    return _dot(_silu(y).astype(BF16), wpw_ref[...])


def _gate(val, g):
    return (val * _silu(g)).astype(BF16)


def _mix_out(x, m_a, m_c, wout_ref, fg, d_attn, final_norm):
    y = x + _dot(m_a, wout_ref[:d_attn, :]) + _dot(m_c, wout_ref[d_attn:, :])
    return _rms_norm(y, fg) if final_norm else y


def _softmax_sink(s, sink):
    m = jnp.maximum(jnp.max(s, axis=-1, keepdims=True), sink)
    p = jnp.exp(s - m)
    den = jnp.sum(p, axis=-1, keepdims=True) + jnp.exp(sink - m)
    return p, 1.0 / den


def _prompt_kernel(x_ref, sink_ref, ng_ref, win_ref, dww_ref, dwb_ref, lng_ref, lnb_ref,
                   wpw_ref, wout_ref, fg_ref,
                   y_ref, kst_ref, vst_ref, cst_ref,
                   k_ref, vt_ref, up_ref, shifted_ref, wb_ref, bias_ref, attn_ref, conv_ref,
                   *, tile, final_norm):
    t = pl.program_id(1)
    d_attn = N_HEADS * HEAD_DIM
    d_kv = N_KV_HEADS * HEAD_DIM
    d_conv = up_ref.shape[1]
    n_blocks = tile // WINDOW
    two_w = 2 * WINDOW
    u_off = d_attn + 2 * d_kv + d_attn
    n_shift = shifted_ref.shape[1]

    @pl.when(t == 0)
    def _init():
        k_ref[:WINDOW, :] = jnp.zeros((WINDOW, d_kv), BF16)
        vt_ref[:, :WINDOW] = jnp.zeros((d_kv, WINDOW), BF16)
        up_ref[:CONV_PAD, :] = jnp.zeros((CONV_PAD, d_conv), F32)
        for j in range(CONV_W):
            wb_ref[j] = jnp.broadcast_to(dww_ref[j:j + 1, :], (SUBLANES, d_conv))
        ji = lax.broadcasted_iota(jnp.int32, (two_w, two_w), 0)
        ci = lax.broadcasted_iota(jnp.int32, (two_w, two_w), 1)
        odd = ci >= WINDOW
        dist = jnp.where(odd, ci - WINDOW, ci) + WINDOW - ji
        valid = (dist >= 0) & (dist < WINDOW)
        valid_first = valid & (ji >= WINDOW)
        distf = dist.astype(F32) * LOG2E
        for pair in range(N_PAIRS):
            pen = -jnp.where(odd, _alibi_slope(2 * pair + 1), _alibi_slope(2 * pair)) * distf
            bias_ref[0, pair] = jnp.where(valid, pen, NEG_INF)
            bias_ref[1, pair] = jnp.where(valid_first, pen, NEG_INF)

    x = x_ref[...]
    h_in = _rms_norm(x, ng_ref[...]).astype(BF16)

    kv = _dot(h_in, win_ref[:, d_attn:d_attn + 2 * d_kv])
    k = kv[:, :d_kv]
    v = kv[:, d_kv:]
    k_ref[WINDOW:, :] = k.astype(BF16)
    vt_ref[:, WINDOW:] = v.T.astype(BF16)
    kst_ref[...] = k[tile - WINDOW:, :]
    vst_ref[...] = v[tile - WINDOW:, :]

    u_ab = _dot(h_in, win_ref[:, u_off:u_off + 2 * d_conv])
    up_ref[CONV_PAD:, :] = u_ab[:, :d_conv] * jax.nn.sigmoid(u_ab[:, d_conv:])
    cst_ref[...] = up_ref[CONV_PAD + tile - CONV_TAIL:, :]
    for s in range(1, SUBLANES):
        shifted_ref[s - 1] = up_ref[s:s + n_shift, :]

    groups = CONV_ROWS // SUBLANES

    def conv_chunk(r):
        acc = [jnp.zeros((SUBLANES, d_conv), F32) for _ in range(groups)]
        loaded = {}
        for j in range(CONV_W):
            off = CONV_PAD - CONV_TAIL + j
            a, s = off // SUBLANES, off % SUBLANES
            wb = wb_ref[j]
            for g in range(groups):
                if (s, a + g) not in loaded:
                    rows = slice(r + (a + g) * SUBLANES, r + (a + g + 1) * SUBLANES)
                    loaded[s, a + g] = up_ref[rows, :] if s == 0 else shifted_ref[s - 1, rows, :]
                acc[g] = acc[g] + loaded[s, a + g] * wb
        for g in range(groups):
            conv_ref[r + g * SUBLANES:r + (g + 1) * SUBLANES, :] = acc[g]

    conv_starts = list(range(0, tile, CONV_ROWS))
    chunks_per_step = -(-len(conv_starts) // (n_blocks * N_PAIRS))

    q = _dot(h_in, win_ref[:, :d_attn]) * (HEAD_DIM ** -0.5 * LOG2E)
    g_a = _dot(h_in, win_ref[:, d_attn + 2 * d_kv:u_off])
    g_c = _dot(h_in, win_ref[:, u_off + 2 * d_conv:])

    lo = lax.broadcasted_iota(jnp.int32, (WINDOW, LANES), 1) < HALF
    zero = jnp.zeros((WINDOW, LANES), F32)

    def scores(blk, pair):
        r0 = blk * WINDOW
        first = jnp.where(t == 0, 1, 0) if blk == 0 else 0
        kvh = pair // PAIRS_PER_KV
        qp = q[r0:r0 + WINDOW, pair * LANES:(pair + 1) * LANES]
        rolled = pltpu.roll(qp, HALF, axis=1)
        if kvh == 0:
            q_even, q_odd = jnp.where(lo, qp, zero), jnp.where(lo, rolled, zero)
        else:
            q_even, q_odd = jnp.where(lo, zero, rolled), jnp.where(lo, zero, qp)
        qm = jnp.concatenate([q_even, q_odd], axis=0).astype(BF16)
        return _dot_nt(k_ref[r0:r0 + two_w, :], qm) + bias_ref[first, pair]

    def attend(blk, pair, s):
        r0 = blk * WINDOW
        kvh = pair // PAIRS_PER_KV
        sink = sink_ref[pair:pair + 1, :] * LOG2E
        m = jnp.maximum(jnp.max(s, axis=0, keepdims=True), sink)
        p = jnp.exp2(s - m)
        den = jnp.sum(p, axis=0, keepdims=True) + jnp.exp2(sink - m)
        vt_win = vt_ref[kvh * HEAD_DIM:(kvh + 1) * HEAD_DIM, r0:r0 + two_w]
        ot = _dot(vt_win, p.astype(BF16)) * (1.0 / den)
        o = jnp.concatenate([ot[:, :WINDOW], ot[:, WINDOW:]], axis=0).T
        attn_ref[r0:r0 + WINDOW, pair * LANES:(pair + 1) * LANES] = o

    steps = [(blk, pair) for blk in range(n_blocks) for pair in range(N_PAIRS)]
    s_next = scores(*steps[0])
    for i, step in enumerate(steps):
        s_cur = s_next
        if i + 1 < len(steps):
            s_next = scores(*steps[i + 1])
        attend(*step, s_cur)
        for _ in range(chunks_per_step):
            if conv_starts:
                conv_chunk(conv_starts.pop(0))
    while conv_starts:
        conv_chunk(conv_starts.pop(0))

    c2 = _conv_branch_tail(conv_ref[...] + dwb_ref[...], lng_ref[...], lnb_ref[...], wpw_ref)
    y_ref[...] = _mix_out(x, _gate(attn_ref[...], g_a), _gate(c2, g_c), wout_ref, fg_ref[...],
                          d_attn, final_norm)

    k_ref[:WINDOW, :] = k_ref[tile:, :]
    vt_ref[:, :WINDOW] = vt_ref[:, tile:]
    up_ref[:CONV_PAD, :] = up_ref[tile:, :]


def _prompt_layer(x, sink_rows, ng, win, dww, dwb, lng, lnb, wpw, wout, fg, final_norm):
    B, T, D = x.shape
    tile = PROMPT_TILE
    assert T % tile == 0 and tile % WINDOW == 0 and tile % CONV_ROWS == 0
    d_conv = dww.shape[1]
    d_kv = N_KV_HEADS * HEAD_DIM
    d_attn = N_HEADS * HEAD_DIM
    n_t = T // tile
    const = lambda shape: pl.BlockSpec(shape, lambda b, t: (0,) * len(shape))
    return pl.pallas_call(
        functools.partial(_prompt_kernel, tile=tile, final_norm=final_norm),
        grid=(B, n_t),
        in_specs=[
            pl.BlockSpec((None, tile, D), lambda b, t: (b, t, 0)),
            const(sink_rows.shape), const(ng.shape), const(win.shape), const(dww.shape), const(dwb.shape),
            const(lng.shape), const(lnb.shape), const(wpw.shape), const(wout.shape), const(fg.shape),
        ],
        out_specs=[
            pl.BlockSpec((None, tile, D), lambda b, t: (b, t, 0)),
            pl.BlockSpec((None, WINDOW, d_kv), lambda b, t: (b, 0, 0)),
            pl.BlockSpec((None, WINDOW, d_kv), lambda b, t: (b, 0, 0)),
            pl.BlockSpec((None, CONV_TAIL, d_conv), lambda b, t: (b, 0, 0)),
        ],
        scratch_shapes=[
            pltpu.VMEM((WINDOW + tile, d_kv), BF16),
            pltpu.VMEM((d_kv, WINDOW + tile), BF16),
            pltpu.VMEM((CONV_PAD + tile, d_conv), F32),
            pltpu.VMEM((SUBLANES - 1, CONV_PAD - SUBLANES + tile, d_conv), F32),
            pltpu.VMEM((CONV_W, SUBLANES, d_conv), F32),
            pltpu.VMEM((2, N_PAIRS, 2 * WINDOW, 2 * WINDOW), F32),
            pltpu.VMEM((tile, d_attn), F32),
            pltpu.VMEM((tile, d_conv), F32),
        ],
        out_shape=[
            jax.ShapeDtypeStruct((B, T, D), F32),
            jax.ShapeDtypeStruct((B, WINDOW, d_kv), F32),
            jax.ShapeDtypeStruct((B, WINDOW, d_kv), F32),
            jax.ShapeDtypeStruct((B, CONV_TAIL, d_conv), F32),
        ],
        compiler_params=pltpu.CompilerParams(
            dimension_semantics=("arbitrary", "arbitrary"), vmem_limit_bytes=VMEM_LIMIT_BYTES),
        name="prompt_layer",
    )(x, sink_rows, ng, win, dww, dwb, lng, lnb, wpw, wout, fg)


def _sample_kernel(x_ref, ck_ref, cv_ref, cs_ref, sink_ref, perm_ref, perm_t_ref, ng_ref, win_ref,
                   dww_ref, dwb_ref, lng_ref, lnb_ref, wpw_ref, wout_ref, fg_ref,
                   y_ref, ko_ref, vo_ref, co_ref,
                   q_ref, kn_ref, vn_ref, upt_ref, wb_ref, bias_ref, attn_ref, conv_ref,
                   *, group, dec, final_norm):
    d_attn = N_HEADS * HEAD_DIM
    d_kv = N_KV_HEADS * HEAD_DIM
    d_conv = conv_ref.shape[1]
    n_keys = WINDOW + dec
    rows_q = N_HEADS * dec

    @pl.when(pl.program_id(0) == 0)
    def _init():
        ri = lax.broadcasted_iota(jnp.int32, (rows_q, n_keys), 0)
        si = lax.broadcasted_iota(jnp.int32, (rows_q, n_keys), 1)
        ti = ri
        slope = jnp.zeros((rows_q, n_keys), F32)
        for h in range(N_HEADS):
            in_head = (ri >= h * dec) & (ri < (h + 1) * dec)
            ti = jnp.where(in_head, ri - h * dec, ti)
            slope = jnp.where(in_head, _alibi_slope(h), slope)
        dist = WINDOW + ti - si
        valid = (dist >= 0) & (dist < WINDOW)
        distf = dist.astype(F32)
        bias_ref[...] = jnp.where(valid, -slope * distf, NEG_INF)
        for j in range(CONV_W):
            wb_ref[j] = jnp.broadcast_to(dww_ref[j:j + 1, :], (SUBLANES, d_conv))

    x = x_ref[...]
    h_in = _rms_norm(x, ng_ref[...]).astype(BF16)
    q_ref[...] = _dot(h_in, win_ref[:, :d_attn]) * (HEAD_DIM ** -0.5)
    kv = _dot(h_in, win_ref[:, d_attn:d_attn + 2 * d_kv])
    kn_ref[...] = kv[:, :d_kv]
    vn_ref[...] = kv[:, d_kv:]
    g_a = _dot(h_in, win_ref[:, d_attn + 2 * d_kv:d_attn + 2 * d_kv + d_attn])

    h_tm = _dot(perm_ref[...], h_in).astype(BF16)
    u_off = d_attn + 2 * d_kv + d_attn
    u_ab = _dot(h_tm, win_ref[:, u_off:u_off + 2 * d_conv])
    u_tm = u_ab[:, :d_conv] * jax.nn.sigmoid(u_ab[:, d_conv:])
    g_c_tm = _dot(h_tm, win_ref[:, u_off + 2 * d_conv:])

    lo = lax.broadcasted_iota(jnp.int32, (dec, LANES), 1) < HALF
    zero = jnp.zeros((dec, LANES), F32)
    sink = sink_ref[...]
    bias = bias_ref[...]

    def scores(b):
        r = pl.multiple_of(b * dec, dec)
        k_all = jnp.concatenate([ck_ref[b], kn_ref[pl.ds(r, dec), :]], axis=0)
        ko_ref[b] = k_all[dec:, :]
        q_b = q_ref[pl.ds(r, dec), :]
        rows = []
        for h in range(N_HEADS):
            tile_q = q_b[:, (h // 2) * LANES:(h // 2 + 1) * LANES]
            in_lo = h % 2 == 0
            want_lo = h // GQA_GROUP == 0
            src = tile_q if in_lo == want_lo else pltpu.roll(tile_q, HALF, axis=1)
            rows.append(jnp.where(lo, src, zero) if want_lo else jnp.where(lo, zero, src))
        q_rows = jnp.concatenate(rows, axis=0).astype(BF16)
        return _dot_nt(q_rows, k_all.astype(BF16)) + bias

    def attend(b, s):
        r = pl.multiple_of(b * dec, dec)
        v_all = jnp.concatenate([cv_ref[b], vn_ref[pl.ds(r, dec), :]], axis=0)
        vo_ref[b] = v_all[dec:, :]
        p, rden = _softmax_sink(s, sink)
        o = _dot(p.astype(BF16), v_all.astype(BF16)) * rden
        tiles = []
        for pair in range(N_PAIRS):
            a = o[(2 * pair) * dec:(2 * pair + 1) * dec, :]
            c = o[(2 * pair + 1) * dec:(2 * pair + 2) * dec, :]
            if pair // PAIRS_PER_KV == 0:
                tiles.append(jnp.where(lo, a, pltpu.roll(c, HALF, axis=1)))
            else:
                tiles.append(jnp.where(lo, pltpu.roll(a, HALF, axis=1), c))
        attn_ref[pl.ds(r, dec), :] = jnp.concatenate(tiles, axis=1)

    def attention_batch(i, carry):
        elems = [i * SAMPLE_UNROLL + e for e in range(SAMPLE_UNROLL)]
        ss = [scores(b) for b in elems]
        for b, s in zip(elems, ss):
            attend(b, s)
        return carry

    lax.fori_loop(0, group // SAMPLE_UNROLL, attention_batch, 0)

    for tau in range(CONV_TAIL):
        upt_ref[tau] = cs_ref[:, tau, :]
    for tt in range(dec):
        upt_ref[CONV_TAIL + tt] = u_tm[tt * group:(tt + 1) * group, :]
    for tau in range(CONV_TAIL):
        co_ref[:, tau, :] = upt_ref[dec + tau]
    for tt in range(dec):
        for g0 in range(0, group, SUBLANES):
            acc = jnp.zeros((SUBLANES, d_conv), F32)
            for j in range(CONV_W):
                acc = acc + upt_ref[tt + j, g0:g0 + SUBLANES, :] * wb_ref[j]
            conv_ref[tt * group + g0:tt * group + g0 + SUBLANES, :] = acc

    c2_tm = _conv_branch_tail(conv_ref[...] + dwb_ref[...], lng_ref[...], lnb_ref[...], wpw_ref)
    m_c = _dot(perm_t_ref[...], _gate(c2_tm, g_c_tm)).astype(BF16)
    y_ref[...] = _mix_out(x, _gate(attn_ref[...], g_a), m_c, wout_ref, fg_ref[...], d_attn, final_norm)


def _sample_layer(x, ck, cv, cs, sink_col, ng, win, dww, dwb, lng, lnb, wpw, wout, fg, final_norm):
    Bs, dec, D = x.shape
    group = SAMPLE_GROUP
    assert Bs % group == 0 and dec % 8 == 0 and dec <= CONV_TAIL
    d_conv = dww.shape[1]
    d_kv = N_KV_HEADS * HEAD_DIM
    d_attn = N_HEADS * HEAD_DIM
    rows = group * dec
    x2 = x.reshape(Bs * dec, D)
    perm = jnp.eye(rows, dtype=BF16).reshape(group, dec, rows).swapaxes(0, 1).reshape(rows, rows)
    perm_t = perm.T
    const = lambda shape: pl.BlockSpec(shape, lambda g: (0,) * len(shape))
    per_group = lambda shape: pl.BlockSpec((group,) + shape, lambda g: (g,) + (0,) * len(shape))
    y, ko, vo, co = pl.pallas_call(
        functools.partial(_sample_kernel, group=group, dec=dec, final_norm=final_norm),
        grid=(Bs // group,),
        in_specs=[
            pl.BlockSpec((rows, D), lambda g: (g, 0)),
            per_group((WINDOW, d_kv)), per_group((WINDOW, d_kv)), per_group((CONV_TAIL, d_conv)),
            const(sink_col.shape), const(perm.shape), const(perm_t.shape),
            const(ng.shape), const(win.shape), const(dww.shape), const(dwb.shape),
            const(lng.shape), const(lnb.shape), const(wpw.shape), const(wout.shape), const(fg.shape),
        ],
        out_specs=[
            pl.BlockSpec((rows, D), lambda g: (g, 0)),
            per_group((WINDOW, d_kv)), per_group((WINDOW, d_kv)), per_group((CONV_TAIL, d_conv)),
        ],
        out_shape=[
            jax.ShapeDtypeStruct((Bs * dec, D), F32),
            jax.ShapeDtypeStruct((Bs, WINDOW, d_kv), F32),
            jax.ShapeDtypeStruct((Bs, WINDOW, d_kv), F32),
            jax.ShapeDtypeStruct((Bs, CONV_TAIL, d_conv), F32),
        ],
        scratch_shapes=[
            pltpu.VMEM((rows, d_attn), F32),
            pltpu.VMEM((rows, d_kv), F32),
            pltpu.VMEM((rows, d_kv), F32),
            pltpu.VMEM((CONV_TAIL + dec, group, d_conv), F32),
            pltpu.VMEM((CONV_W, SUBLANES, d_conv), F32),
            pltpu.VMEM((N_HEADS * dec, WINDOW + dec), F32),
            pltpu.VMEM((rows, d_attn), F32),
            pltpu.VMEM((rows, d_conv), F32),
        ],
        compiler_params=pltpu.CompilerParams(
            dimension_semantics=("arbitrary",), vmem_limit_bytes=VMEM_LIMIT_BYTES),
        name="sample_layer",
    )(x2, ck, cv, cs, sink_col, perm, perm_t, ng, win, dww, dwb, lng, lnb, wpw, wout, fg)
    return y.reshape(Bs, dec, D), ko, vo, co


def kernel(x_prompt, x_sample, cache_k, cache_v, state_conv, norm_g, w_in, attn_sinks, dw_w, dw_b,
           conv_ln_g, conv_ln_b, w_pw2, w_out, final_norm_g):
    depth = w_in.shape[0]
    B = x_prompt.shape[0]
    Bs, dec = x_sample.shape[0], x_sample.shape[1]
    d_kv = N_KV_HEADS * HEAD_DIM
    fg = final_norm_g.reshape(1, -1)
    hp, hs = x_prompt, x_sample
    pk, pv, pc, sk, sv, sc = [], [], [], [], [], []
    for l in range(depth):
        final_norm = l == depth - 1
        row = lambda a: a[l].reshape(1, -1)
        win = w_in[l].astype(BF16)
        wpw = w_pw2[l].astype(BF16)
        wout = w_out[l].astype(BF16)
        shared = (row(norm_g), win, dw_w[l], row(dw_b), row(conv_ln_g), row(conv_ln_b), wpw, wout, fg)
        sink_rows = jnp.repeat(attn_sinks[l], WINDOW).reshape(N_PAIRS, 2 * WINDOW)
        hp, k_p, v_p, c_p = _prompt_layer(hp, sink_rows, *shared, final_norm)
        sink_col = jnp.repeat(attn_sinks[l], dec).reshape(N_HEADS * dec, 1)
        hs, k_s, v_s, c_s = _sample_layer(
            hs, cache_k[l].reshape(Bs, WINDOW, d_kv), cache_v[l].reshape(Bs, WINDOW, d_kv),
            state_conv[l], sink_col, *shared, final_norm)
        pk.append(k_p.reshape(B, WINDOW, N_KV_HEADS, HEAD_DIM))
        pv.append(v_p.reshape(B, WINDOW, N_KV_HEADS, HEAD_DIM))
        pc.append(c_p)
        sk.append(k_s.reshape(Bs, WINDOW, N_KV_HEADS, HEAD_DIM))
        sv.append(v_s.reshape(Bs, WINDOW, N_KV_HEADS, HEAD_DIM))
        sc.append(c_s)
    return (hp, hs, jnp.stack(pk), jnp.stack(pv), jnp.stack(pc),
            jnp.stack(sk), jnp.stack(sv), jnp.stack(sc))
```

```python
import functools

import jax
import jax.numpy as jnp
from jax import lax
from jax.experimental import pallas as pl
from jax.experimental.pallas import tpu as pltpu

HEAD_DIM = 64
N_HEADS = 8
N_KV_HEADS = 2
GQA_GROUP = N_HEADS // N_KV_HEADS
N_PAIRS = N_HEADS // 2
PAIRS_PER_KV = N_PAIRS // N_KV_HEADS
WINDOW = 128
CONV_W = 31
CONV_TAIL = CONV_W - 1
EPS = 1e-5
LANES = 128
SUBLANES = 8
HALF = LANES // 2
NEG_INF = float("-inf")
LOG2E = 1.4426950408889634

PROMPT_TILE = 512
CONV_ROWS = 32
CONV_PAD = 32
SAMPLE_GROUP = 32
SAMPLE_UNROLL = 4
VMEM_LIMIT_BYTES = 56 * 1024 * 1024

F32 = jnp.float32
BF16 = jnp.bfloat16


def _alibi_slope(h):
    return 2.0 ** (-8.0 * (h + 1) / N_HEADS)


def _rms_norm(x, g):
    ms = jnp.mean(x * x, axis=-1, keepdims=True)
    return x * lax.rsqrt(ms + EPS) * g


def _silu(x):
    return x * jax.nn.sigmoid(x)


def _dot(a, b):
    return jnp.dot(a, b, preferred_element_type=F32)


def _dot_nt(a, b):
    return lax.dot_general(a, b, (((1,), (1,)), ((), ())), preferred_element_type=F32)


def _conv_branch_tail(c, lng, lnb, wpw_ref):
    mu = jnp.mean(c, axis=-1, keepdims=True)
    xc = c - mu
    var = jnp.mean(xc * xc, axis=-1, keepdims=True)
    y = xc * lax.rsqrt(var + EPS) * lng + lnb
    return _dot(_silu(y).astype(BF16), wpw_ref[...])


def _gate(val, g):
    return (val * _silu(g)).astype(BF16)


def _mix_out(x, m_a, m_c, wout_ref, fg, d_attn, final_norm):
    y = x + _dot(m_a, wout_ref[:d_attn, :]) + _dot(m_c, wout_ref[d_attn:, :])
    return _rms_norm(y, fg) if final_norm else y


def _softmax_sink(s, sink):
    m = jnp.maximum(jnp.max(s, axis=-1, keepdims=True), sink)
    p = jnp.exp(s - m)
    den = jnp.sum(p, axis=-1, keepdims=True) + jnp.exp(sink - m)
    return p, 1.0 / den


def _prompt_kernel(x_ref, sink_ref, ng_ref, win_ref, dww_ref, dwb_ref, lng_ref, lnb_ref,
                   wpw_ref, wout_ref, fg_ref,
                   y_ref, kst_ref, vst_ref, cst_ref,
                   k_ref, vt_ref, up_ref, shifted_ref, wb_ref, bias_ref, attn_ref, conv_ref,
                   *, tile, final_norm):
    t = pl.program_id(1)
    d_attn = N_HEADS * HEAD_DIM
    d_kv = N_KV_HEADS * HEAD_DIM
    d_conv = up_ref.shape[1]
    n_blocks = tile // WINDOW
    two_w = 2 * WINDOW
    u_off = d_attn + 2 * d_kv + d_attn
    n_shift = shifted_ref.shape[1]

    @pl.when(t == 0)
    def _init():
        k_ref[:WINDOW, :] = jnp.zeros((WINDOW, d_kv), BF16)
        vt_ref[:, :WINDOW] = jnp.zeros((d_kv, WINDOW), BF16)
        up_ref[:CONV_PAD, :] = jnp.zeros((CONV_PAD, d_conv), F32)
        for j in range(CONV_W):
            wb_ref[j] = jnp.broadcast_to(dww_ref[j:j + 1, :], (SUBLANES, d_conv))
        ji = lax.broadcasted_iota(jnp.int32, (two_w, two_w), 0)
        ci = lax.broadcasted_iota(jnp.int32, (two_w, two_w), 1)
        odd = ci >= WINDOW
        dist = jnp.where(odd, ci - WINDOW, ci) + WINDOW - ji
        valid = (dist >= 0) & (dist < WINDOW)
        valid_first = valid & (ji >= WINDOW)
        distf = dist.astype(F32) * LOG2E
        for pair in range(N_PAIRS):
            pen = -jnp.where(odd, _alibi_slope(2 * pair + 1), _alibi_slope(2 * pair)) * distf
            bias_ref[0, pair] = jnp.where(valid, pen, NEG_INF)
            bias_ref[1, pair] = jnp.where(valid_first, pen, NEG_INF)

    x = x_ref[...]
    h_in = _rms_norm(x, ng_ref[...]).astype(BF16)

    kv = _dot(h_in, win_ref[:, d_attn:d_attn + 2 * d_kv])
    k = kv[:, :d_kv]
    v = kv[:, d_kv:]
    k_ref[WINDOW:, :] = k.astype(BF16)
    vt_ref[:, WINDOW:] = v.T.astype(BF16)
    kst_ref[...] = k[tile - WINDOW:, :]
    vst_ref[...] = v[tile - WINDOW:, :]

    u_ab = _dot(h_in, win_ref[:, u_off:u_off + 2 * d_conv])
    up_ref[CONV_PAD:, :] = u_ab[:, :d_conv] * jax.nn.sigmoid(u_ab[:, d_conv:])
    cst_ref[...] = up_ref[CONV_PAD + tile - CONV_TAIL:, :]
    for s in range(1, SUBLANES):
        shifted_ref[s - 1] = up_ref[s:s + n_shift, :]

    groups = CONV_ROWS // SUBLANES

    def conv_chunk(r):
        acc = [jnp.zeros((SUBLANES, d_conv), F32) for _ in range(groups)]
        loaded = {}
        for j in range(CONV_W):
            off = CONV_PAD - CONV_TAIL + j
            a, s = off // SUBLANES, off % SUBLANES
            wb = wb_ref[j]
            for g in range(groups):
                if (s, a + g) not in loaded:
                    rows = slice(r + (a + g) * SUBLANES, r + (a + g + 1) * SUBLANES)
                    loaded[s, a + g] = up_ref[rows, :] if s == 0 else shifted_ref[s - 1, rows, :]
                acc[g] = acc[g] + loaded[s, a + g] * wb
        for g in range(groups):
            conv_ref[r + g * SUBLANES:r + (g + 1) * SUBLANES, :] = acc[g]

    conv_starts = list(range(0, tile, CONV_ROWS))
    chunks_per_step = -(-len(conv_starts) // (n_blocks * N_PAIRS))

    q = _dot(h_in, win_ref[:, :d_attn]) * (HEAD_DIM ** -0.5 * LOG2E)
    g_a = _dot(h_in, win_ref[:, d_attn + 2 * d_kv:u_off])
    g_c = _dot(h_in, win_ref[:, u_off + 2 * d_conv:])

    lo = lax.broadcasted_iota(jnp.int32, (WINDOW, LANES), 1) < HALF
    zero = jnp.zeros((WINDOW, LANES), F32)
    odd_head = lax.broadcasted_iota(jnp.int32, (1, two_w), 1) >= WINDOW

    def scores(blk, pair):
        r0 = blk * WINDOW
        first = jnp.where(t == 0, 1, 0) if blk == 0 else 0
        kvh = pair // PAIRS_PER_KV
        qp = q[r0:r0 + WINDOW, pair * LANES:(pair + 1) * LANES]
        rolled = pltpu.roll(qp, HALF, axis=1)
        if kvh == 0:
            q_even, q_odd = jnp.where(lo, qp, zero), jnp.where(lo, rolled, zero)
        else:
            q_even, q_odd = jnp.where(lo, zero, rolled), jnp.where(lo, zero, qp)
        qm = jnp.concatenate([q_even, q_odd], axis=0).astype(BF16)
        return _dot_nt(k_ref[r0:r0 + two_w, :], qm) + bias_ref[first, pair]

    def attend(blk, pair, s):
        r0 = blk * WINDOW
        kvh = pair // PAIRS_PER_KV
        sink = jnp.where(odd_head, sink_ref[2 * pair + 1], sink_ref[2 * pair]) * LOG2E
        m = jnp.maximum(jnp.max(s, axis=0, keepdims=True), sink)
        p = jnp.exp2(s - m)
        den = jnp.sum(p, axis=0, keepdims=True) + jnp.exp2(sink - m)
        vt_win = vt_ref[kvh * HEAD_DIM:(kvh + 1) * HEAD_DIM, r0:r0 + two_w]
        ot = _dot(vt_win, p.astype(BF16)) * (1.0 / den)
        o = jnp.concatenate([ot[:, :WINDOW], ot[:, WINDOW:]], axis=0).T
        attn_ref[r0:r0 + WINDOW, pair * LANES:(pair + 1) * LANES] = o

    steps = [(blk, pair) for blk in range(n_blocks) for pair in range(N_PAIRS)]
    s_next = scores(*steps[0])
    for i, step in enumerate(steps):
        s_cur = s_next
        if i + 1 < len(steps):
            s_next = scores(*steps[i + 1])
        attend(*step, s_cur)
        for _ in range(chunks_per_step):
            if conv_starts:
                conv_chunk(conv_starts.pop(0))
    while conv_starts:
        conv_chunk(conv_starts.pop(0))

    c2 = _conv_branch_tail(conv_ref[...] + dwb_ref[...], lng_ref[...], lnb_ref[...], wpw_ref)
    y_ref[...] = _mix_out(x, _gate(attn_ref[...], g_a), _gate(c2, g_c), wout_ref, fg_ref[...],
                          d_attn, final_norm)

    k_ref[:WINDOW, :] = k_ref[tile:, :]
    vt_ref[:, :WINDOW] = vt_ref[:, tile:]
    up_ref[:CONV_PAD, :] = up_ref[tile:, :]


def _prompt_layer(x, sinks, ng, win, dww, dwb, lng, lnb, wpw, wout, fg, final_norm):
    B, T, D = x.shape
    tile = PROMPT_TILE
    assert T % tile == 0 and tile % WINDOW == 0 and tile % CONV_ROWS == 0
    d_conv = dww.shape[1]
    d_kv = N_KV_HEADS * HEAD_DIM
    d_attn = N_HEADS * HEAD_DIM
    n_t = T // tile
    const = lambda shape: pl.BlockSpec(shape, lambda b, t: (0,) * len(shape))
    return pl.pallas_call(
        functools.partial(_prompt_kernel, tile=tile, final_norm=final_norm),
        grid=(B, n_t),
        in_specs=[
            pl.BlockSpec((None, tile, D), lambda b, t: (b, t, 0)),
            pl.BlockSpec(memory_space=pltpu.SMEM),
            const(ng.shape), const(win.shape), const(dww.shape), const(dwb.shape),
            const(lng.shape), const(lnb.shape), const(wpw.shape), const(wout.shape), const(fg.shape),
        ],
        out_specs=[
            pl.BlockSpec((None, tile, D), lambda b, t: (b, t, 0)),
            pl.BlockSpec((None, WINDOW, d_kv), lambda b, t: (b, 0, 0)),
            pl.BlockSpec((None, WINDOW, d_kv), lambda b, t: (b, 0, 0)),
            pl.BlockSpec((None, CONV_TAIL, d_conv), lambda b, t: (b, 0, 0)),
        ],
        scratch_shapes=[
            pltpu.VMEM((WINDOW + tile, d_kv), BF16),
            pltpu.VMEM((d_kv, WINDOW + tile), BF16),
            pltpu.VMEM((CONV_PAD + tile, d_conv), F32),
            pltpu.VMEM((SUBLANES - 1, CONV_PAD - SUBLANES + tile, d_conv), F32),
            pltpu.VMEM((CONV_W, SUBLANES, d_conv), F32),
            pltpu.VMEM((2, N_PAIRS, 2 * WINDOW, 2 * WINDOW), F32),
            pltpu.VMEM((tile, d_attn), F32),
            pltpu.VMEM((tile, d_conv), F32),
        ],
        out_shape=[
            jax.ShapeDtypeStruct((B, T, D), F32),
            jax.ShapeDtypeStruct((B, WINDOW, d_kv), F32),
            jax.ShapeDtypeStruct((B, WINDOW, d_kv), F32),
            jax.ShapeDtypeStruct((B, CONV_TAIL, d_conv), F32),
        ],
        compiler_params=pltpu.CompilerParams(
            dimension_semantics=("arbitrary", "arbitrary"), vmem_limit_bytes=VMEM_LIMIT_BYTES),
        name="prompt_layer",
    )(x, sinks, ng, win, dww, dwb, lng, lnb, wpw, wout, fg)


def _sample_kernel(x_ref, ck_ref, cv_ref, cs_ref, sink_ref, perm_ref, perm_t_ref, ng_ref, win_ref,
                   dww_ref, dwb_ref, lng_ref, lnb_ref, wpw_ref, wout_ref, fg_ref,
                   y_ref, ko_ref, vo_ref, co_ref,
                   q_ref, kn_ref, vn_ref, upt_ref, wb_ref, bias_ref, sinkcol_ref, attn_ref, conv_ref,
                   *, group, dec, final_norm):
    d_attn = N_HEADS * HEAD_DIM
    d_kv = N_KV_HEADS * HEAD_DIM
    d_conv = conv_ref.shape[1]
    n_keys = WINDOW + dec
    rows_q = N_HEADS * dec

    @pl.when(pl.program_id(0) == 0)
    def _init():
        ri = lax.broadcasted_iota(jnp.int32, (rows_q, n_keys), 0)
        si = lax.broadcasted_iota(jnp.int32, (rows_q, n_keys), 1)
        ti = ri
        slope = jnp.zeros((rows_q, n_keys), F32)
        for h in range(N_HEADS):
            in_head = (ri >= h * dec) & (ri < (h + 1) * dec)
            ti = jnp.where(in_head, ri - h * dec, ti)
            slope = jnp.where(in_head, _alibi_slope(h), slope)
        dist = WINDOW + ti - si
        valid = (dist >= 0) & (dist < WINDOW)
        distf = dist.astype(F32)
        bias_ref[...] = jnp.where(valid, -slope * distf, NEG_INF)
        rc = lax.broadcasted_iota(jnp.int32, (rows_q, 1), 0)
        sink_rows = jnp.zeros((rows_q, 1), F32)
        for h in range(N_HEADS):
            sink_rows = jnp.where((rc >= h * dec) & (rc < (h + 1) * dec), sink_ref[h], sink_rows)
        sinkcol_ref[...] = sink_rows
        for j in range(CONV_W):
            wb_ref[j] = jnp.broadcast_to(dww_ref[j:j + 1, :], (SUBLANES, d_conv))

    x = x_ref[...]
    h_in = _rms_norm(x, ng_ref[...]).astype(BF16)
    q_ref[...] =_dot(h_in, win_ref[:, :d_attn]) * (HEAD_DIM ** -0.5)
    kv = _dot(h_in, win_ref[:, d_attn:d_attn + 2 * d_kv])
    kn_ref[...] = kv[:, :d_kv]
    vn_ref[...] = kv[:, d_kv:]
    g_a = _dot(h_in, win_ref[:, d_attn + 2 * d_kv:d_attn + 2 * d_kv + d_attn])

    h_tm = _dot(perm_ref[...], h_in).astype(BF16)
    u_off = d_attn + 2 * d_kv + d_attn
    u_ab = _dot(h_tm, win_ref[:, u_off:u_off + 2 * d_conv])
    u_tm = u_ab[:, :d_conv] * jax.nn.sigmoid(u_ab[:, d_conv:])
    g_c_tm = _dot(h_tm, win_ref[:, u_off + 2 * d_conv:])

    lo = lax.broadcasted_iota(jnp.int32, (dec, LANES), 1) < HALF
    zero = jnp.zeros((dec, LANES), F32)
    sink = sinkcol_ref[...]
    bias = bias_ref[...]

    def scores(b):
        r = pl.multiple_of(b * dec, dec)
        k_all = jnp.concatenate([ck_ref[b], kn_ref[pl.ds(r, dec), :]], axis=0)
        ko_ref[b] = k_all[dec:, :]
        q_b = q_ref[pl.ds(r, dec), :]
        rows = []
        for h in range(N_HEADS):
            tile_q = q_b[:, (h // 2) * LANES:(h // 2 + 1) * LANES]
            in_lo = h % 2 == 0
            want_lo = h // GQA_GROUP == 0
            src = tile_q if in_lo == want_lo else pltpu.roll(tile_q, HALF, axis=1)
            rows.append(jnp.where(lo, src, zero) if want_lo else jnp.where(lo, zero, src))
        q_rows = jnp.concatenate(rows, axis=0).astype(BF16)
        return _dot_nt(q_rows, k_all.astype(BF16)) + bias

    def attend(b, s):
        r = pl.multiple_of(b * dec, dec)
        v_all = jnp.concatenate([cv_ref[b], vn_ref[pl.ds(r, dec), :]], axis=0)
        vo_ref[b] = v_all[dec:, :]
        p, rden = _softmax_sink(s, sink)
        o = _dot(p.astype(BF16), v_all.astype(BF16)) * rden
        tiles = []
        for pair in range(N_PAIRS):
            a = o[(2 * pair) * dec:(2 * pair + 1) * dec, :]
            c = o[(2 * pair + 1) * dec:(2 * pair + 2) * dec, :]
            if pair // PAIRS_PER_KV == 0:
                tiles.append(jnp.where(lo, a, pltpu.roll(c, HALF, axis=1)))
            else:
                tiles.append(jnp.where(lo, pltpu.roll(a, HALF, axis=1), c))
        attn_ref[pl.ds(r, dec), :] = jnp.concatenate(tiles, axis=1)

    for tau in range(CONV_TAIL):
        upt_ref[tau] = cs_ref[:, tau, :]
    for tt in range(dec):
        upt_ref[CONV_TAIL + tt] = u_tm[tt * group:(tt + 1) * group, :]
    for tau in range(CONV_TAIL):
        co_ref[:, tau, :] = upt_ref[dec + tau]

    def conv_step(tt):
        for g0 in range(0, group, SUBLANES):
            acc = jnp.zeros((SUBLANES, d_conv), F32)
            for j in range(CONV_W):
                acc = acc + upt_ref[tt + j, g0:g0 + SUBLANES, :] * wb_ref[j]
            start = tt * group + g0
            if not isinstance(start, int):
                start = pl.multiple_of(start, SUBLANES)
            conv_ref[pl.ds(start, SUBLANES), :] = acc

    trips = group // SAMPLE_UNROLL
    conv_per_trip = dec // trips

    def attention_batch(i, carry):
        elems = [i * SAMPLE_UNROLL + e for e in range(SAMPLE_UNROLL)]
        ss = [scores(b) for b in elems]
        for b, s in zip(elems, ss):
            attend(b, s)
        for c in range(conv_per_trip):
            conv_step(i * conv_per_trip + c)
        return carry

    lax.fori_loop(0, trips, attention_batch, 0)
    for tt in range(trips * conv_per_trip, dec):
        conv_step(tt)

    c2_tm = _conv_branch_tail(conv_ref[...] + dwb_ref[...], lng_ref[...], lnb_ref[...], wpw_ref)
    m_c = _dot(perm_t_ref[...], _gate(c2_tm, g_c_tm)).astype(BF16)
    y_ref[...] = _mix_out(x, _gate(attn_ref[...], g_a), m_c, wout_ref, fg_ref[...], d_attn, final_norm)


def _sample_layer(x, ck, cv, cs, sinks, ng, win, dww, dwb, lng, lnb, wpw, wout, fg, final_norm):
    Bs, dec, D = x.shape
    group = SAMPLE_GROUP
    assert Bs % group == 0 and dec % 8 == 0 and dec <= CONV_TAIL
    d_conv = dww.shape[1]
    d_kv = N_KV_HEADS * HEAD_DIM
    d_attn = N_HEADS * HEAD_DIM
    rows = group * dec
    x2 = x.reshape(Bs * dec, D)
    perm = jnp.eye(rows, dtype=BF16).reshape(group, dec, rows).swapaxes(0, 1).reshape(rows, rows)
    perm_t = perm.T
    const = lambda shape: pl.BlockSpec(shape, lambda g: (0,) * len(shape))
    per_group = lambda shape: pl.BlockSpec((group,) + shape, lambda g: (g,) + (0,) * len(shape))
    y, ko, vo, co = pl.pallas_call(
        functools.partial(_sample_kernel, group=group, dec=dec, final_norm=final_norm),
        grid=(Bs // group,),
        in_specs=[
            pl.BlockSpec((rows, D), lambda g: (g, 0)),
            per_group((WINDOW, d_kv)), per_group((WINDOW, d_kv)), per_group((CONV_TAIL, d_conv)),
            pl.BlockSpec(memory_space=pltpu.SMEM), const(perm.shape), const(perm_t.shape),
            const(ng.shape), const(win.shape), const(dww.shape), const(dwb.shape),
            const(lng.shape), const(lnb.shape), const(wpw.shape), const(wout.shape), const(fg.shape),
        ],
        out_specs=[
            pl.BlockSpec((rows, D), lambda g: (g, 0)),
            per_group((WINDOW, d_kv)), per_group((WINDOW, d_kv)), per_group((CONV_TAIL, d_conv)),
        ],
        out_shape=[
            jax.ShapeDtypeStruct((Bs * dec, D), F32),
            jax.ShapeDtypeStruct((Bs, WINDOW, d_kv), F32),
            jax.ShapeDtypeStruct((Bs, WINDOW, d_kv), F32),
            jax.ShapeDtypeStruct((Bs, CONV_TAIL, d_conv), F32),
        ],
        scratch_shapes=[
            pltpu.VMEM((rows, d_attn), F32),
            pltpu.VMEM((rows, d_kv), F32),
            pltpu.VMEM((rows, d_kv), F32),
            pltpu.VMEM((CONV_TAIL + dec, group, d_conv), F32),
            pltpu.VMEM((CONV_W, SUBLANES, d_conv), F32),
            pltpu.VMEM((N_HEADS * dec, WINDOW + dec), F32),
            pltpu.VMEM((N_HEADS * dec, 1), F32),
            pltpu.VMEM((rows, d_attn), F32),
            pltpu.VMEM((rows, d_conv), F32),
        ],
        compiler_params=pltpu.CompilerParams(
            dimension_semantics=("arbitrary",), vmem_limit_bytes=VMEM_LIMIT_BYTES),
        name="sample_layer",
    )(x2, ck, cv, cs, sinks, perm, perm_t, ng, win, dww, dwb, lng, lnb, wpw, wout, fg)
    return y.reshape(Bs, dec, D), ko, vo, co


def kernel(x_prompt, x_sample, cache_k, cache_v, state_conv, norm_g, w_in, attn_sinks, dw_w, dw_b,
           conv_ln_g, conv_ln_b, w_pw2, w_out, final_norm_g):
    depth = w_in.shape[0]
    B = x_prompt.shape[0]
    Bs, dec = x_sample.shape[0], x_sample.shape[1]
    d_kv = N_KV_HEADS * HEAD_DIM
    fg = final_norm_g.reshape(1, -1)
    hp, hs = x_prompt, x_sample
    pk, pv, pc, sk, sv, sc = [], [], [], [], [], []
    for l in range(depth):
        final_norm = l == depth - 1
        row = lambda a: a[l].reshape(1, -1)
        win = w_in[l].astype(BF16)
        wpw = w_pw2[l].astype(BF16)
        wout = w_out[l].astype(BF16)
        shared = (row(norm_g), win, dw_w[l], row(dw_b), row(conv_ln_g), row(conv_ln_b), wpw, wout, fg)
        hp, k_p, v_p, c_p = _prompt_layer(hp, attn_sinks[l], *shared, final_norm)
        hs, k_s, v_s, c_s = _sample_layer(
            hs, cache_k[l].reshape(Bs, WINDOW, d_kv), cache_v[l].reshape(Bs, WINDOW, d_kv),
            state_conv[l], attn_sinks[l], *shared, final_norm)
        pk.append(k_p.reshape(B, WINDOW, N_KV_HEADS, HEAD_DIM))
        pv.append(v_p.reshape(B, WINDOW, N_KV_HEADS, HEAD_DIM))
        pc.append(c_p)
        sk.append(k_s.reshape(Bs, WINDOW, N_KV_HEADS, HEAD_DIM))
        sv.append(v_s.reshape(Bs, WINDOW, N_KV_HEADS, HEAD_DIM))
        sc.append(c_s)
    stack = lambda xs: xs[0][None] if len(xs) == 1 else jnp.stack(xs)
    return (hp, hs, stack(pk), stack(pv), stack(pc), stack(sk), stack(sv), stack(sc))
```

```python
import functools

import jax
import jax.numpy as jnp
from jax import lax
from jax.experimental import pallas as pl
from jax.experimental.pallas import tpu as pltpu

HEAD_DIM = 64
N_HEADS = 8
N_KV_HEADS = 2
GQA_GROUP = N_HEADS // N_KV_HEADS
N_PAIRS = N_HEADS // 2
PAIRS_PER_KV = N_PAIRS // N_KV_HEADS
WINDOW = 128
CONV_W = 31
CONV_TAIL = CONV_W - 1
EPS = 1e-5
LANES = 128
SUBLANES = 8
HALF = LANES // 2
NEG_INF = float("-inf")
LOG2E = 1.4426950408889634

PROMPT_TILE = 512
CONV_ROWS = 32
CONV_PAD = 32
SAMPLE_GROUP = 32
SAMPLE_UNROLL = 4
VMEM_LIMIT_BYTES = 56 * 1024 * 1024

F32 = jnp.float32
BF16 = jnp.bfloat16


def _alibi_slope(h):
    return 2.0 ** (-8.0 * (h + 1) / N_HEADS)


def _rms_norm(x, g):
    ms = jnp.mean(x * x, axis=-1, keepdims=True)
    return x * lax.rsqrt(ms + EPS) * g


def _silu(x):
    return x * jax.nn.sigmoid(x)


def _dot(a, b):
    return jnp.dot(a, b, preferred_element_type=F32)


def _dot_nt(a, b):
    return lax.dot_general(a, b, (((1,), (1,)), ((), ())), preferred_element_type=F32)


def _conv_branch_tail(c, lng, lnb, wpw_ref):
    mu = jnp.mean(c, axis=-1, keepdims=True)
    xc = c - mu
    var = jnp.mean(xc * xc, axis=-1, keepdims=True)
    y = xc * lax.rsqrt(var + EPS) * lng + lnb
    return _dot(_silu(y).astype(BF16), wpw_ref[...])


def _gate(val, g):
    return (val * _silu(g)).astype(BF16)


def _mix_out(x, m_a, m_c, wout_ref, fg, d_attn, final_norm):
    y = x + _dot(m_a, wout_ref[:d_attn, :]) + _dot(m_c, wout_ref[d_attn:, :])
    return _rms_norm(y, fg) if final_norm else y


def _softmax_sink(s, sink):
    m = jnp.maximum(jnp.max(s, axis=-1, keepdims=True), sink)
    p = jnp.exp(s - m)
    den = jnp.sum(p, axis=-1, keepdims=True) + jnp.exp(sink - m)
    return p, 1.0 / den


def _prompt_kernel(x_ref, sink_ref, ng_ref, win_ref, dww_ref, dwb_ref, lng_ref, lnb_ref,
                   wpw_ref, wout_ref, fg_ref,
                   y_ref, kst_ref, vst_ref, cst_ref,
                   k_ref, vt_ref, up_ref, shifted_ref, wb_ref, bias_ref, attn_ref, conv_ref,
                   *, tile, final_norm):
    t = pl.program_id(1)
    d_attn = N_HEADS * HEAD_DIM
    d_kv = N_KV_HEADS * HEAD_DIM
    d_conv = up_ref.shape[1]
    n_blocks = tile // WINDOW
    two_w = 2 * WINDOW
    u_off = d_attn + 2 * d_kv + d_attn
    n_shift = shifted_ref.shape[1]

    @pl.when(t == 0)
    def _init():
        k_ref[:WINDOW, :] = jnp.zeros((WINDOW, d_kv), BF16)
        vt_ref[:, :WINDOW] = jnp.zeros((d_kv, WINDOW), BF16)
        up_ref[:CONV_PAD, :] = jnp.zeros((CONV_PAD, d_conv), F32)
        for j in range(CONV_W):
            wb_ref[j] = jnp.broadcast_to(dww_ref[j:j + 1, :], (SUBLANES, d_conv))
        ji = lax.broadcasted_iota(jnp.int32, (two_w, two_w), 0)
        ci = lax.broadcasted_iota(jnp.int32, (two_w, two_w), 1)
        odd = ci >= WINDOW
        dist = jnp.where(odd, ci - WINDOW, ci) + WINDOW - ji
        valid = (dist >= 0) & (dist < WINDOW)
        valid_first = valid & (ji >= WINDOW)
        distf = dist.astype(F32) * LOG2E
        for pair in range(N_PAIRS):
            pen = -jnp.where(odd, _alibi_slope(2 * pair + 1), _alibi_slope(2 * pair)) * distf
            bias_ref[0, pair] = jnp.where(valid, pen, NEG_INF)
            bias_ref[1, pair] = jnp.where(valid_first, pen, NEG_INF)

    x = x_ref[...]
    h_in = _rms_norm(x, ng_ref[...]).astype(BF16)

    kv = _dot(h_in, win_ref[:, d_attn:d_attn + 2 * d_kv])
    k = kv[:, :d_kv]
    v = kv[:, d_kv:]
    k_ref[WINDOW:, :] = k.astype(BF16)
    vt_ref[:, WINDOW:] = v.T.astype(BF16)
    kst_ref[...] = k[tile - WINDOW:, :]
    vst_ref[...] = v[tile - WINDOW:, :]

    u_ab = _dot(h_in, win_ref[:, u_off:u_off + 2 * d_conv])
    up_ref[CONV_PAD:, :] = u_ab[:, :d_conv] * jax.nn.sigmoid(u_ab[:, d_conv:])
    cst_ref[...] = up_ref[CONV_PAD + tile - CONV_TAIL:, :]
    for s in range(1, SUBLANES):
        shifted_ref[s - 1] = up_ref[s:s + n_shift, :]

    groups = CONV_ROWS // SUBLANES

    def conv_chunk(r):
        acc = [jnp.zeros((SUBLANES, d_conv), F32) for _ in range(groups)]
        loaded = {}
        for j in range(CONV_W):
            off = CONV_PAD - CONV_TAIL + j
            a, s = off // SUBLANES, off % SUBLANES
            wb = wb_ref[j]
            for g in range(groups):
                if (s, a + g) not in loaded:
                    rows = slice(r + (a + g) * SUBLANES, r + (a + g + 1) * SUBLANES)
                    loaded[s, a + g] = up_ref[rows, :] if s == 0 else shifted_ref[s - 1, rows, :]
                acc[g] = acc[g] + loaded[s, a + g] * wb
        for g in range(groups):
            conv_ref[r + g * SUBLANES:r + (g + 1) * SUBLANES, :] = acc[g]

    conv_starts = list(range(0, tile, CONV_ROWS))
    chunks_per_step = -(-len(conv_starts) // (n_blocks * N_PAIRS))

    q = _dot(h_in, win_ref[:, :d_attn]) * (HEAD_DIM ** -0.5 * LOG2E)
    g_a = _dot(h_in, win_ref[:, d_attn + 2 * d_kv:u_off])
    g_c = _dot(h_in, win_ref[:, u_off + 2 * d_conv:])

    lo = lax.broadcasted_iota(jnp.int32, (WINDOW, LANES), 1) < HALF
    zero = jnp.zeros((WINDOW, LANES), F32)
    odd_head = lax.broadcasted_iota(jnp.int32, (1, two_w), 1) >= WINDOW

    def scores(blk, pair):
        r0 = blk * WINDOW
        first = jnp.where(t == 0, 1, 0) if blk == 0 else 0
        kvh = pair // PAIRS_PER_KV
        qp = q[r0:r0 + WINDOW, pair * LANES:(pair + 1) * LANES]
        rolled = pltpu.roll(qp, HALF, axis=1)
        if kvh == 0:
            q_even, q_odd = jnp.where(lo, qp, zero), jnp.where(lo, rolled, zero)
        else:
            q_even, q_odd = jnp.where(lo, zero, rolled), jnp.where(lo, zero, qp)
        qm = jnp.concatenate([q_even, q_odd], axis=0).astype(BF16)
        return _dot_nt(k_ref[r0:r0 + two_w, :], qm) + bias_ref[first, pair]

    def attend(blk, pair, s):
        r0 = blk * WINDOW
        kvh = pair // PAIRS_PER_KV
        sink = jnp.where(odd_head, sink_ref[2 * pair + 1], sink_ref[2 * pair]) * LOG2E
        m = jnp.maximum(jnp.max(s, axis=0, keepdims=True), sink)
        p = jnp.exp2(s - m)
        den = jnp.sum(p, axis=0, keepdims=True) + jnp.exp2(sink - m)
        vt_win = vt_ref[kvh * HEAD_DIM:(kvh + 1) * HEAD_DIM, r0:r0 + two_w]
        ot = _dot(vt_win, p.astype(BF16)) * (1.0 / den)
        o = jnp.concatenate([ot[:, :WINDOW], ot[:, WINDOW:]], axis=0).T
        attn_ref[r0:r0 + WINDOW, pair * LANES:(pair + 1) * LANES] = o

    steps = [(blk, pair) for blk in range(n_blocks) for pair in range(N_PAIRS)]
    s_next = scores(*steps[0])
    for i, step in enumerate(steps):
        s_cur = s_next
        if i + 1 < len(steps):
            s_next = scores(*steps[i + 1])
        attend(*step, s_cur)
        for _ in range(chunks_per_step):
            if conv_starts:
                conv_chunk(conv_starts.pop(0))
    while conv_starts:
        conv_chunk(conv_starts.pop(0))

    c2 = _conv_branch_tail(conv_ref[...] + dwb_ref[...], lng_ref[...], lnb_ref[...], wpw_ref)
    y_ref[...] = _mix_out(x, _gate(attn_ref[...], g_a), _gate(c2, g_c), wout_ref, fg_ref[...],
                          d_attn, final_norm)

    k_ref[:WINDOW, :] = k_ref[tile:, :]
    vt_ref[:, :WINDOW] = vt_ref[:, tile:]
    up_ref[:CONV_PAD, :] = up_ref[tile:, :]


def _prompt_layer(x, sinks, ng, win, dww, dwb, lng, lnb, wpw, wout, fg, final_norm):
    B, T, D = x.shape
    tile = PROMPT_TILE
    assert T % tile == 0 and tile % WINDOW == 0 and tile % CONV_ROWS == 0
    d_conv = dww.shape[1]
    d_kv = N_KV_HEADS * HEAD_DIM
    d_attn = N_HEADS * HEAD_DIM
    n_t = T // tile
    const = lambda shape: pl.BlockSpec(shape, lambda b, t: (0,) * len(shape))
    return pl.pallas_call(
        functools.partial(_prompt_kernel, tile=tile, final_norm=final_norm),
        grid=(B, n_t),
        in_specs=[
            pl.BlockSpec((None, tile, D), lambda b, t: (b, t, 0)),
            pl.BlockSpec(memory_space=pltpu.SMEM),
            const(ng.shape), const(win.shape), const(dww.shape), const(dwb.shape),
            const(lng.shape), const(lnb.shape), const(wpw.shape), const(wout.shape), const(fg.shape),
        ],
        out_specs=[
            pl.BlockSpec((None, tile, D), lambda b, t: (b, t, 0)),
            pl.BlockSpec((None, WINDOW, d_kv), lambda b, t: (b, 0, 0)),
            pl.BlockSpec((None, WINDOW, d_kv), lambda b, t: (b, 0, 0)),
            pl.BlockSpec((None, CONV_TAIL, d_conv), lambda b, t: (b, 0, 0)),
        ],
        scratch_shapes=[
            pltpu.VMEM((WINDOW + tile, d_kv), BF16),
            pltpu.VMEM((d_kv, WINDOW + tile), BF16),
            pltpu.VMEM((CONV_PAD + tile, d_conv), F32),
            pltpu.VMEM((SUBLANES - 1, CONV_PAD - SUBLANES + tile, d_conv), F32),
            pltpu.VMEM((CONV_W, SUBLANES, d_conv), F32),
            pltpu.VMEM((2, N_PAIRS, 2 * WINDOW, 2 * WINDOW), F32),
            pltpu.VMEM((tile, d_attn), F32),
            pltpu.VMEM((tile, d_conv), F32),
        ],
        out_shape=[
            jax.ShapeDtypeStruct((B, T, D), F32),
            jax.ShapeDtypeStruct((B, WINDOW, d_kv), F32),
            jax.ShapeDtypeStruct((B, WINDOW, d_kv), F32),
            jax.ShapeDtypeStruct((B, CONV_TAIL, d_conv), F32),
        ],
        compiler_params=pltpu.CompilerParams(
            dimension_semantics=("arbitrary", "arbitrary"), vmem_limit_bytes=VMEM_LIMIT_BYTES),
        name="prompt_layer",
    )(x, sinks, ng, win, dww, dwb, lng, lnb, wpw, wout, fg)


def _sample_kernel(x_ref, ck_ref, cv_ref, cs_ref, sink_ref, perm_ref, perm_t_ref, ng_ref, win_ref,
                   dww_ref, dwb_ref, lng_ref, lnb_ref, wpw_ref, wout_ref, fg_ref,
                   y_ref, ko_ref, vo_ref, co_ref,
                   q_ref, kn_ref, vn_ref, upt_ref, wb_ref, bias_ref, sinkcol_ref, attn_ref, conv_ref,
                   *, group, dec, final_norm):
    d_attn = N_HEADS * HEAD_DIM
    d_kv = N_KV_HEADS * HEAD_DIM
    d_conv = conv_ref.shape[1]
    n_keys = WINDOW + dec
    rows_q = N_HEADS * dec

    @pl.when(pl.program_id(0) == 0)
    def _init():
        ri = lax.broadcasted_iota(jnp.int32, (rows_q, n_keys), 0)
        si = lax.broadcasted_iota(jnp.int32, (rows_q, n_keys), 1)
        ti = ri
        slope = jnp.zeros((rows_q, n_keys), F32)
        for h in range(N_HEADS):
            in_head = (ri >= h * dec) & (ri < (h + 1) * dec)
            ti = jnp.where(in_head, ri - h * dec, ti)
            slope = jnp.where(in_head, _alibi_slope(h), slope)
        dist = WINDOW + ti - si
        valid = (dist >= 0) & (dist < WINDOW)
        distf = dist.astype(F32)
        bias_ref[...] = jnp.where(valid, -slope * distf, NEG_INF)
        rc = lax.broadcasted_iota(jnp.int32, (rows_q, 1), 0)
        sink_rows = jnp.zeros((rows_q, 1), F32)
        for h in range(N_HEADS):
            sink_rows = jnp.where((rc >= h * dec) & (rc < (h + 1) * dec), sink_ref[h], sink_rows)
        sinkcol_ref[...] = sink_rows
        for j in range(CONV_W):
            wb_ref[j] = jnp.broadcast_to(dww_ref[j:j + 1, :], (SUBLANES, d_conv))

    x = x_ref[...]
    h_in = _rms_norm(x, ng_ref[...]).astype(BF16)
    q_ref[...] =_dot(h_in, win_ref[:, :d_attn]) * (HEAD_DIM ** -0.5)
    kv = _dot(h_in, win_ref[:, d_attn:d_attn + 2 * d_kv])
    kn_ref[...] = kv[:, :d_kv]
    vn_ref[...] = kv[:, d_kv:]
    g_a = _dot(h_in, win_ref[:, d_attn + 2 * d_kv:d_attn + 2 * d_kv + d_attn])

    h_tm = _dot(perm_ref[...], h_in).astype(BF16)
    u_off = d_attn + 2 * d_kv + d_attn
    u_ab = _dot(h_tm, win_ref[:, u_off:u_off + 2 * d_conv])
    u_tm = u_ab[:, :d_conv] * jax.nn.sigmoid(u_ab[:, d_conv:])
    g_c_tm = _dot(h_tm, win_ref[:, u_off + 2 * d_conv:])

    lo = lax.broadcasted_iota(jnp.int32, (dec, LANES), 1) < HALF
    zero = jnp.zeros((dec, LANES), F32)
    sink = sinkcol_ref[...]
    bias = bias_ref[...]

    def scores(b):
        r = pl.multiple_of(b * dec, dec)
        k_all = jnp.concatenate([ck_ref[b], kn_ref[pl.ds(r, dec), :]], axis=0)
        ko_ref[b] = k_all[dec:, :]
        q_b = q_ref[pl.ds(r, dec), :]
        rows = []
        for h in range(N_HEADS):
            tile_q = q_b[:, (h // 2) * LANES:(h // 2 + 1) * LANES]
            in_lo = h % 2 == 0
            want_lo = h // GQA_GROUP == 0
            src = tile_q if in_lo == want_lo else pltpu.roll(tile_q, HALF, axis=1)
            rows.append(jnp.where(lo, src, zero) if want_lo else jnp.where(lo, zero, src))
        q_rows = jnp.concatenate(rows, axis=0).astype(BF16)
        return _dot_nt(q_rows, k_all.astype(BF16)) + bias

    def attend(b, s):
        r = pl.multiple_of(b * dec, dec)
        v_all = jnp.concatenate([cv_ref[b], vn_ref[pl.ds(r, dec), :]], axis=0)
        vo_ref[b] = v_all[dec:, :]
        p, rden = _softmax_sink(s, sink)
        o = _dot(p.astype(BF16), v_all.astype(BF16)) * rden
        tiles = []
        for pair in range(N_PAIRS):
            a = o[(2 * pair) * dec:(2 * pair + 1) * dec, :]
            c = o[(2 * pair + 1) * dec:(2 * pair + 2) * dec, :]
            if pair // PAIRS_PER_KV == 0:
                tiles.append(jnp.where(lo, a, pltpu.roll(c, HALF, axis=1)))
            else:
                tiles.append(jnp.where(lo, pltpu.roll(a, HALF, axis=1), c))
        attn_ref[pl.ds(r, dec), :] = jnp.concatenate(tiles, axis=1)

    upt_ref[:CONV_TAIL] = cs_ref[...]
    for tt in range(dec):
        upt_ref[CONV_TAIL + tt] = u_tm[tt * group:(tt + 1) * group, :]
    co_ref[...] = upt_ref[dec:]

    def conv_step(tt):
        for g0 in range(0, group, SUBLANES):
            acc = jnp.zeros((SUBLANES, d_conv), F32)
            for j in range(CONV_W):
                acc = acc + upt_ref[tt + j, g0:g0 + SUBLANES, :] * wb_ref[j]
            start = tt * group + g0
            if not isinstance(start, int):
                start = pl.multiple_of(start, SUBLANES)
            conv_ref[pl.ds(start, SUBLANES), :] = acc

    trips = group // SAMPLE_UNROLL
    conv_per_trip = dec // trips

    def attention_batch(i, carry):
        elems = [i * SAMPLE_UNROLL + e for e in range(SAMPLE_UNROLL)]
        ss = [scores(b) for b in elems]
        for b, s in zip(elems, ss):
            attend(b, s)
        for c in range(conv_per_trip):
            conv_step(i * conv_per_trip + c)
        return carry

    lax.fori_loop(0, trips, attention_batch, 0)
    for tt in range(trips * conv_per_trip, dec):
        conv_step(tt)

    c2_tm = _conv_branch_tail(conv_ref[...] + dwb_ref[...], lng_ref[...], lnb_ref[...], wpw_ref)
    m_c = _dot(perm_t_ref[...], _gate(c2_tm, g_c_tm)).astype(BF16)
    y_ref[...] = _mix_out(x, _gate(attn_ref[...], g_a), m_c, wout_ref, fg_ref[...], d_attn, final_norm)


def _sample_layer(x, ck, cv, cs, sinks, ng, win, dww, dwb, lng, lnb, wpw, wout, fg, final_norm):
    Bs, dec, D = x.shape
    group = SAMPLE_GROUP
    assert Bs % group == 0 and dec % 8 == 0 and dec <= CONV_TAIL
    d_conv = dww.shape[1]
    d_kv = N_KV_HEADS * HEAD_DIM
    d_attn = N_HEADS * HEAD_DIM
    rows = group * dec
    x2 = x.reshape(Bs * dec, D)
    perm = jnp.eye(rows, dtype=BF16).reshape(group, dec, rows).swapaxes(0, 1).reshape(rows, rows)
    perm_t = perm.T
    const = lambda shape: pl.BlockSpec(shape, lambda g: (0,) * len(shape))
    per_group = lambda shape: pl.BlockSpec((group,) + shape, lambda g: (g,) + (0,) * len(shape))
    conv_state = pl.BlockSpec((CONV_TAIL, group, d_conv), lambda g: (0, g, 0))
    y, ko, vo, co = pl.pallas_call(
        functools.partial(_sample_kernel, group=group, dec=dec, final_norm=final_norm),
        grid=(Bs // group,),
        in_specs=[
            pl.BlockSpec((rows, D), lambda g: (g, 0)),
            per_group((WINDOW, d_kv)), per_group((WINDOW, d_kv)), conv_state,
            pl.BlockSpec(memory_space=pltpu.SMEM), const(perm.shape), const(perm_t.shape),
            const(ng.shape), const(win.shape), const(dww.shape), const(dwb.shape),
            const(lng.shape), const(lnb.shape), const(wpw.shape), const(wout.shape), const(fg.shape),
        ],
        out_specs=[
            pl.BlockSpec((rows, D), lambda g: (g, 0)),
            per_group((WINDOW, d_kv)), per_group((WINDOW, d_kv)), conv_state,
        ],
        out_shape=[
            jax.ShapeDtypeStruct((Bs * dec, D), F32),
            jax.ShapeDtypeStruct((Bs, WINDOW, d_kv), F32),
            jax.ShapeDtypeStruct((Bs, WINDOW, d_kv), F32),
            jax.ShapeDtypeStruct((CONV_TAIL, Bs, d_conv), F32),
        ],
        scratch_shapes=[
            pltpu.VMEM((rows, d_attn), F32),
            pltpu.VMEM((rows, d_kv), F32),
            pltpu.VMEM((rows, d_kv), F32),
            pltpu.VMEM((CONV_TAIL + dec, group, d_conv), F32),
            pltpu.VMEM((CONV_W, SUBLANES, d_conv), F32),
            pltpu.VMEM((N_HEADS * dec, WINDOW + dec), F32),
            pltpu.VMEM((N_HEADS * dec, 1), F32),
            pltpu.VMEM((rows, d_attn), F32),
            pltpu.VMEM((rows, d_conv), F32),
        ],
        compiler_params=pltpu.CompilerParams(
            dimension_semantics=("arbitrary",), vmem_limit_bytes=VMEM_LIMIT_BYTES),
        name="sample_layer",
    )(x2, ck, cv, cs, sinks, perm, perm_t, ng, win, dww, dwb, lng, lnb, wpw, wout, fg)
    return y.reshape(Bs, dec, D), ko, vo, co


def kernel(x_prompt, x_sample, cache_k, cache_v, state_conv, norm_g, w_in, attn_sinks, dw_w, dw_b,
           conv_ln_g, conv_ln_b, w_pw2, w_out, final_norm_g):
    depth = w_in.shape[0]
    B = x_prompt.shape[0]
    Bs, dec = x_sample.shape[0], x_sample.shape[1]
    d_kv = N_KV_HEADS * HEAD_DIM
    fg = final_norm_g.reshape(1, -1)
    hp, hs = x_prompt, x_sample
    pk, pv, pc, sk, sv, sc = [], [], [], [], [], []
    for l in range(depth):
        final_norm = l == depth - 1
        row = lambda a: a[l].reshape(1, -1)
        win = w_in[l].astype(BF16)
        wpw = w_pw2[l].astype(BF16)
        wout = w_out[l].astype(BF16)
        shared = (row(norm_g), win, dw_w[l], row(dw_b), row(conv_ln_g), row(conv_ln_b), wpw, wout, fg)
        hp, k_p, v_p, c_p = _prompt_layer(hp, attn_sinks[l], *shared, final_norm)
        hs, k_s, v_s, c_s = _sample_layer(
            hs, cache_k[l].reshape(Bs, WINDOW, d_kv), cache_v[l].reshape(Bs, WINDOW, d_kv),
            state_conv[l].swapaxes(0, 1), attn_sinks[l], *shared, final_norm)
        pk.append(k_p.reshape(B, WINDOW, N_KV_HEADS, HEAD_DIM))
        pv.append(v_p.reshape(B, WINDOW, N_KV_HEADS, HEAD_DIM))
        pc.append(c_p)
        sk.append(k_s.reshape(Bs, WINDOW, N_KV_HEADS, HEAD_DIM))
        sv.append(v_s.reshape(Bs, WINDOW, N_KV_HEADS, HEAD_DIM))
        sc.append(c_s.swapaxes(0, 1))
    stack = lambda xs: xs[0][None] if len(xs) == 1 else jnp.stack(xs)
    return (hp, hs, stack(pk), stack(pv), stack(pc), stack(sk), stack(sv), stack(sc))
```

```python
import functools

import jax
import jax.numpy as jnp
from jax import lax
from jax.experimental import pallas as pl
from jax.experimental.pallas import tpu as pltpu

HEAD_DIM = 64
N_HEADS = 8
N_KV_HEADS = 2
GQA_GROUP = N_HEADS // N_KV_HEADS
N_PAIRS = N_HEADS // 2
PAIRS_PER_KV = N_PAIRS // N_KV_HEADS
WINDOW = 128
CONV_W = 31
CONV_TAIL = CONV_W - 1
EPS = 1e-5
LANES = 128
SUBLANES = 8
HALF = LANES // 2
NEG_INF = float("-inf")
LOG2E = 1.4426950408889634

PROMPT_TILE = 512
CONV_ROWS = 32
CONV_PAD = 32
SAMPLE_GROUP = 32
SAMPLE_UNROLL = 4
VMEM_LIMIT_BYTES = 56 * 1024 * 1024

F32 = jnp.float32
BF16 = jnp.bfloat16


def _alibi_slope(h):
    return 2.0 ** (-8.0 * (h + 1) / N_HEADS)


def _rms_norm(x, g):
    ms = jnp.mean(x * x, axis=-1, keepdims=True)
    return x * lax.rsqrt(ms + EPS) * g


def _silu(x):
    return x * jax.nn.sigmoid(x)


def _dot(a, b):
    return jnp.dot(a, b, preferred_element_type=F32)


def _dot_nt(a, b):
    return lax.dot_general(a, b, (((1,), (1,)), ((), ())), preferred_element_type=F32)


def _conv_branch_tail(c, lng, lnb, wpw_ref):
    mu = jnp.mean(c, axis=-1, keepdims=True)
    xc = c - mu
    var = jnp.mean(xc * xc, axis=-1, keepdims=True)
    y = xc * lax.rsqrt(var + EPS) * lng + lnb
    return _dot(_silu(y).astype(BF16), wpw_ref[...])


def _gate(val, g):
    return (val * _silu(g)).astype(BF16)


def _mix_out(x, m_a, m_c, wout_ref, fg, d_attn, final_norm):
    y = x + _dot(m_a, wout_ref[:d_attn, :]) + _dot(m_c, wout_ref[d_attn:, :])
    return _rms_norm(y, fg) if final_norm else y


def _softmax_sink(s, sink):
    m = jnp.maximum(jnp.max(s, axis=-1, keepdims=True), sink)
    p = jnp.exp(s - m)
    den = jnp.sum(p, axis=-1, keepdims=True) + jnp.exp(sink - m)
    return p, 1.0 / den


def _prompt_kernel(x_ref, sink_ref, ng_ref, win_ref, dww_ref, dwb_ref, lng_ref, lnb_ref,
                   wpw_ref, wout_ref, fg_ref,
                   y_ref, kst_ref, vst_ref, cst_ref,
                   k_ref, vt_ref, up_ref, shifted_ref, wb_ref, bias_ref, attn_ref, conv_ref,
                   *, tile, final_norm):
    t = pl.program_id(1)
    d_attn = N_HEADS * HEAD_DIM
    d_kv = N_KV_HEADS * HEAD_DIM
    d_conv = up_ref.shape[1]
    n_blocks = tile // WINDOW
    two_w = 2 * WINDOW
    u_off = d_attn + 2 * d_kv + d_attn
    n_shift = shifted_ref.shape[1]

    @pl.when(t == 0)
    def _init():
        k_ref[:WINDOW, :] = jnp.zeros((WINDOW, d_kv), BF16)
        vt_ref[:, :WINDOW] = jnp.zeros((d_kv, WINDOW), BF16)
        up_ref[:CONV_PAD, :] = jnp.zeros((CONV_PAD, d_conv), F32)
        for j in range(CONV_W):
            wb_ref[j] = jnp.broadcast_to(dww_ref[j:j + 1, :], (SUBLANES, d_conv))
        ji = lax.broadcasted_iota(jnp.int32, (two_w, two_w), 0)
        ci = lax.broadcasted_iota(jnp.int32, (two_w, two_w), 1)
        odd = ci >= WINDOW
        dist = jnp.where(odd, ci - WINDOW, ci) + WINDOW - ji
        valid = (dist >= 0) & (dist < WINDOW)
        valid_first = valid & (ji >= WINDOW)
        distf = dist.astype(F32) * LOG2E
        for pair in range(N_PAIRS):
            pen = -jnp.where(odd, _alibi_slope(2 * pair + 1), _alibi_slope(2 * pair)) * distf
            bias_ref[0, pair] = jnp.where(valid, pen, NEG_INF)
            bias_ref[1, pair] = jnp.where(valid_first, pen, NEG_INF)

    x = x_ref[...]
    h_in = _rms_norm(x, ng_ref[...]).astype(BF16)

    kv = _dot(h_in, win_ref[:, d_attn:d_attn + 2 * d_kv])
    k = kv[:, :d_kv]
    v = kv[:, d_kv:]
    k_ref[WINDOW:, :] = k.astype(BF16)
    vt_ref[:, WINDOW:] = v.T.astype(BF16)
    kst_ref[...] = k[tile - WINDOW:, :]
    vst_ref[...] = v[tile - WINDOW:, :]

    u_ab = _dot(h_in, win_ref[:, u_off:u_off + 2 * d_conv])
    up_ref[CONV_PAD:, :] = u_ab[:, :d_conv] * jax.nn.sigmoid(u_ab[:, d_conv:])
    cst_ref[...] = up_ref[CONV_PAD + tile - CONV_TAIL:, :]
    for s in range(1, SUBLANES):
        shifted_ref[s - 1] = up_ref[s:s + n_shift, :]

    groups = CONV_ROWS // SUBLANES

    def conv_chunk(r):
        acc = [jnp.zeros((SUBLANES, d_conv), F32) for _ in range(groups)]
        loaded = {}
        for j in range(CONV_W):
            off = CONV_PAD - CONV_TAIL + j
            a, s = off // SUBLANES, off % SUBLANES
            wb = wb_ref[j]
            for g in range(groups):
                if (s, a + g) not in loaded:
                    rows = slice(r + (a + g) * SUBLANES, r + (a + g + 1) * SUBLANES)
                    loaded[s, a + g] = up_ref[rows, :] if s == 0 else shifted_ref[s - 1, rows, :]
                acc[g] = acc[g] + loaded[s, a + g] * wb
        for g in range(groups):
            conv_ref[r + g * SUBLANES:r + (g + 1) * SUBLANES, :] = acc[g]

    conv_starts = list(range(0, tile, CONV_ROWS))
    chunks_per_step = -(-len(conv_starts) // (n_blocks * N_PAIRS))

    q = _dot(h_in, win_ref[:, :d_attn]) * (HEAD_DIM ** -0.5 * LOG2E)
    g_a = _dot(h_in, win_ref[:, d_attn + 2 * d_kv:u_off])
    g_c = _dot(h_in, win_ref[:, u_off + 2 * d_conv:])

    lo = lax.broadcasted_iota(jnp.int32, (WINDOW, LANES), 1) < HALF
    zero = jnp.zeros((WINDOW, LANES), F32)
    odd_head = lax.broadcasted_iota(jnp.int32, (1, two_w), 1) >= WINDOW

    def scores(blk, pair):
        r0 = blk * WINDOW
        first = jnp.where(t == 0, 1, 0) if blk == 0 else 0
        kvh = pair // PAIRS_PER_KV
        qp = q[r0:r0 + WINDOW, pair * LANES:(pair + 1) * LANES]
        rolled = pltpu.roll(qp, HALF, axis=1)
        if kvh == 0:
            q_even, q_odd = jnp.where(lo, qp, zero), jnp.where(lo, rolled, zero)
        else:
            q_even, q_odd = jnp.where(lo, zero, rolled), jnp.where(lo, zero, qp)
        qm = jnp.concatenate([q_even, q_odd], axis=0).astype(BF16)
        return _dot_nt(k_ref[r0:r0 + two_w, :], qm) + bias_ref[first, pair]

    def attend(blk, pair, s):
        r0 = blk * WINDOW
        kvh = pair // PAIRS_PER_KV
        sink = jnp.where(odd_head, sink_ref[2 * pair + 1], sink_ref[2 * pair]) * LOG2E
        m = jnp.maximum(jnp.max(s, axis=0, keepdims=True), sink)
        p = jnp.exp2(s - m)
        den = jnp.sum(p, axis=0, keepdims=True) + jnp.exp2(sink - m)
        vt_win = vt_ref[kvh * HEAD_DIM:(kvh + 1) * HEAD_DIM, r0:r0 + two_w]
        ot = _dot(vt_win, p.astype(BF16)) * (1.0 / den)
        o = jnp.concatenate([ot[:, :WINDOW], ot[:, WINDOW:]], axis=0).T
        attn_ref[r0:r0 + WINDOW, pair * LANES:(pair + 1) * LANES] = o

    steps = [(blk, pair) for blk in range(n_blocks) for pair in range(N_PAIRS)]
    s_next = scores(*steps[0])
    for i, step in enumerate(steps):
        s_cur = s_next
        if i + 1 < len(steps):
            s_next = scores(*steps[i + 1])
        attend(*step, s_cur)
        for _ in range(chunks_per_step):
            if conv_starts:
                conv_chunk(conv_starts.pop(0))
    while conv_starts:
        conv_chunk(conv_starts.pop(0))

    y_a = _dot(_gate(attn_ref[...], g_a), wout_ref[:d_attn, :])

    c2 = _conv_branch_tail(conv_ref[...] + dwb_ref[...], lng_ref[...], lnb_ref[...], wpw_ref)
    y = x + y_a + _dot(_gate(c2, g_c), wout_ref[d_attn:, :])
    y_ref[...] = _rms_norm(y, fg_ref[...]) if final_norm else y

    k_ref[:WINDOW, :] = k_ref[tile:, :]
    vt_ref[:, :WINDOW] = vt_ref[:, tile:]
    up_ref[:CONV_PAD, :] = up_ref[tile:, :]


def _prompt_layer(x, sinks, ng, win, dww, dwb, lng, lnb, wpw, wout, fg, final_norm):
    B, T, D = x.shape
    tile = PROMPT_TILE
    assert T % tile == 0 and tile % WINDOW == 0 and tile % CONV_ROWS == 0
    d_conv = dww.shape[1]
    d_kv = N_KV_HEADS * HEAD_DIM
    d_attn = N_HEADS * HEAD_DIM
    n_t = T // tile
    const = lambda shape: pl.BlockSpec(shape, lambda b, t: (0,) * len(shape))
    return pl.pallas_call(
        functools.partial(_prompt_kernel, tile=tile, final_norm=final_norm),
        grid=(B, n_t),
        in_specs=[
            pl.BlockSpec((None, tile, D), lambda b, t: (b, t, 0)),
            pl.BlockSpec(memory_space=pltpu.SMEM),
            const(ng.shape), const(win.shape), const(dww.shape), const(dwb.shape),
            const(lng.shape), const(lnb.shape), const(wpw.shape), const(wout.shape), const(fg.shape),
        ],
        out_specs=[
            pl.BlockSpec((None, tile, D), lambda b, t: (b, t, 0)),
            pl.BlockSpec((None, WINDOW, d_kv), lambda b, t: (b, 0, 0)),
            pl.BlockSpec((None, WINDOW, d_kv), lambda b, t: (b, 0, 0)),
            pl.BlockSpec((None, CONV_TAIL, d_conv), lambda b, t: (b, 0, 0)),
        ],
        scratch_shapes=[
            pltpu.VMEM((WINDOW + tile, d_kv), BF16),
            pltpu.VMEM((d_kv, WINDOW + tile), BF16),
            pltpu.VMEM((CONV_PAD + tile, d_conv), F32),
            pltpu.VMEM((SUBLANES - 1, CONV_PAD - SUBLANES + tile, d_conv), F32),
            pltpu.VMEM((CONV_W, SUBLANES, d_conv), F32),
            pltpu.VMEM((2, N_PAIRS, 2 * WINDOW, 2 * WINDOW), F32),
            pltpu.VMEM((tile, d_attn), F32),
            pltpu.VMEM((tile, d_conv), F32),
        ],
        out_shape=[
            jax.ShapeDtypeStruct((B, T, D), F32),
            jax.ShapeDtypeStruct((B, WINDOW, d_kv), F32),
            jax.ShapeDtypeStruct((B, WINDOW, d_kv), F32),
            jax.ShapeDtypeStruct((B, CONV_TAIL, d_conv), F32),
        ],
        compiler_params=pltpu.CompilerParams(
            dimension_semantics=("arbitrary", "arbitrary"), vmem_limit_bytes=VMEM_LIMIT_BYTES),
        name="prompt_layer",
    )(x, sinks, ng, win, dww, dwb, lng, lnb, wpw, wout, fg)


def _sample_kernel(x_ref, ck_ref, cv_ref, cs_ref, sink_ref, perm_ref, perm_t_ref, ng_ref, win_ref,
                   dww_ref, dwb_ref, lng_ref, lnb_ref, wpw_ref, wout_ref, fg_ref,
                   y_ref, ko_ref, vo_ref, co_ref,
                   q_ref, kn_ref, vn_ref, upt_ref, wb_ref, bias_ref, sinkcol_ref, attn_ref, conv_ref,
                   *, group, dec, final_norm):
    d_attn = N_HEADS * HEAD_DIM
    d_kv = N_KV_HEADS * HEAD_DIM
    d_conv = conv_ref.shape[1]
    n_keys = WINDOW + dec
    rows_q = N_HEADS * dec

    @pl.when(pl.program_id(0) == 0)
    def _init():
        ri = lax.broadcasted_iota(jnp.int32, (rows_q, n_keys), 0)
        si = lax.broadcasted_iota(jnp.int32, (rows_q, n_keys), 1)
        ti = ri
        slope = jnp.zeros((rows_q, n_keys), F32)
        for h in range(N_HEADS):
            in_head = (ri >= h * dec) & (ri < (h + 1) * dec)
            ti = jnp.where(in_head, ri - h * dec, ti)
            slope = jnp.where(in_head, _alibi_slope(h), slope)
        dist = WINDOW + ti - si
        valid = (dist >= 0) & (dist < WINDOW)
        distf = dist.astype(F32)
        bias_ref[...] = jnp.where(valid, -slope * distf, NEG_INF)
        rc = lax.broadcasted_iota(jnp.int32, (rows_q, 1), 0)
        sink_rows = jnp.zeros((rows_q, 1), F32)
        for h in range(N_HEADS):
            sink_rows = jnp.where((rc >= h * dec) & (rc < (h + 1) * dec), sink_ref[h], sink_rows)
        sinkcol_ref[...] = sink_rows
        for j in range(CONV_W):
            wb_ref[j] = jnp.broadcast_to(dww_ref[j:j + 1, :], (SUBLANES, d_conv))

    x = x_ref[...]
    h_in = _rms_norm(x, ng_ref[...]).astype(BF16)
    q_ref[...] =_dot(h_in, win_ref[:, :d_attn]) * (HEAD_DIM ** -0.5)
    kv = _dot(h_in, win_ref[:, d_attn:d_attn + 2 * d_kv])
    kn_ref[...] = kv[:, :d_kv]
    vn_ref[...] = kv[:, d_kv:]
    g_a = _dot(h_in, win_ref[:, d_attn + 2 * d_kv:d_attn + 2 * d_kv + d_attn])

    h_tm = _dot(perm_ref[...], h_in).astype(BF16)
    u_off = d_attn + 2 * d_kv + d_attn
    u_ab = _dot(h_tm, win_ref[:, u_off:u_off + 2 * d_conv])
    u_tm = u_ab[:, :d_conv] * jax.nn.sigmoid(u_ab[:, d_conv:])
    g_c_tm = _dot(h_tm, win_ref[:, u_off + 2 * d_conv:])

    lo = lax.broadcasted_iota(jnp.int32, (dec, LANES), 1) < HALF
    zero = jnp.zeros((dec, LANES), F32)
    sink = sinkcol_ref[...]
    bias = bias_ref[...]

    def scores(b):
        r = pl.multiple_of(b * dec, dec)
        k_all = jnp.concatenate([ck_ref[b], kn_ref[pl.ds(r, dec), :]], axis=0)
        ko_ref[b] = k_all[dec:, :]
        q_b = q_ref[pl.ds(r, dec), :]
        rows = []
        for h in range(N_HEADS):
            tile_q = q_b[:, (h // 2) * LANES:(h // 2 + 1) * LANES]
            in_lo = h % 2 == 0
            want_lo = h // GQA_GROUP == 0
            src = tile_q if in_lo == want_lo else pltpu.roll(tile_q, HALF, axis=1)
            rows.append(jnp.where(lo, src, zero) if want_lo else jnp.where(lo, zero, src))
        q_rows = jnp.concatenate(rows, axis=0).astype(BF16)
        return _dot_nt(q_rows, k_all.astype(BF16)) + bias

    def attend(b, s):
        r = pl.multiple_of(b * dec, dec)
        v_all = jnp.concatenate([cv_ref[b], vn_ref[pl.ds(r, dec), :]], axis=0)
        vo_ref[b] = v_all[dec:, :]
        p, rden = _softmax_sink(s, sink)
        o = _dot(p.astype(BF16), v_all.astype(BF16)) * rden
        tiles = []
        for pair in range(N_PAIRS):
            a = o[(2 * pair) * dec:(2 * pair + 1) * dec, :]
            c = o[(2 * pair + 1) * dec:(2 * pair + 2) * dec, :]
            if pair // PAIRS_PER_KV == 0:
                tiles.append(jnp.where(lo, a, pltpu.roll(c, HALF, axis=1)))
            else:
                tiles.append(jnp.where(lo, pltpu.roll(a, HALF, axis=1), c))
        attn_ref[pl.ds(r, dec), :] = jnp.concatenate(tiles, axis=1)

    upt_ref[:CONV_TAIL] = cs_ref[...]
    for tt in range(dec):
        upt_ref[CONV_TAIL + tt] = u_tm[tt * group:(tt + 1) * group, :]
    co_ref[...] = upt_ref[dec:]

    def conv_step(tt):
        for g0 in range(0, group, SUBLANES):
            acc = jnp.zeros((SUBLANES, d_conv), F32)
            for j in range(CONV_W):
                acc = acc + upt_ref[tt + j, g0:g0 + SUBLANES, :] * wb_ref[j]
            start = tt * group + g0
            if not isinstance(start, int):
                start = pl.multiple_of(start, SUBLANES)
            conv_ref[pl.ds(start, SUBLANES), :] = acc

    trips = group // SAMPLE_UNROLL
    conv_per_trip = dec // trips

    def attention_batch(i, carry):
        elems = [i * SAMPLE_UNROLL + e for e in range(SAMPLE_UNROLL)]
        ss = [scores(b) for b in elems]
        for b, s in zip(elems, ss):
            attend(b, s)
        for c in range(conv_per_trip):
            conv_step(i * conv_per_trip + c)
        return carry

    lax.fori_loop(0, trips, attention_batch, 0)
    for tt in range(trips * conv_per_trip, dec):
        conv_step(tt)

    c2_tm = _conv_branch_tail(conv_ref[...] + dwb_ref[...], lng_ref[...], lnb_ref[...], wpw_ref)
    m_c = _dot(perm_t_ref[...], _gate(c2_tm, g_c_tm)).astype(BF16)
    y_ref[...] = _mix_out(x, _gate(attn_ref[...], g_a), m_c, wout_ref, fg_ref[...], d_attn, final_norm)


def _sample_layer(x, ck, cv, cs, sinks, ng, win, dww, dwb, lng, lnb, wpw, wout, fg, final_norm):
    Bs, dec, D = x.shape
    group = SAMPLE_GROUP
    assert Bs % group == 0 and dec % 8 == 0 and dec <= CONV_TAIL
    d_conv = dww.shape[1]
    d_kv = N_KV_HEADS * HEAD_DIM
    d_attn = N_HEADS * HEAD_DIM
    rows = group * dec
    x2 = x.reshape(Bs * dec, D)
    perm = jnp.eye(rows, dtype=BF16).reshape(group, dec, rows).swapaxes(0, 1).reshape(rows, rows)
    perm_t = perm.T
    const = lambda shape: pl.BlockSpec(shape, lambda g: (0,) * len(shape))
    per_group = lambda shape: pl.BlockSpec((group,) + shape, lambda g: (g,) + (0,) * len(shape))
    conv_state = pl.BlockSpec((CONV_TAIL, group, d_conv), lambda g: (0, g, 0))
    y, ko, vo, co = pl.pallas_call(
        functools.partial(_sample_kernel, group=group, dec=dec, final_norm=final_norm),
        grid=(Bs // group,),
        in_specs=[
            pl.BlockSpec((rows, D), lambda g: (g, 0)),
            per_group((WINDOW, d_kv)), per_group((WINDOW, d_kv)), conv_state,
            pl.BlockSpec(memory_space=pltpu.SMEM), const(perm.shape), const(perm_t.shape),
            const(ng.shape), const(win.shape), const(dww.shape), const(dwb.shape),
            const(lng.shape), const(lnb.shape), const(wpw.shape), const(wout.shape), const(fg.shape),
        ],
        out_specs=[
            pl.BlockSpec((rows, D), lambda g: (g, 0)),
            per_group((WINDOW, d_kv)), per_group((WINDOW, d_kv)), conv_state,
        ],
        out_shape=[
            jax.ShapeDtypeStruct((Bs * dec, D), F32),
            jax.ShapeDtypeStruct((Bs, WINDOW, d_kv), F32),
            jax.ShapeDtypeStruct((Bs, WINDOW, d_kv), F32),
            jax.ShapeDtypeStruct((CONV_TAIL, Bs, d_conv), F32),
        ],
        scratch_shapes=[
            pltpu.VMEM((rows, d_attn), F32),
            pltpu.VMEM((rows, d_kv), F32),
            pltpu.VMEM((rows, d_kv), F32),
            pltpu.VMEM((CONV_TAIL + dec, group, d_conv), F32),
            pltpu.VMEM((CONV_W, SUBLANES, d_conv), F32),
            pltpu.VMEM((N_HEADS * dec, WINDOW + dec), F32),
            pltpu.VMEM((N_HEADS * dec, 1), F32),
            pltpu.VMEM((rows, d_attn), F32),
            pltpu.VMEM((rows, d_conv), F32),
        ],
        compiler_params=pltpu.CompilerParams(
            dimension_semantics=("arbitrary",), vmem_limit_bytes=VMEM_LIMIT_BYTES),
        name="sample_layer",
    )(x2, ck, cv, cs, sinks, perm, perm_t, ng, win, dww, dwb, lng, lnb, wpw, wout, fg)
    return y.reshape(Bs, dec, D), ko, vo, co


def kernel(x_prompt, x_sample, cache_k, cache_v, state_conv, norm_g, w_in, attn_sinks, dw_w, dw_b,
           conv_ln_g, conv_ln_b, w_pw2, w_out, final_norm_g):
    depth = w_in.shape[0]
    B = x_prompt.shape[0]
    Bs, dec = x_sample.shape[0], x_sample.shape[1]
    d_kv = N_KV_HEADS * HEAD_DIM
    fg = final_norm_g.reshape(1, -1)
    hp, hs = x_prompt, x_sample
    pk, pv, pc, sk, sv, sc = [], [], [], [], [], []
    for l in range(depth):
        final_norm = l == depth - 1
        row = lambda a: a[l].reshape(1, -1)
        win = w_in[l].astype(BF16)
        wpw = w_pw2[l].astype(BF16)
        wout = w_out[l].astype(BF16)
        shared = (row(norm_g), win, dw_w[l], row(dw_b), row(conv_ln_g), row(conv_ln_b), wpw, wout, fg)
        hp, k_p, v_p, c_p = _prompt_layer(hp, attn_sinks[l], *shared, final_norm)
        hs, k_s, v_s, c_s = _sample_layer(
            hs, cache_k[l].reshape(Bs, WINDOW, d_kv), cache_v[l].reshape(Bs, WINDOW, d_kv),
            state_conv[l].swapaxes(0, 1), attn_sinks[l], *shared, final_norm)
        pk.append(k_p.reshape(B, WINDOW, N_KV_HEADS, HEAD_DIM))
        pv.append(v_p.reshape(B, WINDOW, N_KV_HEADS, HEAD_DIM))
        pc.append(c_p)
        sk.append(k_s.reshape(Bs, WINDOW, N_KV_HEADS, HEAD_DIM))
        sv.append(v_s.reshape(Bs, WINDOW, N_KV_HEADS, HEAD_DIM))
        sc.append(c_s.swapaxes(0, 1))
    stack = lambda xs: xs[0][None] if len(xs) == 1 else jnp.stack(xs)
    return (hp, hs, stack(pk), stack(pv), stack(pc), stack(sk), stack(sv), stack(sc))
```

```python
import functools

import jax
import jax.numpy as jnp
from jax import lax
from jax.experimental import pallas as pl
from jax.experimental.pallas import tpu as pltpu

HEAD_DIM = 64
N_HEADS = 8
N_KV_HEADS = 2
GQA_GROUP = N_HEADS // N_KV_HEADS
N_PAIRS = N_HEADS // 2
PAIRS_PER_KV = N_PAIRS // N_KV_HEADS
WINDOW = 128
CONV_W = 31
CONV_TAIL = CONV_W - 1
EPS = 1e-5
LANES = 128
SUBLANES = 8
HALF = LANES // 2
NEG_INF = float("-inf")
LOG2E = 1.4426950408889634

PROMPT_TILE = 512
CONV_ROWS = 64
CONV_CH_BLOCKS = 2
CONV_PAD = 32
SAMPLE_GROUP = 32
SAMPLE_UNROLL = 4
VMEM_LIMIT_BYTES = 56 * 1024 * 1024

F32 = jnp.float32
BF16 = jnp.bfloat16


def _alibi_slope(h):
    return 2.0 ** (-8.0 * (h + 1) / N_HEADS)


def _rms_norm(x, g):
    ms = jnp.mean(x * x, axis=-1, keepdims=True)
    return x * lax.rsqrt(ms + EPS) * g


def _silu(x):
    return x * jax.nn.sigmoid(x)


def _dot(a, b):
    return jnp.dot(a, b, preferred_element_type=F32)


def _dot_nt(a, b):
    return lax.dot_general(a, b, (((1,), (1,)), ((), ())), preferred_element_type=F32)


def _conv_branch_tail(c, lng, lnb, wpw_ref):
    mu = jnp.mean(c, axis=-1, keepdims=True)
    xc = c - mu
    var = jnp.mean(xc * xc, axis=-1, keepdims=True)
    y = xc * lax.rsqrt(var + EPS) * lng + lnb
    return _dot(_silu(y).astype(BF16), wpw_ref[...])


def _gate(val, g):
    return (val * _silu(g)).astype(BF16)


def _mix_out(x, m_a, m_c, wout_ref, fg, d_attn, final_norm):
    y = x + _dot(m_a, wout_ref[:d_attn, :]) + _dot(m_c, wout_ref[d_attn:, :])
    return _rms_norm(y, fg) if final_norm else y


def _softmax_sink(s, sink):
    m = jnp.maximum(jnp.max(s, axis=-1, keepdims=True), sink)
    p = jnp.exp(s - m)
    den = jnp.sum(p, axis=-1, keepdims=True) + jnp.exp(sink - m)
    return p, 1.0 / den


def _prompt_kernel(x_ref, sink_ref, ng_ref, win_ref, dww_ref, dwb_ref, lng_ref, lnb_ref,
                   wpw_ref, wout_ref, fg_ref,
                   y_ref, kst_ref, vst_ref, cst_ref,
                   k_ref, vt_ref, up_ref, shifted_ref, wb_ref, bias_ref, attn_ref, conv_ref,
                   *, tile, final_norm):
    t = pl.program_id(1)
    d_attn = N_HEADS * HEAD_DIM
    d_kv = N_KV_HEADS * HEAD_DIM
    d_conv = up_ref.shape[1]
    n_blocks = tile // WINDOW
    two_w = 2 * WINDOW
    u_off = d_attn + 2 * d_kv + d_attn
    n_shift = shifted_ref.shape[1]

    @pl.when(t == 0)
    def _init():
        k_ref[:WINDOW, :] = jnp.zeros((WINDOW, d_kv), BF16)
        vt_ref[:, :WINDOW] = jnp.zeros((d_kv, WINDOW), BF16)
        up_ref[:CONV_PAD, :] = jnp.zeros((CONV_PAD, d_conv), F32)
        for j in range(CONV_W):
            wb_ref[j] = jnp.broadcast_to(dww_ref[j:j + 1, :], (SUBLANES, d_conv))
        ji = lax.broadcasted_iota(jnp.int32, (two_w, two_w), 0)
        ci = lax.broadcasted_iota(jnp.int32, (two_w, two_w), 1)
        odd = ci >= WINDOW
        dist = jnp.where(odd, ci - WINDOW, ci) + WINDOW - ji
        valid = (dist >= 0) & (dist < WINDOW)
        valid_first = valid & (ji >= WINDOW)
        distf = dist.astype(F32) * LOG2E
        for pair in range(N_PAIRS):
            pen = -jnp.where(odd, _alibi_slope(2 * pair + 1), _alibi_slope(2 * pair)) * distf
            bias_ref[0, pair] = jnp.where(valid, pen, NEG_INF)
            bias_ref[1, pair] = jnp.where(valid_first, pen, NEG_INF)

    x = x_ref[...]
    h_in = _rms_norm(x, ng_ref[...]).astype(BF16)

    kv = _dot(h_in, win_ref[:, d_attn:d_attn + 2 * d_kv])
    k = kv[:, :d_kv]
    v = kv[:, d_kv:]
    k_ref[WINDOW:, :] = k.astype(BF16)
    v_t = v.T
    vt_ref[:, WINDOW:] = v_t.astype(BF16)
    kst_ref[...] = k[tile - WINDOW:, :].T
    vst_ref[...] = v_t[:, tile - WINDOW:]

    cw = d_conv // CONV_CH_BLOCKS
    for cb in range(CONV_CH_BLOCKS):
        ch = slice(cb * cw, (cb + 1) * cw)
        u_a = _dot(h_in, win_ref[:, u_off + cb * cw:u_off + (cb + 1) * cw])
        u_b = _dot(h_in, win_ref[:, u_off + d_conv + cb * cw:u_off + d_conv + (cb + 1) * cw])
        up_ref[CONV_PAD:, ch] = u_a * jax.nn.sigmoid(u_b)
        for s in range(1, SUBLANES):
            shifted_ref[s - 1, :, ch] = up_ref[s:s + n_shift, ch]
    cst_ref[...] = up_ref[CONV_PAD + tile - CONV_TAIL:, :]

    groups = CONV_ROWS // SUBLANES

    def conv_chunk(r, cb):
        ch = slice(cb * cw, (cb + 1) * cw)
        acc = [jnp.zeros((SUBLANES, cw), F32) for _ in range(groups)]
        loaded = {}
        for j in range(CONV_W):
            off = CONV_PAD - CONV_TAIL + j
            a, s = off // SUBLANES, off % SUBLANES
            wb = wb_ref[j, :, ch]
            for g in range(groups):
                if (s, a + g) not in loaded:
                    rows = slice(r + (a + g) * SUBLANES, r + (a + g + 1) * SUBLANES)
                    loaded[s, a + g] = up_ref[rows, ch] if s == 0 else shifted_ref[s - 1, rows, ch]
                acc[g] = acc[g] + loaded[s, a + g] * wb
        for g in range(groups):
            conv_ref[r + g * SUBLANES:r + (g + 1) * SUBLANES, ch] = acc[g]

    q = _dot(h_in, win_ref[:, :d_attn]) * (HEAD_DIM ** -0.5 * LOG2E)
    g_a = _dot(h_in, win_ref[:, d_attn + 2 * d_kv:u_off])
    g_c = _dot(h_in, win_ref[:, u_off + 2 * d_conv:])

    lo = lax.broadcasted_iota(jnp.int32, (WINDOW, LANES), 1) < HALF
    zero = jnp.zeros((WINDOW, LANES), F32)
    odd_head = lax.broadcasted_iota(jnp.int32, (1, two_w), 1) >= WINDOW

    def scores(blk, pair):
        r0 = blk * WINDOW
        first = jnp.where(t == 0, 1, 0) if blk == 0 else 0
        kvh = pair // PAIRS_PER_KV
        qp = q[r0:r0 + WINDOW, pair * LANES:(pair + 1) * LANES]
        rolled = pltpu.roll(qp, HALF, axis=1)
        if kvh == 0:
            q_even, q_odd = jnp.where(lo, qp, zero), jnp.where(lo, rolled, zero)
        else:
            q_even, q_odd = jnp.where(lo, zero, rolled), jnp.where(lo, zero, qp)
        qm = jnp.concatenate([q_even, q_odd], axis=0).astype(BF16)
        return _dot_nt(k_ref[r0:r0 + two_w, :], qm) + bias_ref[first, pair]

    def attend(blk, pair, s):
        r0 = blk * WINDOW
        kvh = pair // PAIRS_PER_KV
        sink = jnp.where(odd_head, sink_ref[2 * pair + 1], sink_ref[2 * pair]) * LOG2E
        m = jnp.maximum(jnp.max(s, axis=0, keepdims=True), sink)
        p = jnp.exp2(s - m)
        den = jnp.sum(p, axis=0, keepdims=True) + jnp.exp2(sink - m)
        vt_win = vt_ref[kvh * HEAD_DIM:(kvh + 1) * HEAD_DIM, r0:r0 + two_w]
        ot = _dot(vt_win, p.astype(BF16)) * (1.0 / den)
        o = jnp.concatenate([ot[:, :WINDOW], ot[:, WINDOW:]], axis=0).T
        attn_ref[r0:r0 + WINDOW, pair * LANES:(pair + 1) * LANES] = o

    steps = [(blk, pair) for blk in range(n_blocks) for pair in range(N_PAIRS)]
    s_next = scores(*steps[0])
    for i, step in enumerate(steps):
        s_cur = s_next
        if i + 1 < len(steps):
            s_next = scores(*steps[i + 1])
        attend(*step, s_cur)

    y_a = _dot(_gate(attn_ref[...], g_a), wout_ref[:d_attn, :])
    for cb in range(CONV_CH_BLOCKS):
        for r in range(0, tile, CONV_ROWS):
            conv_chunk(r, cb)

    c2 = _conv_branch_tail(conv_ref[...] + dwb_ref[...], lng_ref[...], lnb_ref[...], wpw_ref)
    y = x + y_a + _dot(_gate(c2, g_c), wout_ref[d_attn:, :])
    y_ref[...] = _rms_norm(y, fg_ref[...]) if final_norm else y

    k_ref[:WINDOW, :] = k_ref[tile:, :]
    vt_ref[:, :WINDOW] = vt_ref[:, tile:]
    up_ref[:CONV_PAD, :] = up_ref[tile:, :]


def _prompt_layer(x, sinks, ng, win, dww, dwb, lng, lnb, wpw, wout, fg, final_norm):
    B, T, D = x.shape
    tile = PROMPT_TILE
    assert T % tile == 0 and tile % WINDOW == 0 and tile % CONV_ROWS == 0
    d_conv = dww.shape[1]
    d_kv = N_KV_HEADS * HEAD_DIM
    d_attn = N_HEADS * HEAD_DIM
    n_t = T // tile
    const = lambda shape: pl.BlockSpec(shape, lambda b, t: (0,) * len(shape))
    return pl.pallas_call(
        functools.partial(_prompt_kernel, tile=tile, final_norm=final_norm),
        grid=(B, n_t),
        in_specs=[
            pl.BlockSpec((None, tile, D), lambda b, t: (b, t, 0)),
            pl.BlockSpec(memory_space=pltpu.SMEM),
            const(ng.shape), const(win.shape), const(dww.shape), const(dwb.shape),
            const(lng.shape), const(lnb.shape), const(wpw.shape), const(wout.shape), const(fg.shape),
        ],
        out_specs=[
            pl.BlockSpec((None, tile, D), lambda b, t: (b, t, 0)),
            pl.BlockSpec((None, d_kv, WINDOW), lambda b, t: (b, 0, 0)),
            pl.BlockSpec((None, d_kv, WINDOW), lambda b, t: (b, 0, 0)),
            pl.BlockSpec((None, CONV_TAIL, d_conv), lambda b, t: (b, 0, 0)),
        ],
        scratch_shapes=[
            pltpu.VMEM((WINDOW + tile, d_kv), BF16),
            pltpu.VMEM((d_kv, WINDOW + tile), BF16),
            pltpu.VMEM((CONV_PAD + tile, d_conv), F32),
            pltpu.VMEM((SUBLANES - 1, CONV_PAD - SUBLANES + tile, d_conv), F32),
            pltpu.VMEM((CONV_W, SUBLANES, d_conv), F32),
            pltpu.VMEM((2, N_PAIRS, 2 * WINDOW, 2 * WINDOW), F32),
            pltpu.VMEM((tile, d_attn), F32),
            pltpu.VMEM((tile, d_conv), F32),
        ],
        out_shape=[
            jax.ShapeDtypeStruct((B, T, D), F32),
            jax.ShapeDtypeStruct((B, d_kv, WINDOW), F32),
            jax.ShapeDtypeStruct((B, d_kv, WINDOW), F32),
            jax.ShapeDtypeStruct((B, CONV_TAIL, d_conv), F32),
        ],
        compiler_params=pltpu.CompilerParams(
            dimension_semantics=("arbitrary", "arbitrary"), vmem_limit_bytes=VMEM_LIMIT_BYTES),
        name="prompt_layer",
    )(x, sinks, ng, win, dww, dwb, lng, lnb, wpw, wout, fg)


def _sample_kernel(x_ref, ck_ref, cv_ref, cs_ref, sink_ref, perm_ref, perm_t_ref, ng_ref, win_ref,
                   dww_ref, dwb_ref, lng_ref, lnb_ref, wpw_ref, wout_ref, fg_ref,
                   y_ref, ko_ref, vo_ref, co_ref,
                   q_ref, kn_ref, vn_ref, upt_ref, wb_ref, bias_ref, sinkcol_ref, attn_ref, conv_ref,
                   *, group, dec, final_norm):
    d_attn = N_HEADS * HEAD_DIM
    d_kv = N_KV_HEADS * HEAD_DIM
    d_conv = conv_ref.shape[1]
    n_keys = WINDOW + dec
    rows_q = N_HEADS * dec

    @pl.when(pl.program_id(0) == 0)
    def _init():
        ri = lax.broadcasted_iota(jnp.int32, (rows_q, n_keys), 0)
        si = lax.broadcasted_iota(jnp.int32, (rows_q, n_keys), 1)
        ti = ri
        slope = jnp.zeros((rows_q, n_keys), F32)
        for h in range(N_HEADS):
            in_head = (ri >= h * dec) & (ri < (h + 1) * dec)
            ti = jnp.where(in_head, ri - h * dec, ti)
            slope = jnp.where(in_head, _alibi_slope(h), slope)
        dist = WINDOW + ti - si
        valid = (dist >= 0) & (dist < WINDOW)
        distf = dist.astype(F32)
        bias_ref[...] = jnp.where(valid, -slope * distf, NEG_INF)
        rc = lax.broadcasted_iota(jnp.int32, (rows_q, 1), 0)
        sink_rows = jnp.zeros((rows_q, 1), F32)
        for h in range(N_HEADS):
            sink_rows = jnp.where((rc >= h * dec) & (rc < (h + 1) * dec), sink_ref[h], sink_rows)
        sinkcol_ref[...] = sink_rows
        for j in range(CONV_W):
            wb_ref[j] = jnp.broadcast_to(dww_ref[j:j + 1, :], (SUBLANES, d_conv))

    x = x_ref[...]
    h_in = _rms_norm(x, ng_ref[...]).astype(BF16)
    q_ref[...] =_dot(h_in, win_ref[:, :d_attn]) * (HEAD_DIM ** -0.5)
    kv = _dot(h_in, win_ref[:, d_attn:d_attn + 2 * d_kv])
    kn_ref[...] = kv[:, :d_kv]
    vn_ref[...] = kv[:, d_kv:]
    g_a = _dot(h_in, win_ref[:, d_attn + 2 * d_kv:d_attn + 2 * d_kv + d_attn])

    h_tm = _dot(perm_ref[...], h_in).astype(BF16)
    u_off = d_attn + 2 * d_kv + d_attn
    u_ab = _dot(h_tm, win_ref[:, u_off:u_off + 2 * d_conv])
    u_tm = u_ab[:, :d_conv] * jax.nn.sigmoid(u_ab[:, d_conv:])
    g_c_tm = _dot(h_tm, win_ref[:, u_off + 2 * d_conv:])

    lo = lax.broadcasted_iota(jnp.int32, (dec, LANES), 1) < HALF
    zero = jnp.zeros((dec, LANES), F32)
    sink = sinkcol_ref[...]
    bias = bias_ref[...]

    def scores(b):
        r = pl.multiple_of(b * dec, dec)
        k_all = jnp.concatenate([ck_ref[b].T, kn_ref[pl.ds(r, dec), :]], axis=0)
        ko_ref[b] = k_all[dec:, :]
        q_b = q_ref[pl.ds(r, dec), :]
        rows = []
        for h in range(N_HEADS):
            tile_q = q_b[:, (h // 2) * LANES:(h // 2 + 1) * LANES]
            in_lo = h % 2 == 0
            want_lo = h // GQA_GROUP == 0
            src = tile_q if in_lo == want_lo else pltpu.roll(tile_q, HALF, axis=1)
            rows.append(jnp.where(lo, src, zero) if want_lo else jnp.where(lo, zero, src))
        q_rows = jnp.concatenate(rows, axis=0).astype(BF16)
        return _dot_nt(q_rows, k_all.astype(BF16)) + bias

    def attend(b, s):
        r = pl.multiple_of(b * dec, dec)
        v_all = jnp.concatenate([cv_ref[b].T, vn_ref[pl.ds(r, dec), :]], axis=0)
        vo_ref[b] = v_all[dec:, :]
        p, rden = _softmax_sink(s, sink)
        o = _dot(p.astype(BF16), v_all.astype(BF16)) * rden
        tiles = []
        for pair in range(N_PAIRS):
            a = o[(2 * pair) * dec:(2 * pair + 1) * dec, :]
            c = o[(2 * pair + 1) * dec:(2 * pair + 2) * dec, :]
            if pair // PAIRS_PER_KV == 0:
                tiles.append(jnp.where(lo, a, pltpu.roll(c, HALF, axis=1)))
            else:
                tiles.append(jnp.where(lo, pltpu.roll(a, HALF, axis=1), c))
        attn_ref[pl.ds(r, dec), :] = jnp.concatenate(tiles, axis=1)

    upt_ref[:CONV_TAIL] = cs_ref[...]
    for tt in range(dec):
        upt_ref[CONV_TAIL + tt] = u_tm[tt * group:(tt + 1) * group, :]
    co_ref[...] = upt_ref[dec:]

    def conv_step(tt):
        for g0 in range(0, group, SUBLANES):
            acc = jnp.zeros((SUBLANES, d_conv), F32)
            for j in range(CONV_W):
                acc = acc + upt_ref[tt + j, g0:g0 + SUBLANES, :] * wb_ref[j]
            start = tt * group + g0
            if not isinstance(start, int):
                start = pl.multiple_of(start, SUBLANES)
            conv_ref[pl.ds(start, SUBLANES), :] = acc

    trips = group // SAMPLE_UNROLL
    conv_per_trip = dec // trips

    def attention_batch(i, carry):
        elems = [i * SAMPLE_UNROLL + e for e in range(SAMPLE_UNROLL)]
        ss = [scores(b) for b in elems]
        for b, s in zip(elems, ss):
            attend(b, s)
        for c in range(conv_per_trip):
            conv_step(i * conv_per_trip + c)
        return carry

    lax.fori_loop(0, trips, attention_batch, 0)
    for tt in range(trips * conv_per_trip, dec):
        conv_step(tt)

    c2_tm = _conv_branch_tail(conv_ref[...] + dwb_ref[...], lng_ref[...], lnb_ref[...], wpw_ref)
    m_c = _dot(perm_t_ref[...], _gate(c2_tm, g_c_tm)).astype(BF16)
    y_ref[...] = _mix_out(x, _gate(attn_ref[...], g_a), m_c, wout_ref, fg_ref[...], d_attn, final_norm)


def _sample_layer(x, ck, cv, cs, sinks, ng, win, dww, dwb, lng, lnb, wpw, wout, fg, final_norm):
    Bs, dec, D = x.shape
    group = SAMPLE_GROUP
    assert Bs % group == 0 and dec % 8 == 0 and dec <= CONV_TAIL
    d_conv = dww.shape[1]
    d_kv = N_KV_HEADS * HEAD_DIM
    d_attn = N_HEADS * HEAD_DIM
    rows = group * dec
    x2 = x.reshape(Bs * dec, D)
    perm = jnp.eye(rows, dtype=BF16).reshape(group, dec, rows).swapaxes(0, 1).reshape(rows, rows)
    perm_t = perm.T
    const = lambda shape: pl.BlockSpec(shape, lambda g: (0,) * len(shape))
    per_group = lambda shape: pl.BlockSpec((group,) + shape, lambda g: (g,) + (0,) * len(shape))
    conv_state = pl.BlockSpec((CONV_TAIL, group, d_conv), lambda g: (0, g, 0))
    y, ko, vo, co = pl.pallas_call(
        functools.partial(_sample_kernel, group=group, dec=dec, final_norm=final_norm),
        grid=(Bs // group,),
        in_specs=[
            pl.BlockSpec((rows, D), lambda g: (g, 0)),
            per_group((d_kv, WINDOW)), per_group((d_kv, WINDOW)), conv_state,
            pl.BlockSpec(memory_space=pltpu.SMEM), const(perm.shape), const(perm_t.shape),
            const(ng.shape), const(win.shape), const(dww.shape), const(dwb.shape),
            const(lng.shape), const(lnb.shape), const(wpw.shape), const(wout.shape), const(fg.shape),
        ],
        out_specs=[
            pl.BlockSpec((rows, D), lambda g: (g, 0)),
            per_group((WINDOW, d_kv)), per_group((WINDOW, d_kv)), conv_state,
        ],
        out_shape=[
            jax.ShapeDtypeStruct((Bs * dec, D), F32),
            jax.ShapeDtypeStruct((Bs, WINDOW, d_kv), F32),
            jax.ShapeDtypeStruct((Bs, WINDOW, d_kv), F32),
            jax.ShapeDtypeStruct((CONV_TAIL, Bs, d_conv), F32),
        ],
        scratch_shapes=[
            pltpu.VMEM((rows, d_attn), F32),
            pltpu.VMEM((rows, d_kv), F32),
            pltpu.VMEM((rows, d_kv), F32),
            pltpu.VMEM((CONV_TAIL + dec, group, d_conv), F32),
            pltpu.VMEM((CONV_W, SUBLANES, d_conv), F32),
            pltpu.VMEM((N_HEADS * dec, WINDOW + dec), F32),
            pltpu.VMEM((N_HEADS * dec, 1), F32),
            pltpu.VMEM((rows, d_attn), F32),
            pltpu.VMEM((rows, d_conv), F32),
        ],
        compiler_params=pltpu.CompilerParams(
            dimension_semantics=("arbitrary",), vmem_limit_bytes=VMEM_LIMIT_BYTES),
        name="sample_layer",
    )(x2, ck, cv, cs, sinks, perm, perm_t, ng, win, dww, dwb, lng, lnb, wpw, wout, fg)
    return y.reshape(Bs, dec, D), ko, vo, co


def kernel(x_prompt, x_sample, cache_k, cache_v, state_conv, norm_g, w_in, attn_sinks, dw_w, dw_b,
           conv_ln_g, conv_ln_b, w_pw2, w_out, final_norm_g):
    depth = w_in.shape[0]
    B = x_prompt.shape[0]
    Bs, dec = x_sample.shape[0], x_sample.shape[1]
    d_kv = N_KV_HEADS * HEAD_DIM
    fg = final_norm_g.reshape(1, -1)
    hp, hs = x_prompt, x_sample
    pk, pv, pc, sk, sv, sc = [], [], [], [], [], []
    kv_major = lambda c: c.transpose(0, 2, 3, 1).reshape(Bs, d_kv, WINDOW)
    for l in range(depth):
        final_norm = l == depth - 1
        row = lambda a: a[l].reshape(1, -1)
        win = w_in[l].astype(BF16)
        wpw = w_pw2[l].astype(BF16)
        wout = w_out[l].astype(BF16)
        shared = (row(norm_g), win, dw_w[l], row(dw_b), row(conv_ln_g), row(conv_ln_b), wpw, wout, fg)
        hp, k_p, v_p, c_p = _prompt_layer(hp, attn_sinks[l], *shared, final_norm)
        hs, k_s, v_s, c_s = _sample_layer(
            hs, kv_major(cache_k[l]), kv_major(cache_v[l]),
            state_conv[l].swapaxes(0, 1), attn_sinks[l], *shared, final_norm)
        key_major = lambda s: s.reshape(B, N_KV_HEADS, HEAD_DIM, WINDOW).transpose(0, 3, 1, 2)
        pk.append(key_major(k_p))
        pv.append(key_major(v_p))
        pc.append(c_p)
        sk.append(k_s.reshape(Bs, WINDOW, N_KV_HEADS, HEAD_DIM))
        sv.append(v_s.reshape(Bs, WINDOW, N_KV_HEADS, HEAD_DIM))
        sc.append(c_s.swapaxes(0, 1))
    stack = lambda xs: xs[0][None] if len(xs) == 1 else jnp.stack(xs)
    return (hp, hs, stack(pk), stack(pv), stack(pc), stack(sk), stack(sv), stack(sc))
```

```python
import functools

import jax
import jax.numpy as jnp
from jax import lax
from jax.experimental import pallas as pl
from jax.experimental.pallas import tpu as pltpu

HEAD_DIM = 64
N_HEADS = 8
N_KV_HEADS = 2
GQA_GROUP = N_HEADS // N_KV_HEADS
N_PAIRS = N_HEADS // 2
PAIRS_PER_KV = N_PAIRS // N_KV_HEADS
WINDOW = 128
CONV_W = 31
CONV_TAIL = CONV_W - 1
EPS = 1e-5
LANES = 128
SUBLANES = 8
HALF = LANES // 2
NEG_INF = float("-inf")
LOG2E = 1.4426950408889634

PROMPT_TILE = 512
CONV_ROWS = 64
CONV_CH_BLOCKS = 2
CONV_PAD = 32
CAST_STEPS = 8
SAMPLE_GROUP = 32
SAMPLE_UNROLL = 4
VMEM_LIMIT_BYTES = 56 * 1024 * 1024

F32 = jnp.float32
BF16 = jnp.bfloat16


def _alibi_slope(h):
    return 2.0 ** (-8.0 * (h + 1) / N_HEADS)


def _rms_norm(x, g):
    ms = jnp.mean(x * x, axis=-1, keepdims=True)
    return x * lax.rsqrt(ms + EPS) * g


def _silu(x):
    return x * jax.nn.sigmoid(x)


def _dot(a, b):
    return jnp.dot(a, b, preferred_element_type=F32)


def _dot_nt(a, b):
    return lax.dot_general(a, b, (((1,), (1,)), ((), ())), preferred_element_type=F32)


def _conv_branch_tail(c, lng, lnb, wpw_ref):
    mu = jnp.mean(c, axis=-1, keepdims=True)
    xc = c - mu
    var = jnp.mean(xc * xc, axis=-1, keepdims=True)
    y = xc * lax.rsqrt(var + EPS) * lng + lnb
    return _dot(_silu(y).astype(BF16), wpw_ref[...])


def _gate(val, g):
    return (val * _silu(g)).astype(BF16)


def _mix_out(x, m_a, m_c, wout_ref, fg, d_attn, final_norm):
    y = x + _dot(m_a, wout_ref[:d_attn, :]) + _dot(m_c, wout_ref[d_attn:, :])
    return _rms_norm(y, fg) if final_norm else y


def _softmax_sink(s, sink):
    m = jnp.maximum(jnp.max(s, axis=-1, keepdims=True), sink)
    p = jnp.exp(s - m)
    den = jnp.sum(p, axis=-1, keepdims=True) + jnp.exp(sink - m)
    return p, 1.0 / den


def _cast_kernel(*refs):
    n = len(refs) // 2
    for src, dst in zip(refs[:n], refs[n:]):
        dst[...] = src[...].astype(BF16)


def _cast_weights(*ws):
    steps = CAST_STEPS
    assert all(w.shape[0] % (steps * 2 * SUBLANES) == 0 for w in ws)
    spec = lambda w: pl.BlockSpec((w.shape[0] // steps, w.shape[1]), lambda i: (i, 0))
    return pl.pallas_call(
        _cast_kernel,
        grid=(steps,),
        in_specs=[spec(w) for w in ws],
        out_specs=[spec(w) for w in ws],
        out_shape=[jax.ShapeDtypeStruct(w.shape, BF16) for w in ws],
        name="cast_weights",
    )(*ws)


def _prompt_kernel(x_ref, sink_ref, ng_ref, win_ref, dww_ref, dwb_ref, lng_ref, lnb_ref,
                   wpw_ref, wout_ref, fg_ref,
                   y_ref, kst_ref, vst_ref, cst_ref,
                   k_ref, vt_ref, up_ref, shifted_ref, wb_ref, bias_ref, attn_ref, conv_ref,
                   *, tile, final_norm):
    t = pl.program_id(1)
    d_attn = N_HEADS * HEAD_DIM
    d_kv = N_KV_HEADS * HEAD_DIM
    d_conv = up_ref.shape[1]
    n_blocks = tile // WINDOW
    two_w = 2 * WINDOW
    u_off = d_attn + 2 * d_kv + d_attn
    n_shift = shifted_ref.shape[1]

    @pl.when(t == 0)
    def _init():
        k_ref[:WINDOW, :] = jnp.zeros((WINDOW, d_kv), BF16)
        vt_ref[:, :WINDOW] = jnp.zeros((d_kv, WINDOW), BF16)
        up_ref[:CONV_PAD, :] = jnp.zeros((CONV_PAD, d_conv), F32)
        for j in range(CONV_W):
            wb_ref[j] = jnp.broadcast_to(dww_ref[j:j + 1, :], (SUBLANES, d_conv))
        ji = lax.broadcasted_iota(jnp.int32, (two_w, two_w), 0)
        ci = lax.broadcasted_iota(jnp.int32, (two_w, two_w), 1)
        odd = ci >= WINDOW
        dist = jnp.where(odd, ci - WINDOW, ci) + WINDOW - ji
        valid = (dist >= 0) & (dist < WINDOW)
        valid_first = valid & (ji >= WINDOW)
        distf = dist.astype(F32) * LOG2E
        for pair in range(N_PAIRS):
            pen = -jnp.where(odd, _alibi_slope(2 * pair + 1), _alibi_slope(2 * pair)) * distf
            bias_ref[0, pair] = jnp.where(valid, pen, NEG_INF)
            bias_ref[1, pair] = jnp.where(valid_first, pen, NEG_INF)

    x = x_ref[...]
    h_in = _rms_norm(x, ng_ref[...]).astype(BF16)

    kv = _dot(h_in, win_ref[:, d_attn:d_attn + 2 * d_kv])
    k = kv[:, :d_kv]
    v = kv[:, d_kv:]
    k_ref[WINDOW:, :] = k.astype(BF16)
    v_t = v.T
    vt_ref[:, WINDOW:] = v_t.astype(BF16)
    kst_ref[...] = k[tile - WINDOW:, :].T
    vst_ref[...] = v_t[:, tile - WINDOW:]

    cw = d_conv // CONV_CH_BLOCKS
    for cb in range(CONV_CH_BLOCKS):
        ch = slice(cb * cw, (cb + 1) * cw)
        u_a = _dot(h_in, win_ref[:, u_off + cb * cw:u_off + (cb + 1) * cw])
        u_b = _dot(h_in, win_ref[:, u_off + d_conv + cb * cw:u_off + d_conv + (cb + 1) * cw])
        up_ref[CONV_PAD:, ch] = u_a * jax.nn.sigmoid(u_b)
        for s in range(1, SUBLANES):
            shifted_ref[s - 1, :, ch] = up_ref[s:s + n_shift, ch]
    cst_ref[...] = up_ref[CONV_PAD + tile - CONV_TAIL:, :]

    groups = CONV_ROWS // SUBLANES

    def conv_chunk(r, cb):
        ch = slice(cb * cw, (cb + 1) * cw)
        acc = [jnp.zeros((SUBLANES, cw), F32) for _ in range(groups)]
        loaded = {}
        for j in range(CONV_W):
            off = CONV_PAD - CONV_TAIL + j
            a, s = off // SUBLANES, off % SUBLANES
            wb = wb_ref[j, :, ch]
            for g in range(groups):
                if (s, a + g) not in loaded:
                    rows = slice(r + (a + g) * SUBLANES, r + (a + g + 1) * SUBLANES)
                    loaded[s, a + g] = up_ref[rows, ch] if s == 0 else shifted_ref[s - 1, rows, ch]
                acc[g] = acc[g] + loaded[s, a + g] * wb
        for g in range(groups):
            conv_ref[r + g * SUBLANES:r + (g + 1) * SUBLANES, ch] = acc[g]

    q = _dot(h_in, win_ref[:, :d_attn]) * (HEAD_DIM ** -0.5 * LOG2E)
    g_a = _dot(h_in, win_ref[:, d_attn + 2 * d_kv:u_off])
    g_c = _dot(h_in, win_ref[:, u_off + 2 * d_conv:])

    lo = lax.broadcasted_iota(jnp.int32, (WINDOW, LANES), 1) < HALF
    zero = jnp.zeros((WINDOW, LANES), F32)
    odd_head = lax.broadcasted_iota(jnp.int32, (1, two_w), 1) >= WINDOW

    def scores(blk, pair):
        r0 = blk * WINDOW
        first = jnp.where(t == 0, 1, 0) if blk == 0 else 0
        kvh = pair // PAIRS_PER_KV
        qp = q[r0:r0 + WINDOW, pair * LANES:(pair + 1) * LANES]
        rolled = pltpu.roll(qp, HALF, axis=1)
        if kvh == 0:
            q_even, q_odd = jnp.where(lo, qp, zero), jnp.where(lo, rolled, zero)
        else:
            q_even, q_odd = jnp.where(lo, zero, rolled), jnp.where(lo, zero, qp)
        qm = jnp.concatenate([q_even, q_odd], axis=0).astype(BF16)
        return _dot_nt(k_ref[r0:r0 + two_w, :], qm) + bias_ref[first, pair]

    def attend(blk, pair, s):
        r0 = blk * WINDOW
        kvh = pair // PAIRS_PER_KV
        sink = jnp.where(odd_head, sink_ref[2 * pair + 1], sink_ref[2 * pair]) * LOG2E
        m = jnp.maximum(jnp.max(s, axis=0, keepdims=True), sink)
        p = jnp.exp2(s - m)
        den = jnp.sum(p, axis=0, keepdims=True) + jnp.exp2(sink - m)
        vt_win = vt_ref[kvh * HEAD_DIM:(kvh + 1) * HEAD_DIM, r0:r0 + two_w]
        ot = _dot(vt_win, p.astype(BF16)) * (1.0 / den)
        o = jnp.concatenate([ot[:, :WINDOW], ot[:, WINDOW:]], axis=0).T
        attn_ref[r0:r0 + WINDOW, pair * LANES:(pair + 1) * LANES] = o

    steps = [(blk, pair) for blk in range(n_blocks) for pair in range(N_PAIRS)]
    s_next = scores(*steps[0])
    for i, step in enumerate(steps):
        s_cur = s_next
        if i + 1 < len(steps):
            s_next = scores(*steps[i + 1])
        attend(*step, s_cur)

    y_a = _dot(_gate(attn_ref[...], g_a), wout_ref[:d_attn, :])
    for cb in range(CONV_CH_BLOCKS):
        for r in range(0, tile, CONV_ROWS):
            conv_chunk(r, cb)

    c2 = _conv_branch_tail(conv_ref[...] + dwb_ref[...], lng_ref[...], lnb_ref[...], wpw_ref)
    y = x + y_a + _dot(_gate(c2, g_c), wout_ref[d_attn:, :])
    y_ref[...] = _rms_norm(y, fg_ref[...]) if final_norm else y

    k_ref[:WINDOW, :] = k_ref[tile:, :]
    vt_ref[:, :WINDOW] = vt_ref[:, tile:]
    up_ref[:CONV_PAD, :] = up_ref[tile:, :]


def _prompt_layer(x, sinks, ng, win, dww, dwb, lng, lnb, wpw, wout, fg, final_norm):
    B, T, D = x.shape
    tile = PROMPT_TILE
    assert T % tile == 0 and tile % WINDOW == 0 and tile % CONV_ROWS == 0
    d_conv = dww.shape[1]
    d_kv = N_KV_HEADS * HEAD_DIM
    d_attn = N_HEADS * HEAD_DIM
    n_t = T // tile
    const = lambda shape: pl.BlockSpec(shape, lambda b, t: (0,) * len(shape))
    return pl.pallas_call(
        functools.partial(_prompt_kernel, tile=tile, final_norm=final_norm),
        grid=(B, n_t),
        in_specs=[
            pl.BlockSpec((None, tile, D), lambda b, t: (b, t, 0)),
            pl.BlockSpec(memory_space=pltpu.SMEM),
            const(ng.shape), const(win.shape), const(dww.shape), const(dwb.shape),
            const(lng.shape), const(lnb.shape), const(wpw.shape), const(wout.shape), const(fg.shape),
        ],
        out_specs=[
            pl.BlockSpec((None, tile, D), lambda b, t: (b, t, 0)),
            pl.BlockSpec((None, d_kv, WINDOW), lambda b, t: (b, 0, 0)),
            pl.BlockSpec((None, d_kv, WINDOW), lambda b, t: (b, 0, 0)),
            pl.BlockSpec((None, CONV_TAIL, d_conv), lambda b, t: (b, 0, 0)),
        ],
        scratch_shapes=[
            pltpu.VMEM((WINDOW + tile, d_kv), BF16),
            pltpu.VMEM((d_kv, WINDOW + tile), BF16),
            pltpu.VMEM((CONV_PAD + tile, d_conv), F32),
            pltpu.VMEM((SUBLANES - 1, CONV_PAD - SUBLANES + tile, d_conv), F32),
            pltpu.VMEM((CONV_W, SUBLANES, d_conv), F32),
            pltpu.VMEM((2, N_PAIRS, 2 * WINDOW, 2 * WINDOW), F32),
            pltpu.VMEM((tile, d_attn), F32),
            pltpu.VMEM((tile, d_conv), F32),
        ],
        out_shape=[
            jax.ShapeDtypeStruct((B, T, D), F32),
            jax.ShapeDtypeStruct((B, d_kv, WINDOW), F32),
            jax.ShapeDtypeStruct((B, d_kv, WINDOW), F32),
            jax.ShapeDtypeStruct((B, CONV_TAIL, d_conv), F32),
        ],
        compiler_params=pltpu.CompilerParams(
            dimension_semantics=("arbitrary", "arbitrary"), vmem_limit_bytes=VMEM_LIMIT_BYTES),
        name="prompt_layer",
    )(x, sinks, ng, win, dww, dwb, lng, lnb, wpw, wout, fg)


def _sample_kernel(x_ref, ck_ref, cv_ref, cs_ref, sink_ref, ng_ref, win_ref,
                   dww_ref, dwb_ref, lng_ref, lnb_ref, wpw_ref, wout_ref, fg_ref,
                   y_ref, ko_ref, vo_ref, co_ref,
                   q_ref, kn_ref, vn_ref, upt_ref, wb_ref, bias_ref, sinkcol_ref, attn_ref, conv_ref,
                   perm_ref, perm_t_ref, *, group, dec, final_norm):
    d_attn = N_HEADS * HEAD_DIM
    d_kv = N_KV_HEADS * HEAD_DIM
    d_conv = conv_ref.shape[1]
    n_keys = WINDOW + dec
    rows_q = N_HEADS * dec

    @pl.when(pl.program_id(0) == 0)
    def _init():
        ri = lax.broadcasted_iota(jnp.int32, (rows_q, n_keys), 0)
        si = lax.broadcasted_iota(jnp.int32, (rows_q, n_keys), 1)
        ti = ri
        slope = jnp.zeros((rows_q, n_keys), F32)
        for h in range(N_HEADS):
            in_head = (ri >= h * dec) & (ri < (h + 1) * dec)
            ti = jnp.where(in_head, ri - h * dec, ti)
            slope = jnp.where(in_head, _alibi_slope(h), slope)
        dist = WINDOW + ti - si
        valid = (dist >= 0) & (dist < WINDOW)
        distf = dist.astype(F32)
        bias_ref[...] = jnp.where(valid, -slope * distf, NEG_INF)
        rc = lax.broadcasted_iota(jnp.int32, (rows_q, 1), 0)
        sink_rows = jnp.zeros((rows_q, 1), F32)
        for h in range(N_HEADS):
            sink_rows = jnp.where((rc >= h * dec) & (rc < (h + 1) * dec), sink_ref[h], sink_rows)
        sinkcol_ref[...] = sink_rows
        for j in range(CONV_W):
            wb_ref[j] = jnp.broadcast_to(dww_ref[j:j + 1, :], (SUBLANES, d_conv))
        n_rows = group * dec
        pi = lax.broadcasted_iota(jnp.int32, (n_rows, n_rows), 0)
        pj = lax.broadcasted_iota(jnp.int32, (n_rows, n_rows), 1)
        shift = group.bit_length() - 1
        source = lambda i: (i & (group - 1)) * dec + lax.shift_right_logical(i, shift)
        perm_ref[...] = jnp.where(pj == source(pi), 1.0, 0.0).astype(BF16)
        perm_t_ref[...] = jnp.where(pi == source(pj), 1.0, 0.0).astype(BF16)

    x = x_ref[...]
    h_in = _rms_norm(x, ng_ref[...]).astype(BF16)
    q_ref[...] =_dot(h_in, win_ref[:, :d_attn]) * (HEAD_DIM ** -0.5)
    kv = _dot(h_in, win_ref[:, d_attn:d_attn + 2 * d_kv])
    kn_ref[...] = kv[:, :d_kv]
    vn_ref[...] = kv[:, d_kv:]
    g_a = _dot(h_in, win_ref[:, d_attn + 2 * d_kv:d_attn + 2 * d_kv + d_attn])

    h_tm = _dot(perm_ref[...], h_in).astype(BF16)
    u_off = d_attn + 2 * d_kv + d_attn
    u_ab = _dot(h_tm, win_ref[:, u_off:u_off + 2 * d_conv])
    u_tm = u_ab[:, :d_conv] * jax.nn.sigmoid(u_ab[:, d_conv:])
    g_c_tm = _dot(h_tm, win_ref[:, u_off + 2 * d_conv:])

    lo = lax.broadcasted_iota(jnp.int32, (dec, LANES), 1) < HALF
    zero = jnp.zeros((dec, LANES), F32)
    sink = sinkcol_ref[...]
    bias = bias_ref[...]

    def scores(b):
        r = pl.multiple_of(b * dec, dec)
        k_all = jnp.concatenate([ck_ref[b].T, kn_ref[pl.ds(r, dec), :]], axis=0)
        ko_ref[b] = k_all[dec:, :]
        q_b = q_ref[pl.ds(r, dec), :]
        rows = []
        for h in range(N_HEADS):
            tile_q = q_b[:, (h // 2) * LANES:(h // 2 + 1) * LANES]
            in_lo = h % 2 == 0
            want_lo = h // GQA_GROUP == 0
            src = tile_q if in_lo == want_lo else pltpu.roll(tile_q, HALF, axis=1)
            rows.append(jnp.where(lo, src, zero) if want_lo else jnp.where(lo, zero, src))
        q_rows = jnp.concatenate(rows, axis=0).astype(BF16)
        return _dot_nt(q_rows, k_all.astype(BF16)) + bias

    def attend(b, s):
        r = pl.multiple_of(b * dec, dec)
        v_all = jnp.concatenate([cv_ref[b].T, vn_ref[pl.ds(r, dec), :]], axis=0)
        vo_ref[b] = v_all[dec:, :]
        p, rden = _softmax_sink(s, sink)
        o = _dot(p.astype(BF16), v_all.astype(BF16)) * rden
        tiles = []
        for pair in range(N_PAIRS):
            a = o[(2 * pair) * dec:(2 * pair + 1) * dec, :]
            c = o[(2 * pair + 1) * dec:(2 * pair + 2) * dec, :]
            if pair // PAIRS_PER_KV == 0:
                tiles.append(jnp.where(lo, a, pltpu.roll(c, HALF, axis=1)))
            else:
                tiles.append(jnp.where(lo, pltpu.roll(a, HALF, axis=1), c))
        attn_ref[pl.ds(r, dec), :] = jnp.concatenate(tiles, axis=1)

    upt_ref[:CONV_TAIL] = cs_ref[...]
    for tt in range(dec):
        upt_ref[CONV_TAIL + tt] = u_tm[tt * group:(tt + 1) * group, :]
    co_ref[...] = upt_ref[dec:]

    def conv_step(tt):
        for g0 in range(0, group, SUBLANES):
            acc = jnp.zeros((SUBLANES, d_conv), F32)
            for j in range(CONV_W):
                acc = acc + upt_ref[tt + j, g0:g0 + SUBLANES, :] * wb_ref[j]
            start = tt * group + g0
            if not isinstance(start, int):
                start = pl.multiple_of(start, SUBLANES)
            conv_ref[pl.ds(start, SUBLANES), :] = acc

    trips = group // SAMPLE_UNROLL
    conv_per_trip = dec // trips

    def attention_batch(i, carry):
        elems = [i * SAMPLE_UNROLL + e for e in range(SAMPLE_UNROLL)]
        ss = [scores(b) for b in elems]
        for b, s in zip(elems, ss):
            attend(b, s)
        for c in range(conv_per_trip):
            conv_step(i * conv_per_trip + c)
        return carry

    lax.fori_loop(0, trips, attention_batch, 0)
    for tt in range(trips * conv_per_trip, dec):
        conv_step(tt)

    c2_tm = _conv_branch_tail(conv_ref[...] + dwb_ref[...], lng_ref[...], lnb_ref[...], wpw_ref)
    m_c = _dot(perm_t_ref[...], _gate(c2_tm, g_c_tm)).astype(BF16)
    y_ref[...] = _mix_out(x, _gate(attn_ref[...], g_a), m_c, wout_ref, fg_ref[...], d_attn, final_norm)


def _sample_layer(x, ck, cv, cs, sinks, ng, win, dww, dwb, lng, lnb, wpw, wout, fg, final_norm):
    Bs, dec, D = x.shape
    group = SAMPLE_GROUP
    assert Bs % group == 0 and dec % 8 == 0 and dec <= CONV_TAIL
    d_conv = dww.shape[1]
    d_kv = N_KV_HEADS * HEAD_DIM
    d_attn = N_HEADS * HEAD_DIM
    rows = group * dec
    x2 = x.reshape(Bs * dec, D)
    assert group & (group - 1) == 0
    const = lambda shape: pl.BlockSpec(shape, lambda g: (0,) * len(shape))
    per_group = lambda shape: pl.BlockSpec((group,) + shape, lambda g: (g,) + (0,) * len(shape))
    conv_state = pl.BlockSpec((CONV_TAIL, group, d_conv), lambda g: (0, g, 0))
    y, ko, vo, co = pl.pallas_call(
        functools.partial(_sample_kernel, group=group, dec=dec, final_norm=final_norm),
        grid=(Bs // group,),
        in_specs=[
            pl.BlockSpec((rows, D), lambda g: (g, 0)),
            per_group((d_kv, WINDOW)), per_group((d_kv, WINDOW)), conv_state,
            pl.BlockSpec(memory_space=pltpu.SMEM),
            const(ng.shape), const(win.shape), const(dww.shape), const(dwb.shape),
            const(lng.shape), const(lnb.shape), const(wpw.shape), const(wout.shape), const(fg.shape),
        ],
        out_specs=[
            pl.BlockSpec((rows, D), lambda g: (g, 0)),
            per_group((WINDOW, d_kv)), per_group((WINDOW, d_kv)), conv_state,
        ],
        out_shape=[
            jax.ShapeDtypeStruct((Bs * dec, D), F32),
            jax.ShapeDtypeStruct((Bs, WINDOW, d_kv), F32),
            jax.ShapeDtypeStruct((Bs, WINDOW, d_kv), F32),
            jax.ShapeDtypeStruct((CONV_TAIL, Bs, d_conv), F32),
        ],
        scratch_shapes=[
            pltpu.VMEM((rows, d_attn), F32),
            pltpu.VMEM((rows, d_kv), F32),
            pltpu.VMEM((rows, d_kv), F32),
            pltpu.VMEM((CONV_TAIL + dec, group, d_conv), F32),
            pltpu.VMEM((CONV_W, SUBLANES, d_conv), F32),
            pltpu.VMEM((N_HEADS * dec, WINDOW + dec), F32),
            pltpu.VMEM((N_HEADS * dec, 1), F32),
            pltpu.VMEM((rows, d_attn), F32),
            pltpu.VMEM((rows, d_conv), F32),
            pltpu.VMEM((rows, rows), BF16),
            pltpu.VMEM((rows, rows), BF16),
        ],
        compiler_params=pltpu.CompilerParams(
            dimension_semantics=("arbitrary",), vmem_limit_bytes=VMEM_LIMIT_BYTES),
        name="sample_layer",
    )(x2, ck, cv, cs, sinks, ng, win, dww, dwb, lng, lnb, wpw, wout, fg)
    return y.reshape(Bs, dec, D), ko, vo, co


def kernel(x_prompt, x_sample, cache_k, cache_v, state_conv, norm_g, w_in, attn_sinks, dw_w, dw_b,
           conv_ln_g, conv_ln_b, w_pw2, w_out, final_norm_g):
    depth = w_in.shape[0]
    B = x_prompt.shape[0]
    Bs, dec = x_sample.shape[0], x_sample.shape[1]
    d_kv = N_KV_HEADS * HEAD_DIM
    fg = final_norm_g.reshape(1, -1)
    hp, hs = x_prompt, x_sample
    pk, pv, pc, sk, sv, sc = [], [], [], [], [], []
    kv_major = lambda c: c.transpose(0, 2, 3, 1).reshape(Bs, d_kv, WINDOW)
    for l in range(depth):
        final_norm = l == depth - 1
        row = lambda a: a[l].reshape(1, -1)
        win, wpw, wout = _cast_weights(w_in[l], w_pw2[l], w_out[l])
        shared = (row(norm_g), win, dw_w[l], row(dw_b), row(conv_ln_g), row(conv_ln_b), wpw, wout, fg)
        hp, k_p, v_p, c_p = _prompt_layer(hp, attn_sinks[l], *shared, final_norm)
        hs, k_s, v_s, c_s = _sample_layer(
            hs, kv_major(cache_k[l]), kv_major(cache_v[l]),
            state_conv[l].swapaxes(0, 1), attn_sinks[l], *shared, final_norm)
        key_major = lambda s: s.reshape(B, N_KV_HEADS, HEAD_DIM, WINDOW).transpose(0, 3, 1, 2)
        pk.append(key_major(k_p))
        pv.append(key_major(v_p))
        pc.append(c_p)
        sk.append(k_s.reshape(Bs, WINDOW, N_KV_HEADS, HEAD_DIM))
        sv.append(v_s.reshape(Bs, WINDOW, N_KV_HEADS, HEAD_DIM))
        sc.append(c_s.swapaxes(0, 1))
    stack = lambda xs: xs[0][None] if len(xs) == 1 else jnp.stack(xs)
    return (hp, hs, stack(pk), stack(pv), stack(pc), stack(sk), stack(sv), stack(sc))
```

```python
import functools

import jax
import jax.numpy as jnp
from jax import lax
from jax.experimental import pallas as pl
from jax.experimental.pallas import tpu as pltpu

HEAD_DIM = 64
N_HEADS = 8
N_KV_HEADS = 2
GQA_GROUP = N_HEADS // N_KV_HEADS
N_PAIRS = N_HEADS // 2
PAIRS_PER_KV = N_PAIRS // N_KV_HEADS
WINDOW = 128
CONV_W = 31
CONV_TAIL = CONV_W - 1
EPS = 1e-5
LANES = 128
SUBLANES = 8
HALF = LANES // 2
NEG_INF = float("-inf")
LOG2E = 1.4426950408889634

PROMPT_TILE = 512
CONV_ROWS = 64
CONV_CH_BLOCKS = 2
KV_AFTER_BLOCK = 1
CONV_PAD = 32
CAST_STEPS = 8
SAMPLE_GROUP = 32
SAMPLE_UNROLL = 4
VMEM_LIMIT_BYTES = 56 * 1024 * 1024

F32 = jnp.float32
BF16 = jnp.bfloat16


def _alibi_slope(h):
    return 2.0 ** (-8.0 * (h + 1) / N_HEADS)


def _rms_norm(x, g):
    ms = jnp.mean(x * x, axis=-1, keepdims=True)
    return x * lax.rsqrt(ms + EPS) * g


def _silu(x):
    return x * jax.nn.sigmoid(x)


def _dot(a, b):
    return jnp.dot(a, b, preferred_element_type=F32)


def _dot_nt(a, b):
    return lax.dot_general(a, b, (((1,), (1,)), ((), ())), preferred_element_type=F32)


def _conv_branch_tail(c, lng, lnb, wpw_ref):
    mu = jnp.mean(c, axis=-1, keepdims=True)
    xc = c - mu
    var = jnp.mean(xc * xc, axis=-1, keepdims=True)
    y = xc * lax.rsqrt(var + EPS) * lng + lnb
    return _dot(_silu(y).astype(BF16), wpw_ref[...])


def _gate(val, g):
    return (val * _silu(g)).astype(BF16)


def _mix_out(x, m_a, m_c, wout_ref, fg, d_attn, final_norm):
    y = x + _dot(m_a, wout_ref[:d_attn, :]) + _dot(m_c, wout_ref[d_attn:, :])
    return _rms_norm(y, fg) if final_norm else y


def _softmax_sink(s, sink):
    m = jnp.maximum(jnp.max(s, axis=-1, keepdims=True), sink)
    p = jnp.exp(s - m)
    den = jnp.sum(p, axis=-1, keepdims=True) + jnp.exp(sink - m)
    return p, 1.0 / den


def _cast_kernel(*refs):
    n = len(refs) // 2
    for src, dst in zip(refs[:n], refs[n:]):
        dst[...] = src[...].astype(BF16)


def _cast_weights(*ws):
    steps = CAST_STEPS
    assert all(w.shape[0] % (steps * 2 * SUBLANES) == 0 for w in ws)
    spec = lambda w: pl.BlockSpec((w.shape[0] // steps, w.shape[1]), lambda i: (i, 0))
    return pl.pallas_call(
        _cast_kernel,
        grid=(steps,),
        in_specs=[spec(w) for w in ws],
        out_specs=[spec(w) for w in ws],
        out_shape=[jax.ShapeDtypeStruct(w.shape, BF16) for w in ws],
        name="cast_weights",
    )(*ws)


def _prompt_kernel(x_ref, sink_ref, ng_ref, win_ref, dww_ref, dwb_ref, lng_ref, lnb_ref,
                   wpw_ref, wout_ref, fg_ref,
                   y_ref, kst_ref, vst_ref, cst_ref,
                   k_ref, vt_ref, up_ref, shifted_ref, wb_ref, bias_ref, attn_ref, conv_ref,
                   *, tile, final_norm):
    t = pl.program_id(1)
    d_attn = N_HEADS * HEAD_DIM
    d_kv = N_KV_HEADS * HEAD_DIM
    d_conv = up_ref.shape[1]
    n_blocks = tile // WINDOW
    two_w = 2 * WINDOW
    u_off = d_attn + 2 * d_kv + d_attn
    n_shift = shifted_ref.shape[1]

    @pl.when(t == 0)
    def _init():
        k_ref[:WINDOW, :] = jnp.zeros((WINDOW, d_kv), BF16)
        vt_ref[:, :WINDOW] = jnp.zeros((d_kv, WINDOW), BF16)
        up_ref[:CONV_PAD, :] = jnp.zeros((CONV_PAD, d_conv), F32)
        for j in range(CONV_W):
            wb_ref[j] = jnp.broadcast_to(dww_ref[j:j + 1, :], (SUBLANES, d_conv))
        ji = lax.broadcasted_iota(jnp.int32, (two_w, two_w), 0)
        ci = lax.broadcasted_iota(jnp.int32, (two_w, two_w), 1)
        odd = ci >= WINDOW
        dist = jnp.where(odd, ci - WINDOW, ci) + WINDOW - ji
        valid = (dist >= 0) & (dist < WINDOW)
        valid_first = valid & (ji >= WINDOW)
        distf = dist.astype(F32) * LOG2E
        for pair in range(N_PAIRS):
            pen = -jnp.where(odd, _alibi_slope(2 * pair + 1), _alibi_slope(2 * pair)) * distf
            bias_ref[0, pair] = jnp.where(valid, pen, NEG_INF)
            bias_ref[1, pair] = jnp.where(valid_first, pen, NEG_INF)

    x = x_ref[...]
    h_in = _rms_norm(x, ng_ref[...]).astype(BF16)

    def project_kv():
        kv = _dot(h_in, win_ref[:, d_attn:d_attn + 2 * d_kv])
        k = kv[:, :d_kv]
        v = kv[:, d_kv:]
        k_ref[WINDOW:, :] = k.astype(BF16)
        v_t = v.T
        vt_ref[:, WINDOW:] = v_t.astype(BF16)
        kst_ref[...] = k[tile - WINDOW:, :].T
        vst_ref[...] = v_t[:, tile - WINDOW:]

    cw = d_conv // CONV_CH_BLOCKS
    for cb in range(CONV_CH_BLOCKS):
        ch = slice(cb * cw, (cb + 1) * cw)
        u_a = _dot(h_in, win_ref[:, u_off + cb * cw:u_off + (cb + 1) * cw])
        u_b = _dot(h_in, win_ref[:, u_off + d_conv + cb * cw:u_off + d_conv + (cb + 1) * cw])
        up_ref[CONV_PAD:, ch] = u_a * jax.nn.sigmoid(u_b)
        for s in range(1, SUBLANES):
            shifted_ref[s - 1, :, ch] = up_ref[s:s + n_shift, ch]
        if cb == KV_AFTER_BLOCK:
            project_kv()
    cst_ref[...] = up_ref[CONV_PAD + tile - CONV_TAIL:, :]

    groups = CONV_ROWS // SUBLANES

    def conv_chunk(r, cb):
        ch = slice(cb * cw, (cb + 1) * cw)
        acc = [jnp.zeros((SUBLANES, cw), F32) for _ in range(groups)]
        loaded = {}
        for j in range(CONV_W):
            off = CONV_PAD - CONV_TAIL + j
            a, s = off // SUBLANES, off % SUBLANES
            wb = wb_ref[j, :, ch]
            for g in range(groups):
                if (s, a + g) not in loaded:
                    rows = slice(r + (a + g) * SUBLANES, r + (a + g + 1) * SUBLANES)
                    loaded[s, a + g] = up_ref[rows, ch] if s == 0 else shifted_ref[s - 1, rows, ch]
                acc[g] = acc[g] + loaded[s, a + g] * wb
        for g in range(groups):
            conv_ref[r + g * SUBLANES:r + (g + 1) * SUBLANES, ch] = acc[g]

    q = _dot(h_in, win_ref[:, :d_attn]) * (HEAD_DIM ** -0.5 * LOG2E)
    g_a = _dot(h_in, win_ref[:, d_attn + 2 * d_kv:u_off])
    g_c = _dot(h_in, win_ref[:, u_off + 2 * d_conv:])

    lo = lax.broadcasted_iota(jnp.int32, (WINDOW, LANES), 1) < HALF
    zero = jnp.zeros((WINDOW, LANES), F32)
    odd_head = lax.broadcasted_iota(jnp.int32, (1, two_w), 1) >= WINDOW

    def scores(blk, pair):
        r0 = blk * WINDOW
        first = jnp.where(t == 0, 1, 0) if blk == 0 else 0
        kvh = pair // PAIRS_PER_KV
        qp = q[r0:r0 + WINDOW, pair * LANES:(pair + 1) * LANES]
        rolled = pltpu.roll(qp, HALF, axis=1)
        if kvh == 0:
            q_even, q_odd = jnp.where(lo, qp, zero), jnp.where(lo, rolled, zero)
        else:
            q_even, q_odd = jnp.where(lo, zero, rolled), jnp.where(lo, zero, qp)
        qm = jnp.concatenate([q_even, q_odd], axis=0).astype(BF16)
        return _dot_nt(k_ref[r0:r0 + two_w, :], qm) + bias_ref[first, pair]

    def attend(blk, pair, s):
        r0 = blk * WINDOW
        kvh = pair // PAIRS_PER_KV
        sink = jnp.where(odd_head, sink_ref[2 * pair + 1], sink_ref[2 * pair]) * LOG2E
        m = jnp.maximum(jnp.max(s, axis=0, keepdims=True), sink)
        p = jnp.exp2(s - m)
        den = jnp.sum(p, axis=0, keepdims=True) + jnp.exp2(sink - m)
        vt_win = vt_ref[kvh * HEAD_DIM:(kvh + 1) * HEAD_DIM, r0:r0 + two_w]
        ot = _dot(vt_win, p.astype(BF16)) * (1.0 / den)
        o = jnp.concatenate([ot[:, :WINDOW], ot[:, WINDOW:]], axis=0).T
        attn_ref[r0:r0 + WINDOW, pair * LANES:(pair + 1) * LANES] = o

    steps = [(blk, pair) for blk in range(n_blocks) for pair in range(N_PAIRS)]
    s_next = scores(*steps[0])
    for i, step in enumerate(steps):
        s_cur = s_next
        if i + 1 < len(steps):
            s_next = scores(*steps[i + 1])
        attend(*step, s_cur)

    y_a = _dot(_gate(attn_ref[...], g_a), wout_ref[:d_attn, :])
    for cb in range(CONV_CH_BLOCKS):
        for r in range(0, tile, CONV_ROWS):
            conv_chunk(r, cb)

    c2 = _conv_branch_tail(conv_ref[...] + dwb_ref[...], lng_ref[...], lnb_ref[...], wpw_ref)
    y = x + y_a + _dot(_gate(c2, g_c), wout_ref[d_attn:, :])
    y_ref[...] = _rms_norm(y, fg_ref[...]) if final_norm else y

    k_ref[:WINDOW, :] = k_ref[tile:, :]
    vt_ref[:, :WINDOW] = vt_ref[:, tile:]
    up_ref[:CONV_PAD, :] = up_ref[tile:, :]


def _prompt_layer(x, sinks, ng, win, dww, dwb, lng, lnb, wpw, wout, fg, final_norm):
    B, T, D = x.shape
    tile = PROMPT_TILE
    assert T % tile == 0 and tile % WINDOW == 0 and tile % CONV_ROWS == 0
    d_conv = dww.shape[1]
    d_kv = N_KV_HEADS * HEAD_DIM
    d_attn = N_HEADS * HEAD_DIM
    n_t = T // tile
    const = lambda shape: pl.BlockSpec(shape, lambda b, t: (0,) * len(shape))
    return pl.pallas_call(
        functools.partial(_prompt_kernel, tile=tile, final_norm=final_norm),
        grid=(B, n_t),
        in_specs=[
            pl.BlockSpec((None, tile, D), lambda b, t: (b, t, 0)),
            pl.BlockSpec(memory_space=pltpu.SMEM),
            const(ng.shape), const(win.shape), const(dww.shape), const(dwb.shape),
            const(lng.shape), const(lnb.shape), const(wpw.shape), const(wout.shape), const(fg.shape),
        ],
        out_specs=[
            pl.BlockSpec((None, tile, D), lambda b, t: (b, t, 0)),
            pl.BlockSpec((None, d_kv, WINDOW), lambda b, t: (b, 0, 0)),
            pl.BlockSpec((None, d_kv, WINDOW), lambda b, t: (b, 0, 0)),
            pl.BlockSpec((None, CONV_TAIL, d_conv), lambda b, t: (b, 0, 0)),
        ],
        scratch_shapes=[
            pltpu.VMEM((WINDOW + tile, d_kv), BF16),
            pltpu.VMEM((d_kv, WINDOW + tile), BF16),
            pltpu.VMEM((CONV_PAD + tile, d_conv), F32),
            pltpu.VMEM((SUBLANES - 1, CONV_PAD - SUBLANES + tile, d_conv), F32),
            pltpu.VMEM((CONV_W, SUBLANES, d_conv), F32),
            pltpu.VMEM((2, N_PAIRS, 2 * WINDOW, 2 * WINDOW), F32),
            pltpu.VMEM((tile, d_attn), F32),
            pltpu.VMEM((tile, d_conv), F32),
        ],
        out_shape=[
            jax.ShapeDtypeStruct((B, T, D), F32),
            jax.ShapeDtypeStruct((B, d_kv, WINDOW), F32),
            jax.ShapeDtypeStruct((B, d_kv, WINDOW), F32),
            jax.ShapeDtypeStruct((B, CONV_TAIL, d_conv), F32),
        ],
        compiler_params=pltpu.CompilerParams(
            dimension_semantics=("arbitrary", "arbitrary"), vmem_limit_bytes=VMEM_LIMIT_BYTES),
        name="prompt_layer",
    )(x, sinks, ng, win, dww, dwb, lng, lnb, wpw, wout, fg)


def _sample_kernel(x_ref, ck_ref, cv_ref, cs_ref, sink_ref, ng_ref, win_ref,
                   dww_ref, dwb_ref, lng_ref, lnb_ref, wpw_ref, wout_ref, fg_ref,
                   y_ref, ko_ref, vo_ref, co_ref,
                   q_ref, kn_ref, vn_ref, upt_ref, wb_ref, bias_ref, sinkcol_ref, attn_ref, conv_ref,
                   perm_ref, perm_t_ref, *, group, dec, final_norm):
    d_attn = N_HEADS * HEAD_DIM
    d_kv = N_KV_HEADS * HEAD_DIM
    d_conv = conv_ref.shape[1]
    n_keys = WINDOW + dec
    rows_q = N_HEADS * dec

    @pl.when(pl.program_id(0) == 0)
    def _init():
        ri = lax.broadcasted_iota(jnp.int32, (rows_q, n_keys), 0)
        si = lax.broadcasted_iota(jnp.int32, (rows_q, n_keys), 1)
        ti = ri
        slope = jnp.zeros((rows_q, n_keys), F32)
        for h in range(N_HEADS):
            in_head = (ri >= h * dec) & (ri < (h + 1) * dec)
            ti = jnp.where(in_head, ri - h * dec, ti)
            slope = jnp.where(in_head, _alibi_slope(h), slope)
        dist = WINDOW + ti - si
        valid = (dist >= 0) & (dist < WINDOW)
        distf = dist.astype(F32)
        bias_ref[...] = jnp.where(valid, -slope * distf, NEG_INF)
        rc = lax.broadcasted_iota(jnp.int32, (rows_q, 1), 0)
        sink_rows = jnp.zeros((rows_q, 1), F32)
        for h in range(N_HEADS):
            sink_rows = jnp.where((rc >= h * dec) & (rc < (h + 1) * dec), sink_ref[h], sink_rows)
        sinkcol_ref[...] = sink_rows
        for j in range(CONV_W):
            wb_ref[j] = jnp.broadcast_to(dww_ref[j:j + 1, :], (SUBLANES, d_conv))
        n_rows = group * dec
        pi = lax.broadcasted_iota(jnp.int32, (n_rows, n_rows), 0)
        pj = lax.broadcasted_iota(jnp.int32, (n_rows, n_rows), 1)
        shift = group.bit_length() - 1
        source = lambda i: (i & (group - 1)) * dec + lax.shift_right_logical(i, shift)
        perm_ref[...] = jnp.where(pj == source(pi), 1.0, 0.0).astype(BF16)
        perm_t_ref[...] = jnp.where(pi == source(pj), 1.0, 0.0).astype(BF16)

    x = x_ref[...]
    h_in = _rms_norm(x, ng_ref[...]).astype(BF16)
    q_ref[...] =_dot(h_in, win_ref[:, :d_attn]) * (HEAD_DIM ** -0.5)
    kv = _dot(h_in, win_ref[:, d_attn:d_attn + 2 * d_kv])
    kn_ref[...] = kv[:, :d_kv]
    vn_ref[...] = kv[:, d_kv:]
    g_a = _dot(h_in, win_ref[:, d_attn + 2 * d_kv:d_attn + 2 * d_kv + d_attn])

    h_tm = _dot(perm_ref[...], h_in).astype(BF16)
    u_off = d_attn + 2 * d_kv + d_attn
    u_ab = _dot(h_tm, win_ref[:, u_off:u_off + 2 * d_conv])
    u_tm = u_ab[:, :d_conv] * jax.nn.sigmoid(u_ab[:, d_conv:])
    g_c_tm = _dot(h_tm, win_ref[:, u_off + 2 * d_conv:])

    lo = lax.broadcasted_iota(jnp.int32, (dec, LANES), 1) < HALF
    zero = jnp.zeros((dec, LANES), F32)
    sink = sinkcol_ref[...]
    bias = bias_ref[...]

    def scores(b):
        r = pl.multiple_of(b * dec, dec)
        k_all = jnp.concatenate([ck_ref[b].T, kn_ref[pl.ds(r, dec), :]], axis=0)
        ko_ref[b] = k_all[dec:, :]
        q_b = q_ref[pl.ds(r, dec), :]
        rows = []
        for h in range(N_HEADS):
            tile_q = q_b[:, (h // 2) * LANES:(h // 2 + 1) * LANES]
            in_lo = h % 2 == 0
            want_lo = h // GQA_GROUP == 0
            src = tile_q if in_lo == want_lo else pltpu.roll(tile_q, HALF, axis=1)
            rows.append(jnp.where(lo, src, zero) if want_lo else jnp.where(lo, zero, src))
        q_rows = jnp.concatenate(rows, axis=0).astype(BF16)
        return _dot_nt(q_rows, k_all.astype(BF16)) + bias

    def attend(b, s):
        r = pl.multiple_of(b * dec, dec)
        v_all = jnp.concatenate([cv_ref[b].T, vn_ref[pl.ds(r, dec), :]], axis=0)
        vo_ref[b] = v_all[dec:, :]
        p, rden = _softmax_sink(s, sink)
        o = _dot(p.astype(BF16), v_all.astype(BF16)) * rden
        tiles = []
        for pair in range(N_PAIRS):
            a = o[(2 * pair) * dec:(2 * pair + 1) * dec, :]
            c = o[(2 * pair + 1) * dec:(2 * pair + 2) * dec, :]
            if pair // PAIRS_PER_KV == 0:
                tiles.append(jnp.where(lo, a, pltpu.roll(c, HALF, axis=1)))
            else:
                tiles.append(jnp.where(lo, pltpu.roll(a, HALF, axis=1), c))
        attn_ref[pl.ds(r, dec), :] = jnp.concatenate(tiles, axis=1)

    upt_ref[:CONV_TAIL] = cs_ref[...]
    for tt in range(dec):
        upt_ref[CONV_TAIL + tt] = u_tm[tt * group:(tt + 1) * group, :]
    co_ref[...] = upt_ref[dec:]

    def conv_step(tt):
        for g0 in range(0, group, SUBLANES):
            acc = jnp.zeros((SUBLANES, d_conv), F32)
            for j in range(CONV_W):
                acc = acc + upt_ref[tt + j, g0:g0 + SUBLANES, :] * wb_ref[j]
            start = tt * group + g0
            if not isinstance(start, int):
                start = pl.multiple_of(start, SUBLANES)
            conv_ref[pl.ds(start, SUBLANES), :] = acc

    trips = group // SAMPLE_UNROLL
    conv_per_trip = dec // trips

    def attention_batch(i, carry):
        elems = [i * SAMPLE_UNROLL + e for e in range(SAMPLE_UNROLL)]
        ss = [scores(b) for b in elems]
        for b, s in zip(elems, ss):
            attend(b, s)
        for c in range(conv_per_trip):
            conv_step(i * conv_per_trip + c)
        return carry

    lax.fori_loop(0, trips, attention_batch, 0)
    for tt in range(trips * conv_per_trip, dec):
        conv_step(tt)

    c2_tm = _conv_branch_tail(conv_ref[...] + dwb_ref[...], lng_ref[...], lnb_ref[...], wpw_ref)
    m_c = _dot(perm_t_ref[...], _gate(c2_tm, g_c_tm)).astype(BF16)
    y_ref[...] = _mix_out(x, _gate(attn_ref[...], g_a), m_c, wout_ref, fg_ref[...], d_attn, final_norm)


def _sample_layer(x, ck, cv, cs, sinks, ng, win, dww, dwb, lng, lnb, wpw, wout, fg, final_norm):
    Bs, dec, D = x.shape
    group = SAMPLE_GROUP
    assert Bs % group == 0 and dec % 8 == 0 and dec <= CONV_TAIL
    d_conv = dww.shape[1]
    d_kv = N_KV_HEADS * HEAD_DIM
    d_attn = N_HEADS * HEAD_DIM
    rows = group * dec
    x2 = x.reshape(Bs * dec, D)
    assert group & (group - 1) == 0
    const = lambda shape: pl.BlockSpec(shape, lambda g: (0,) * len(shape))
    per_group = lambda shape: pl.BlockSpec((group,) + shape, lambda g: (g,) + (0,) * len(shape))
    conv_state = pl.BlockSpec((CONV_TAIL, group, d_conv), lambda g: (0, g, 0))
    y, ko, vo, co = pl.pallas_call(
        functools.partial(_sample_kernel, group=group, dec=dec, final_norm=final_norm),
        grid=(Bs // group,),
        in_specs=[
            pl.BlockSpec((rows, D), lambda g: (g, 0)),
            per_group((d_kv, WINDOW)), per_group((d_kv, WINDOW)), conv_state,
            pl.BlockSpec(memory_space=pltpu.SMEM),
            const(ng.shape), const(win.shape), const(dww.shape), const(dwb.shape),
            const(lng.shape), const(lnb.shape), const(wpw.shape), const(wout.shape), const(fg.shape),
        ],
        out_specs=[
            pl.BlockSpec((rows, D), lambda g: (g, 0)),
            per_group((WINDOW, d_kv)), per_group((WINDOW, d_kv)), conv_state,
        ],
        out_shape=[
            jax.ShapeDtypeStruct((Bs * dec, D), F32),
            jax.ShapeDtypeStruct((Bs, WINDOW, d_kv), F32),
            jax.ShapeDtypeStruct((Bs, WINDOW, d_kv), F32),
            jax.ShapeDtypeStruct((CONV_TAIL, Bs, d_conv), F32),
        ],
        scratch_shapes=[
            pltpu.VMEM((rows, d_attn), F32),
            pltpu.VMEM((rows, d_kv), F32),
            pltpu.VMEM((rows, d_kv), F32),
            pltpu.VMEM((CONV_TAIL + dec, group, d_conv), F32),
            pltpu.VMEM((CONV_W, SUBLANES, d_conv), F32),
            pltpu.VMEM((N_HEADS * dec, WINDOW + dec), F32),
            pltpu.VMEM((N_HEADS * dec, 1), F32),
            pltpu.VMEM((rows, d_attn), F32),
            pltpu.VMEM((rows, d_conv), F32),
            pltpu.VMEM((rows, rows), BF16),
            pltpu.VMEM((rows, rows), BF16),
        ],
        compiler_params=pltpu.CompilerParams(
            dimension_semantics=("arbitrary",), vmem_limit_bytes=VMEM_LIMIT_BYTES),
        name="sample_layer",
    )(x2, ck, cv, cs, sinks, ng, win, dww, dwb, lng, lnb, wpw, wout, fg)
    return y.reshape(Bs, dec, D), ko, vo, co


def kernel(x_prompt, x_sample, cache_k, cache_v, state_conv, norm_g, w_in, attn_sinks, dw_w, dw_b,
           conv_ln_g, conv_ln_b, w_pw2, w_out, final_norm_g):
    depth = w_in.shape[0]
    B = x_prompt.shape[0]
    Bs, dec = x_sample.shape[0], x_sample.shape[1]
    d_kv = N_KV_HEADS * HEAD_DIM
    fg = final_norm_g.reshape(1, -1)
    hp, hs = x_prompt, x_sample
    pk, pv, pc, sk, sv, sc = [], [], [], [], [], []
    kv_major = lambda c: c.transpose(0, 2, 3, 1).reshape(Bs, d_kv, WINDOW)
    for l in range(depth):
        final_norm = l == depth - 1
        row = lambda a: a[l].reshape(1, -1)
        win, wpw, wout = _cast_weights(w_in[l], w_pw2[l], w_out[l])
        shared = (row(norm_g), win, dw_w[l], row(dw_b), row(conv_ln_g), row(conv_ln_b), wpw, wout, fg)
        hp, k_p, v_p, c_p = _prompt_layer(hp, attn_sinks[l], *shared, final_norm)
        hs, k_s, v_s, c_s = _sample_layer(
            hs, kv_major(cache_k[l]), kv_major(cache_v[l]),
            state_conv[l].swapaxes(0, 1), attn_sinks[l], *shared, final_norm)
        key_major = lambda s: s.reshape(B, N_KV_HEADS, HEAD_DIM, WINDOW).transpose(0, 3, 1, 2)
        pk.append(key_major(k_p))
        pv.append(key_major(v_p))
        pc.append(c_p)
        sk.append(k_s.reshape(Bs, WINDOW, N_KV_HEADS, HEAD_DIM))
        sv.append(v_s.reshape(Bs, WINDOW, N_KV_HEADS, HEAD_DIM))
        sc.append(c_s.swapaxes(0, 1))
    stack = lambda xs: xs[0][None] if len(xs) == 1 else jnp.stack(xs)
    return (hp, hs, stack(pk), stack(pv), stack(pc), stack(sk), stack(sv), stack(sc))
```

```python
import functools

import jax
import jax.numpy as jnp
from jax import lax
from jax.experimental import pallas as pl
from jax.experimental.pallas import tpu as pltpu

HEAD_DIM = 64
N_HEADS = 8
N_KV_HEADS = 2
GQA_GROUP = N_HEADS // N_KV_HEADS
N_PAIRS = N_HEADS // 2
PAIRS_PER_KV = N_PAIRS // N_KV_HEADS
WINDOW = 128
CONV_W = 31
CONV_TAIL = CONV_W - 1
EPS = 1e-5
LANES = 128
SUBLANES = 8
HALF = LANES // 2
NEG_INF = float("-inf")
LOG2E = 1.4426950408889634

PROMPT_TILE = 512
CONV_ROWS = 64
CONV_CH_BLOCKS = 2
CONV_PAD = 32
CAST_STEPS = 8
SAMPLE_GROUP = 32
SAMPLE_UNROLL = 4
VMEM_LIMIT_BYTES = 56 * 1024 * 1024

F32 = jnp.float32
BF16 = jnp.bfloat16


def _alibi_slope(h):
    return 2.0 ** (-8.0 * (h + 1) / N_HEADS)


def _rms_norm(x, g):
    ms = jnp.mean(x * x, axis=-1, keepdims=True)
    return x * lax.rsqrt(ms + EPS) * g


def _silu(x):
    return x * jax.nn.sigmoid(x)


def _dot(a, b):
    return jnp.dot(a, b, preferred_element_type=F32)


def _dot_nt(a, b):
    return lax.dot_general(a, b, (((1,), (1,)), ((), ())), preferred_element_type=F32)


def _conv_branch_tail(c, lng, lnb, wpw_ref):
    mu = jnp.mean(c, axis=-1, keepdims=True)
    xc = c - mu
    var = jnp.mean(xc * xc, axis=-1, keepdims=True)
    y = xc * lax.rsqrt(var + EPS) * lng + lnb
    return _dot(_silu(y).astype(BF16), wpw_ref[...])


def _gate(val, g):
    return (val * _silu(g)).astype(BF16)


def _mix_out(x, m_a, m_c, wout_ref, fg, d_attn, final_norm):
    y = x + _dot(m_a, wout_ref[:d_attn, :]) + _dot(m_c, wout_ref[d_attn:, :])
    return _rms_norm(y, fg) if final_norm else y


def _softmax_sink(s, sink):
    m = jnp.maximum(jnp.max(s, axis=-1, keepdims=True), sink)
    p = jnp.exp(s - m)
    den = jnp.sum(p, axis=-1, keepdims=True) + jnp.exp(sink - m)
    return p, 1.0 / den


def _cast_kernel(*refs):
    n = len(refs) // 2
    for src, dst in zip(refs[:n], refs[n:]):
        dst[...] = src[...].astype(BF16)


def _cast_weights(*ws):
    steps = CAST_STEPS
    assert all(w.shape[0] % (steps * 2 * SUBLANES) == 0 for w in ws)
    spec = lambda w: pl.BlockSpec((w.shape[0] // steps, w.shape[1]), lambda i: (i, 0))
    return pl.pallas_call(
        _cast_kernel,
        grid=(steps,),
        in_specs=[spec(w) for w in ws],
        out_specs=[spec(w) for w in ws],
        out_shape=[jax.ShapeDtypeStruct(w.shape, BF16) for w in ws],
        name="cast_weights",
    )(*ws)


def _prompt_kernel(x_ref, sink_ref, ng_ref, win_ref, dww_ref, dwb_ref, lng_ref, lnb_ref,
                   wpw_ref, wout_ref, fg_ref,
                   y_ref, kst_ref, vst_ref, cst_ref,
                   k_ref, vt_ref, up_ref, shifted_ref, wb_ref, bias_ref, attn_ref, conv_ref,
                   *, tile, final_norm):
    t = pl.program_id(1)
    d_attn = N_HEADS * HEAD_DIM
    d_kv = N_KV_HEADS * HEAD_DIM
    d_conv = up_ref.shape[1]
    n_blocks = tile // WINDOW
    two_w = 2 * WINDOW
    u_off = d_attn + 2 * d_kv + d_attn
    n_shift = shifted_ref.shape[1]

    @pl.when(t == 0)
    def _init():
        k_ref[:WINDOW, :] = jnp.zeros((WINDOW, d_kv), BF16)
        vt_ref[:, :WINDOW] = jnp.zeros((d_kv, WINDOW), BF16)
        up_ref[:CONV_PAD, :] = jnp.zeros((CONV_PAD, d_conv), F32)
        for j in range(CONV_W):
            wb_ref[j] = jnp.broadcast_to(dww_ref[j:j + 1, :], (SUBLANES, d_conv))
        ji = lax.broadcasted_iota(jnp.int32, (two_w, two_w), 0)
        ci = lax.broadcasted_iota(jnp.int32, (two_w, two_w), 1)
        odd = ci >= WINDOW
        dist = jnp.where(odd, ci - WINDOW, ci) + WINDOW - ji
        valid = (dist >= 0) & (dist < WINDOW)
        valid_first = valid & (ji >= WINDOW)
        distf = dist.astype(F32) * LOG2E
        for pair in range(N_PAIRS):
            pen = -jnp.where(odd, _alibi_slope(2 * pair + 1), _alibi_slope(2 * pair)) * distf
            bias_ref[0, pair] = jnp.where(valid, pen, NEG_INF)
            bias_ref[1, pair] = jnp.where(valid_first, pen, NEG_INF)

    x = x_ref[...]
    h_in = _rms_norm(x, ng_ref[...]).astype(BF16)


    cw = d_conv // CONV_CH_BLOCKS
    for cb in range(CONV_CH_BLOCKS):
        ch = slice(cb * cw, (cb + 1) * cw)
        u_a = _dot(h_in, win_ref[:, u_off + cb * cw:u_off + (cb + 1) * cw])
        u_b = _dot(h_in, win_ref[:, u_off + d_conv + cb * cw:u_off + d_conv + (cb + 1) * cw])
        up_ref[CONV_PAD:, ch] = u_a * jax.nn.sigmoid(u_b)
        for s in range(1, SUBLANES):
            shifted_ref[s - 1, :, ch] = up_ref[s:s + n_shift, ch]
    cst_ref[...] = up_ref[CONV_PAD + tile - CONV_TAIL:, :]

    kv = _dot(h_in, win_ref[:, d_attn:d_attn + 2 * d_kv])
    k = kv[:, :d_kv]
    v = kv[:, d_kv:]
    k_ref[WINDOW:, :] = k.astype(BF16)
    v_t = v.T
    vt_ref[:, WINDOW:] = v_t.astype(BF16)
    kst_ref[...] = k[tile - WINDOW:, :].T
    vst_ref[...] = v_t[:, tile - WINDOW:]

    groups = CONV_ROWS // SUBLANES

    def conv_chunk(r, cb):
        ch = slice(cb * cw, (cb + 1) * cw)
        acc = [jnp.zeros((SUBLANES, cw), F32) for _ in range(groups)]
        loaded = {}
        for j in range(CONV_W):
            off = CONV_PAD - CONV_TAIL + j
            a, s = off // SUBLANES, off % SUBLANES
            wb = wb_ref[j, :, ch]
            for g in range(groups):
                if (s, a + g) not in loaded:
                    rows = slice(r + (a + g) * SUBLANES, r + (a + g + 1) * SUBLANES)
                    loaded[s, a + g] = up_ref[rows, ch] if s == 0 else shifted_ref[s - 1, rows, ch]
                acc[g] = acc[g] + loaded[s, a + g] * wb
        for g in range(groups):
            conv_ref[r + g * SUBLANES:r + (g + 1) * SUBLANES, ch] = acc[g]

    q = _dot(h_in, win_ref[:, :d_attn]) * (HEAD_DIM ** -0.5 * LOG2E)
    g_a = _dot(h_in, win_ref[:, d_attn + 2 * d_kv:u_off])
    g_c = _dot(h_in, win_ref[:, u_off + 2 * d_conv:])

    lo = lax.broadcasted_iota(jnp.int32, (WINDOW, LANES), 1) < HALF
    zero = jnp.zeros((WINDOW, LANES), F32)
    odd_head = lax.broadcasted_iota(jnp.int32, (1, two_w), 1) >= WINDOW

    def scores(blk, pair):
        r0 = blk * WINDOW
        first = jnp.where(t == 0, 1, 0) if blk == 0 else 0
        kvh = pair // PAIRS_PER_KV
        qp = q[r0:r0 + WINDOW, pair * LANES:(pair + 1) * LANES]
        rolled = pltpu.roll(qp, HALF, axis=1)
        if kvh == 0:
            q_even, q_odd = jnp.where(lo, qp, zero), jnp.where(lo, rolled, zero)
        else:
            q_even, q_odd = jnp.where(lo, zero, rolled), jnp.where(lo, zero, qp)
        qm = jnp.concatenate([q_even, q_odd], axis=0).astype(BF16)
        return _dot_nt(k_ref[r0:r0 + two_w, :], qm) + bias_ref[first, pair]

    def attend(blk, pair, s):
        r0 = blk * WINDOW
        kvh = pair // PAIRS_PER_KV
        sink = jnp.where(odd_head, sink_ref[2 * pair + 1], sink_ref[2 * pair]) * LOG2E
        m = jnp.maximum(jnp.max(s, axis=0, keepdims=True), sink)
        p = jnp.exp2(s - m)
        den = jnp.sum(p, axis=0, keepdims=True) + jnp.exp2(sink - m)
        vt_win = vt_ref[kvh * HEAD_DIM:(kvh + 1) * HEAD_DIM, r0:r0 + two_w]
        ot = _dot(vt_win, p.astype(BF16)) * (1.0 / den)
        o = jnp.concatenate([ot[:, :WINDOW], ot[:, WINDOW:]], axis=0).T
        attn_ref[r0:r0 + WINDOW, pair * LANES:(pair + 1) * LANES] = o

    steps = [(blk, pair) for blk in range(n_blocks) for pair in range(N_PAIRS)]
    s_next = scores(*steps[0])
    for i, step in enumerate(steps):
        s_cur = s_next
        if i + 1 < len(steps):
            s_next = scores(*steps[i + 1])
        attend(*step, s_cur)

    y_a = _dot(_gate(attn_ref[...], g_a), wout_ref[:d_attn, :])
    for cb in range(CONV_CH_BLOCKS):
        for r in range(0, tile, CONV_ROWS):
            conv_chunk(r, cb)

    c2 = _conv_branch_tail(conv_ref[...] + dwb_ref[...], lng_ref[...], lnb_ref[...], wpw_ref)
    y = x + y_a + _dot(_gate(c2, g_c), wout_ref[d_attn:, :])
    y_ref[...] = _rms_norm(y, fg_ref[...]) if final_norm else y

    k_ref[:WINDOW, :] = k_ref[tile:, :]
    vt_ref[:, :WINDOW] = vt_ref[:, tile:]
    up_ref[:CONV_PAD, :] = up_ref[tile:, :]


def _prompt_layer(x, sinks, ng, win, dww, dwb, lng, lnb, wpw, wout, fg, final_norm):
    B, T, D = x.shape
    tile = PROMPT_TILE
    assert T % tile == 0 and tile % WINDOW == 0 and tile % CONV_ROWS == 0
    d_conv = dww.shape[1]
    d_kv = N_KV_HEADS * HEAD_DIM
    d_attn = N_HEADS * HEAD_DIM
    n_t = T // tile
    const = lambda shape: pl.BlockSpec(shape, lambda b, t: (0,) * len(shape))
    return pl.pallas_call(
        functools.partial(_prompt_kernel, tile=tile, final_norm=final_norm),
        grid=(B, n_t),
        in_specs=[
            pl.BlockSpec((None, tile, D), lambda b, t: (b, t, 0)),
            pl.BlockSpec(memory_space=pltpu.SMEM),
            const(ng.shape), const(win.shape), const(dww.shape), const(dwb.shape),
            const(lng.shape), const(lnb.shape), const(wpw.shape), const(wout.shape), const(fg.shape),
        ],
        out_specs=[
            pl.BlockSpec((None, tile, D), lambda b, t: (b, t, 0)),
            pl.BlockSpec((None, d_kv, WINDOW), lambda b, t: (b, 0, 0)),
            pl.BlockSpec((None, d_kv, WINDOW), lambda b, t: (b, 0, 0)),
            pl.BlockSpec((None, CONV_TAIL, d_conv), lambda b, t: (b, 0, 0)),
        ],
        scratch_shapes=[
            pltpu.VMEM((WINDOW + tile, d_kv), BF16),
            pltpu.VMEM((d_kv, WINDOW + tile), BF16),
            pltpu.VMEM((CONV_PAD + tile, d_conv), F32),
            pltpu.VMEM((SUBLANES - 1, CONV_PAD - SUBLANES + tile, d_conv), F32),
            pltpu.VMEM((CONV_W, SUBLANES, d_conv), F32),
            pltpu.VMEM((2, N_PAIRS, 2 * WINDOW, 2 * WINDOW), F32),
            pltpu.VMEM((tile, d_attn), F32),
            pltpu.VMEM((tile, d_conv), F32),
        ],
        out_shape=[
            jax.ShapeDtypeStruct((B, T, D), F32),
            jax.ShapeDtypeStruct((B, d_kv, WINDOW), F32),
            jax.ShapeDtypeStruct((B, d_kv, WINDOW), F32),
            jax.ShapeDtypeStruct((B, CONV_TAIL, d_conv), F32),
        ],
        compiler_params=pltpu.CompilerParams(
            dimension_semantics=("arbitrary", "arbitrary"), vmem_limit_bytes=VMEM_LIMIT_BYTES),
        name="prompt_layer",
    )(x, sinks, ng, win, dww, dwb, lng, lnb, wpw, wout, fg)


def _sample_kernel(x_ref, ck_ref, cv_ref, cs_ref, sink_ref, ng_ref, win_ref,
                   dww_ref, dwb_ref, lng_ref, lnb_ref, wpw_ref, wout_ref, fg_ref,
                   y_ref, ko_ref, vo_ref, co_ref,
                   q_ref, kn_ref, vn_ref, upt_ref, wb_ref, bias_ref, sinkcol_ref, attn_ref, conv_ref,
                   perm_ref, perm_t_ref, *, group, dec, final_norm):
    d_attn = N_HEADS * HEAD_DIM
    d_kv = N_KV_HEADS * HEAD_DIM
    d_conv = conv_ref.shape[1]
    n_keys = WINDOW + dec
    rows_q = N_HEADS * dec

    @pl.when(pl.program_id(0) == 0)
    def _init():
        ri = lax.broadcasted_iota(jnp.int32, (rows_q, n_keys), 0)
        si = lax.broadcasted_iota(jnp.int32, (rows_q, n_keys), 1)
        ti = ri
        slope = jnp.zeros((rows_q, n_keys), F32)
        for h in range(N_HEADS):
            in_head = (ri >= h * dec) & (ri < (h + 1) * dec)
            ti = jnp.where(in_head, ri - h * dec, ti)
            slope = jnp.where(in_head, _alibi_slope(h), slope)
        dist = WINDOW + ti - si
        valid = (dist >= 0) & (dist < WINDOW)
        distf = dist.astype(F32)
        bias_ref[...] = jnp.where(valid, -slope * distf, NEG_INF)
        rc = lax.broadcasted_iota(jnp.int32, (rows_q, 1), 0)
        sink_rows = jnp.zeros((rows_q, 1), F32)
        for h in range(N_HEADS):
            sink_rows = jnp.where((rc >= h * dec) & (rc < (h + 1) * dec), sink_ref[h], sink_rows)
        sinkcol_ref[...] = sink_rows
        for j in range(CONV_W):
            wb_ref[j] = jnp.broadcast_to(dww_ref[j:j + 1, :], (SUBLANES, d_conv))
        n_rows = group * dec
        pi = lax.broadcasted_iota(jnp.int32, (n_rows, n_rows), 0)
        pj = lax.broadcasted_iota(jnp.int32, (n_rows, n_rows), 1)
        shift = group.bit_length() - 1
        source = lambda i: (i & (group - 1)) * dec + lax.shift_right_logical(i, shift)
        perm_ref[...] = jnp.where(pj == source(pi), 1.0, 0.0).astype(BF16)
        perm_t_ref[...] = jnp.where(pi == source(pj), 1.0, 0.0).astype(BF16)

    x = x_ref[...]
    h_in = _rms_norm(x, ng_ref[...]).astype(BF16)
    q_ref[...] = _dot(h_in, win_ref[:, :d_attn]) * (HEAD_DIM ** -0.5)
    kv = _dot(h_in, win_ref[:, d_attn:d_attn + 2 * d_kv])
    kn_ref[...] = kv[:, :d_kv]
    vn_ref[...] = kv[:, d_kv:]
    g_a = _dot(h_in, win_ref[:, d_attn + 2 * d_kv:d_attn + 2 * d_kv + d_attn])

    h_tm = _dot(perm_ref[...], h_in).astype(BF16)
    u_off = d_attn + 2 * d_kv + d_attn
    u_ab = _dot(h_tm, win_ref[:, u_off:u_off + 2 * d_conv])
    u_tm = u_ab[:, :d_conv] * jax.nn.sigmoid(u_ab[:, d_conv:])
    g_c_tm = _dot(h_tm, win_ref[:, u_off + 2 * d_conv:])

    lo = lax.broadcasted_iota(jnp.int32, (dec, LANES), 1) < HALF
    zero = jnp.zeros((dec, LANES), F32)
    sink = sinkcol_ref[...]
    bias = bias_ref[...]

    def scores(b):
        r = pl.multiple_of(b * dec, dec)
        k_all = jnp.concatenate([ck_ref[b].T, kn_ref[pl.ds(r, dec), :]], axis=0)
        ko_ref[b] = k_all[dec:, :]
        q_b = q_ref[pl.ds(r, dec), :]
        rows = []
        for h in range(N_HEADS):
            tile_q = q_b[:, (h // 2) * LANES:(h // 2 + 1) * LANES]
            in_lo = h % 2 == 0
            want_lo = h // GQA_GROUP == 0
            src = tile_q if in_lo == want_lo else pltpu.roll(tile_q, HALF, axis=1)
            rows.append(jnp.where(lo, src, zero) if want_lo else jnp.where(lo, zero, src))
        q_rows = jnp.concatenate(rows, axis=0).astype(BF16)
        return _dot_nt(q_rows, k_all.astype(BF16)) + bias

    def attend(b, s):
        r = pl.multiple_of(b * dec, dec)
        v_all = jnp.concatenate([cv_ref[b].T, vn_ref[pl.ds(r, dec), :]], axis=0)
        vo_ref[b] = v_all[dec:, :]
        p, rden = _softmax_sink(s, sink)
        o = _dot(p.astype(BF16), v_all.astype(BF16)) * rden
        tiles = []
        for pair in range(N_PAIRS):
            a = o[(2 * pair) * dec:(2 * pair + 1) * dec, :]
            c = o[(2 * pair + 1) * dec:(2 * pair + 2) * dec, :]
            if pair // PAIRS_PER_KV == 0:
                tiles.append(jnp.where(lo, a, pltpu.roll(c, HALF, axis=1)))
            else:
                tiles.append(jnp.where(lo, pltpu.roll(a, HALF, axis=1), c))
        attn_ref[pl.ds(r, dec), :] = jnp.concatenate(tiles, axis=1)

    upt_ref[:CONV_TAIL] = cs_ref[...]
    for tt in range(dec):
        upt_ref[CONV_TAIL + tt] = u_tm[tt * group:(tt + 1) * group, :]
    co_ref[...] = upt_ref[dec:]

    def conv_step(tt):
        for g0 in range(0, group, SUBLANES):
            acc = jnp.zeros((SUBLANES, d_conv), F32)
            for j in range(CONV_W):
                acc = acc + upt_ref[tt + j, g0:g0 + SUBLANES, :] * wb_ref[j]
            start = tt * group + g0
            if not isinstance(start, int):
                start = pl.multiple_of(start, SUBLANES)
            conv_ref[pl.ds(start, SUBLANES), :] = acc

    trips = group // SAMPLE_UNROLL
    conv_per_trip = dec // trips

    def attention_batch(i, carry):
        elems = [i * SAMPLE_UNROLL + e for e in range(SAMPLE_UNROLL)]
        ss = [scores(b) for b in elems]
        for b, s in zip(elems, ss):
            attend(b, s)
        for c in range(conv_per_trip):
            conv_step(i * conv_per_trip + c)
        return carry

    lax.fori_loop(0, trips, attention_batch, 0)
    for tt in range(trips * conv_per_trip, dec):
        conv_step(tt)

    c2_tm = _conv_branch_tail(conv_ref[...] + dwb_ref[...], lng_ref[...], lnb_ref[...], wpw_ref)
    m_c = _dot(perm_t_ref[...], _gate(c2_tm, g_c_tm)).astype(BF16)
    y_ref[...] = _mix_out(x, _gate(attn_ref[...], g_a), m_c, wout_ref, fg_ref[...], d_attn, final_norm)


def _sample_layer(x, ck, cv, cs, sinks, ng, win, dww, dwb, lng, lnb, wpw, wout, fg, final_norm):
    Bs, dec, D = x.shape
    group = SAMPLE_GROUP
    assert Bs % group == 0 and dec % 8 == 0 and dec <= CONV_TAIL
    d_conv = dww.shape[1]
    d_kv = N_KV_HEADS * HEAD_DIM
    d_attn = N_HEADS * HEAD_DIM
    rows = group * dec
    x2 = x.reshape(Bs * dec, D)
    assert group & (group - 1) == 0
    const = lambda shape: pl.BlockSpec(shape, lambda g: (0,) * len(shape))
    per_group = lambda shape: pl.BlockSpec((group,) + shape, lambda g: (g,) + (0,) * len(shape))
    conv_state = pl.BlockSpec((CONV_TAIL, group, d_conv), lambda g: (0, g, 0))
    y, ko, vo, co = pl.pallas_call(
        functools.partial(_sample_kernel, group=group, dec=dec, final_norm=final_norm),
        grid=(Bs // group,),
        in_specs=[
            pl.BlockSpec((rows, D), lambda g: (g, 0)),
            per_group((d_kv, WINDOW)), per_group((d_kv, WINDOW)), conv_state,
            pl.BlockSpec(memory_space=pltpu.SMEM),
            const(ng.shape), const(win.shape), const(dww.shape), const(dwb.shape),
            const(lng.shape), const(lnb.shape), const(wpw.shape), const(wout.shape), const(fg.shape),
        ],
        out_specs=[
            pl.BlockSpec((rows, D), lambda g: (g, 0)),
            per_group((WINDOW, d_kv)), per_group((WINDOW, d_kv)), conv_state,
        ],
        out_shape=[
            jax.ShapeDtypeStruct((Bs * dec, D), F32),
            jax.ShapeDtypeStruct((Bs, WINDOW, d_kv), F32),
            jax.ShapeDtypeStruct((Bs, WINDOW, d_kv), F32),
            jax.ShapeDtypeStruct((CONV_TAIL, Bs, d_conv), F32),
        ],
        scratch_shapes=[
            pltpu.VMEM((rows, d_attn), F32),
            pltpu.VMEM((rows, d_kv), F32),
            pltpu.VMEM((rows, d_kv), F32),
            pltpu.VMEM((CONV_TAIL + dec, group, d_conv), F32),
            pltpu.VMEM((CONV_W, SUBLANES, d_conv), F32),
            pltpu.VMEM((N_HEADS * dec, WINDOW + dec), F32),
            pltpu.VMEM((N_HEADS * dec, 1), F32),
            pltpu.VMEM((rows, d_attn), F32),
            pltpu.VMEM((rows, d_conv), F32),
            pltpu.VMEM((rows, rows), BF16),
            pltpu.VMEM((rows, rows), BF16),
        ],
        compiler_params=pltpu.CompilerParams(
            dimension_semantics=("arbitrary",), vmem_limit_bytes=VMEM_LIMIT_BYTES),
        name="sample_layer",
    )(x2, ck, cv, cs, sinks, ng, win, dww, dwb, lng, lnb, wpw, wout, fg)
    return y.reshape(Bs, dec, D), ko, vo, co


def kernel(x_prompt, x_sample, cache_k, cache_v, state_conv, norm_g, w_in, attn_sinks, dw_w, dw_b,
           conv_ln_g, conv_ln_b, w_pw2, w_out, final_norm_g):
    depth = w_in.shape[0]
    B = x_prompt.shape[0]
    Bs, dec = x_sample.shape[0], x_sample.shape[1]
    d_kv = N_KV_HEADS * HEAD_DIM
    fg = final_norm_g.reshape(1, -1)
    hp, hs = x_prompt, x_sample
    pk, pv, pc, sk, sv, sc = [], [], [], [], [], []
    kv_major = lambda c: c.transpose(0, 2, 3, 1).reshape(Bs, d_kv, WINDOW)
    for l in range(depth):
        final_norm = l == depth - 1
        row = lambda a: a[l].reshape(1, -1)
        win, wpw, wout = _cast_weights(w_in[l], w_pw2[l], w_out[l])
        shared = (row(norm_g), win, dw_w[l], row(dw_b), row(conv_ln_g), row(conv_ln_b), wpw, wout, fg)
        hp, k_p, v_p, c_p = _prompt_layer(hp, attn_sinks[l], *shared, final_norm)
        hs, k_s, v_s, c_s = _sample_layer(
            hs, kv_major(cache_k[l]), kv_major(cache_v[l]),
            state_conv[l].swapaxes(0, 1), attn_sinks[l], *shared, final_norm)
        key_major = lambda s: s.reshape(B, N_KV_HEADS, HEAD_DIM, WINDOW).transpose(0, 3, 1, 2)
        pk.append(key_major(k_p))
        pv.append(key_major(v_p))
        pc.append(c_p)
        sk.append(k_s.reshape(Bs, WINDOW, N_KV_HEADS, HEAD_DIM))
        sv.append(v_s.reshape(Bs, WINDOW, N_KV_HEADS, HEAD_DIM))
        sc.append(c_s.swapaxes(0, 1))
    stack = lambda xs: xs[0][None] if len(xs) == 1 else jnp.stack(xs)
    return (hp, hs, stack(pk), stack(pv), stack(pc), stack(sk), stack(sv), stack(sc))
```

```python
import functools

import jax
import jax.numpy as jnp
from jax import lax
from jax.experimental import pallas as pl
from jax.experimental.pallas import tpu as pltpu

HEAD_DIM = 64
N_HEADS = 8
N_KV_HEADS = 2
GQA_GROUP = N_HEADS // N_KV_HEADS
N_PAIRS = N_HEADS // 2
PAIRS_PER_KV = N_PAIRS // N_KV_HEADS
WINDOW = 128
CONV_W = 31
CONV_TAIL = CONV_W - 1
EPS = 1e-5
LANES = 128
SUBLANES = 8
HALF = LANES // 2
NEG_INF = float("-inf")
LOG2E = 1.4426950408889634

PROMPT_TILE = 1024
CONV_ROWS = 64
CONV_CH_BLOCKS = 2
CONV_PAD = 32
CAST_STEPS = 8
SAMPLE_GROUP = 32
SAMPLE_UNROLL = 4
VMEM_LIMIT_BYTES = 56 * 1024 * 1024

F32 = jnp.float32
BF16 = jnp.bfloat16


def _alibi_slope(h):
    return 2.0 ** (-8.0 * (h + 1) / N_HEADS)


def _rms_norm(x, g):
    ms = jnp.mean(x * x, axis=-1, keepdims=True)
    return x * lax.rsqrt(ms + EPS) * g


def _silu(x):
    return x * jax.nn.sigmoid(x)


def _dot(a, b):
    return jnp.dot(a, b, preferred_element_type=F32)


def _dot_nt(a, b):
    return lax.dot_general(a, b, (((1,), (1,)), ((), ())), preferred_element_type=F32)


def _conv_branch_tail(c, lng, lnb, wpw_ref):
    mu = jnp.mean(c, axis=-1, keepdims=True)
    xc = c - mu
    var = jnp.mean(xc * xc, axis=-1, keepdims=True)
    y = xc * lax.rsqrt(var + EPS) * lng + lnb
    return _dot(_silu(y).astype(BF16), wpw_ref[...])


def _gate(val, g):
    return (val * _silu(g)).astype(BF16)


def _mix_out(x, m_a, m_c, wout_ref, fg, d_attn, final_norm):
    y = x + _dot(m_a, wout_ref[:d_attn, :]) + _dot(m_c, wout_ref[d_attn:, :])
    return _rms_norm(y, fg) if final_norm else y


def _softmax_sink(s, sink):
    m = jnp.maximum(jnp.max(s, axis=-1, keepdims=True), sink)
    p = jnp.exp(s - m)
    den = jnp.sum(p, axis=-1, keepdims=True) + jnp.exp(sink - m)
    return p, 1.0 / den


def _cast_kernel(*refs):
    n = len(refs) // 2
    for src, dst in zip(refs[:n], refs[n:]):
        dst[...] = src[...].astype(BF16)


def _cast_weights(*ws):
    steps = CAST_STEPS
    assert all(w.shape[0] % (steps * 2 * SUBLANES) == 0 for w in ws)
    spec = lambda w: pl.BlockSpec((w.shape[0] // steps, w.shape[1]), lambda i: (i, 0))
    return pl.pallas_call(
        _cast_kernel,
        grid=(steps,),
        in_specs=[spec(w) for w in ws],
        out_specs=[spec(w) for w in ws],
        out_shape=[jax.ShapeDtypeStruct(w.shape, BF16) for w in ws],
        name="cast_weights",
    )(*ws)


def _prompt_kernel(x_ref, sink_ref, ng_ref, win_ref, dww_ref, dwb_ref, lng_ref, lnb_ref,
                   wpw_ref, wout_ref, fg_ref,
                   y_ref, kst_ref, vst_ref, cst_ref,
                   k_ref, vt_ref, up_ref, shifted_ref, wb_ref, bias_ref, attn_ref, conv_ref,
                   *, tile, final_norm):
    t = pl.program_id(1)
    d_attn = N_HEADS * HEAD_DIM
    d_kv = N_KV_HEADS * HEAD_DIM
    d_conv = up_ref.shape[1]
    n_blocks = tile // WINDOW
    two_w = 2 * WINDOW
    u_off = d_attn + 2 * d_kv + d_attn
    n_shift = shifted_ref.shape[1]

    @pl.when(t == 0)
    def _init():
        k_ref[:WINDOW, :] = jnp.zeros((WINDOW, d_kv), BF16)
        vt_ref[:, :WINDOW] = jnp.zeros((d_kv, WINDOW), BF16)
        up_ref[:CONV_PAD, :] = jnp.zeros((CONV_PAD, d_conv), F32)
        for j in range(CONV_W):
            wb_ref[j] = jnp.broadcast_to(dww_ref[j:j + 1, :], (SUBLANES, d_conv))
        ji = lax.broadcasted_iota(jnp.int32, (two_w, two_w), 0)
        ci = lax.broadcasted_iota(jnp.int32, (two_w, two_w), 1)
        odd = ci >= WINDOW
        dist = jnp.where(odd, ci - WINDOW, ci) + WINDOW - ji
        valid = (dist >= 0) & (dist < WINDOW)
        valid_first = valid & (ji >= WINDOW)
        distf = dist.astype(F32) * LOG2E
        for pair in range(N_PAIRS):
            pen = -jnp.where(odd, _alibi_slope(2 * pair + 1), _alibi_slope(2 * pair)) * distf
            bias_ref[0, pair] = jnp.where(valid, pen, NEG_INF)
            bias_ref[1, pair] = jnp.where(valid_first, pen, NEG_INF)

    x = x_ref[...]
    h_in = _rms_norm(x, ng_ref[...]).astype(BF16)


    cw = d_conv // CONV_CH_BLOCKS
    for cb in range(CONV_CH_BLOCKS):
        ch = slice(cb * cw, (cb + 1) * cw)
        u_a = _dot(h_in, win_ref[:, u_off + cb * cw:u_off + (cb + 1) * cw])
        u_b = _dot(h_in, win_ref[:, u_off + d_conv + cb * cw:u_off + d_conv + (cb + 1) * cw])
        up_ref[CONV_PAD:, ch] = u_a * jax.nn.sigmoid(u_b)
        for s in range(1, SUBLANES):
            shifted_ref[s - 1, :, ch] = up_ref[s:s + n_shift, ch]
    cst_ref[...] = up_ref[CONV_PAD + tile - CONV_TAIL:, :]

    kv = _dot(h_in, win_ref[:, d_attn:d_attn + 2 * d_kv])
    k = kv[:, :d_kv]
    v = kv[:, d_kv:]
    k_ref[WINDOW:, :] = k.astype(BF16)
    v_t = v.T
    vt_ref[:, WINDOW:] = v_t.astype(BF16)
    kst_ref[...] = k[tile - WINDOW:, :].T
    vst_ref[...] = v_t[:, tile - WINDOW:]

    groups = CONV_ROWS // SUBLANES

    def conv_chunk(r, cb):
        ch = slice(cb * cw, (cb + 1) * cw)
        acc = [jnp.zeros((SUBLANES, cw), F32) for _ in range(groups)]
        loaded = {}
        for j in range(CONV_W):
            off = CONV_PAD - CONV_TAIL + j
            a, s = off // SUBLANES, off % SUBLANES
            wb = wb_ref[j, :, ch]
            for g in range(groups):
                if (s, a + g) not in loaded:
                    rows = slice(r + (a + g) * SUBLANES, r + (a + g + 1) * SUBLANES)
                    loaded[s, a + g] = up_ref[rows, ch] if s == 0 else shifted_ref[s - 1, rows, ch]
                acc[g] = acc[g] + loaded[s, a + g] * wb
        for g in range(groups):
            conv_ref[r + g * SUBLANES:r + (g + 1) * SUBLANES, ch] = acc[g]

    q = _dot(h_in, win_ref[:, :d_attn]) * (HEAD_DIM ** -0.5 * LOG2E)
    g_a = _dot(h_in, win_ref[:, d_attn + 2 * d_kv:u_off])
    g_c = _dot(h_in, win_ref[:, u_off + 2 * d_conv:])

    lo = lax.broadcasted_iota(jnp.int32, (WINDOW, LANES), 1) < HALF
    zero = jnp.zeros((WINDOW, LANES), F32)
    odd_head = lax.broadcasted_iota(jnp.int32, (1, two_w), 1) >= WINDOW

    def scores(blk, pair):
        r0 = blk * WINDOW
        first = jnp.where(t == 0, 1, 0) if blk == 0 else 0
        kvh = pair // PAIRS_PER_KV
        qp = q[r0:r0 + WINDOW, pair * LANES:(pair + 1) * LANES]
        rolled = pltpu.roll(qp, HALF, axis=1)
        if kvh == 0:
            q_even, q_odd = jnp.where(lo, qp, zero), jnp.where(lo, rolled, zero)
        else:
            q_even, q_odd = jnp.where(lo, zero, rolled), jnp.where(lo, zero, qp)
        qm = jnp.concatenate([q_even, q_odd], axis=0).astype(BF16)
        return _dot_nt(k_ref[r0:r0 + two_w, :], qm) + bias_ref[first, pair]

    def attend(blk, pair, s):
        r0 = blk * WINDOW
        kvh = pair // PAIRS_PER_KV
        sink = jnp.where(odd_head, sink_ref[2 * pair + 1], sink_ref[2 * pair]) * LOG2E
        m = jnp.maximum(jnp.max(s, axis=0, keepdims=True), sink)
        p = jnp.exp2(s - m)
        den = jnp.sum(p, axis=0, keepdims=True) + jnp.exp2(sink - m)
        vt_win = vt_ref[kvh * HEAD_DIM:(kvh + 1) * HEAD_DIM, r0:r0 + two_w]
        ot = _dot(vt_win, p.astype(BF16)) * (1.0 / den)
        o = jnp.concatenate([ot[:, :WINDOW], ot[:, WINDOW:]], axis=0).T
        attn_ref[r0:r0 + WINDOW, pair * LANES:(pair + 1) * LANES] = o

    steps = [(blk, pair) for blk in range(n_blocks) for pair in range(N_PAIRS)]
    s_next = scores(*steps[0])
    for i, step in enumerate(steps):
        s_cur = s_next
        if i + 1 < len(steps):
            s_next = scores(*steps[i + 1])
        attend(*step, s_cur)

    y_a = _dot(_gate(attn_ref[...], g_a), wout_ref[:d_attn, :])
    for cb in range(CONV_CH_BLOCKS):
        for r in range(0, tile, CONV_ROWS):
            conv_chunk(r, cb)

    c2 = _conv_branch_tail(conv_ref[...] + dwb_ref[...], lng_ref[...], lnb_ref[...], wpw_ref)
    y = x + y_a + _dot(_gate(c2, g_c), wout_ref[d_attn:, :])
    y_ref[...] = _rms_norm(y, fg_ref[...]) if final_norm else y

    k_ref[:WINDOW, :] = k_ref[tile:, :]
    vt_ref[:, :WINDOW] = vt_ref[:, tile:]
    up_ref[:CONV_PAD, :] = up_ref[tile:, :]


def _prompt_layer(x, sinks, ng, win, dww, dwb, lng, lnb, wpw, wout, fg, final_norm):
    B, T, D = x.shape
    tile = PROMPT_TILE
    assert T % tile == 0 and tile % WINDOW == 0 and tile % CONV_ROWS == 0
    d_conv = dww.shape[1]
    d_kv = N_KV_HEADS * HEAD_DIM
    d_attn = N_HEADS * HEAD_DIM
    n_t = T // tile
    const = lambda shape: pl.BlockSpec(shape, lambda b, t: (0,) * len(shape))
    return pl.pallas_call(
        functools.partial(_prompt_kernel, tile=tile, final_norm=final_norm),
        grid=(B, n_t),
        in_specs=[
            pl.BlockSpec((None, tile, D), lambda b, t: (b, t, 0)),
            pl.BlockSpec(memory_space=pltpu.SMEM),
            const(ng.shape), const(win.shape), const(dww.shape), const(dwb.shape),
            const(lng.shape), const(lnb.shape), const(wpw.shape), const(wout.shape), const(fg.shape),
        ],
        out_specs=[
            pl.BlockSpec((None, tile, D), lambda b, t: (b, t, 0)),
            pl.BlockSpec((None, d_kv, WINDOW), lambda b, t: (b, 0, 0)),
            pl.BlockSpec((None, d_kv, WINDOW), lambda b, t: (b, 0, 0)),
            pl.BlockSpec((None, CONV_TAIL, d_conv), lambda b, t: (b, 0, 0)),
        ],
        scratch_shapes=[
            pltpu.VMEM((WINDOW + tile, d_kv), BF16),
            pltpu.VMEM((d_kv, WINDOW + tile), BF16),
            pltpu.VMEM((CONV_PAD + tile, d_conv), F32),
            pltpu.VMEM((SUBLANES - 1, CONV_PAD - SUBLANES + tile, d_conv), F32),
            pltpu.VMEM((CONV_W, SUBLANES, d_conv), F32),
            pltpu.VMEM((2, N_PAIRS, 2 * WINDOW, 2 * WINDOW), F32),
            pltpu.VMEM((tile, d_attn), F32),
            pltpu.VMEM((tile, d_conv), F32),
        ],
        out_shape=[
            jax.ShapeDtypeStruct((B, T, D), F32),
            jax.ShapeDtypeStruct((B, d_kv, WINDOW), F32),
            jax.ShapeDtypeStruct((B, d_kv, WINDOW), F32),
            jax.ShapeDtypeStruct((B, CONV_TAIL, d_conv), F32),
        ],
        compiler_params=pltpu.CompilerParams(
            dimension_semantics=("arbitrary", "arbitrary"), vmem_limit_bytes=VMEM_LIMIT_BYTES),
        name="prompt_layer",
    )(x, sinks, ng, win, dww, dwb, lng, lnb, wpw, wout, fg)


def _sample_kernel(x_ref, ck_ref, cv_ref, cs_ref, sink_ref, ng_ref, win_ref,
                   dww_ref, dwb_ref, lng_ref, lnb_ref, wpw_ref, wout_ref, fg_ref,
                   y_ref, ko_ref, vo_ref, co_ref,
                   q_ref, kn_ref, vn_ref, upt_ref, wb_ref, bias_ref, sinkcol_ref, attn_ref, conv_ref,
                   perm_ref, perm_t_ref, *, group, dec, final_norm):
    d_attn = N_HEADS * HEAD_DIM
    d_kv = N_KV_HEADS * HEAD_DIM
    d_conv = conv_ref.shape[1]
    n_keys = WINDOW + dec
    rows_q = N_HEADS * dec

    @pl.when(pl.program_id(0) == 0)
    def _init():
        ri = lax.broadcasted_iota(jnp.int32, (rows_q, n_keys), 0)
        si = lax.broadcasted_iota(jnp.int32, (rows_q, n_keys), 1)
        ti = ri
        slope = jnp.zeros((rows_q, n_keys), F32)
        for h in range(N_HEADS):
            in_head = (ri >= h * dec) & (ri < (h + 1) * dec)
            ti = jnp.where(in_head, ri - h * dec, ti)
            slope = jnp.where(in_head, _alibi_slope(h), slope)
        dist = WINDOW + ti - si
        valid = (dist >= 0) & (dist < WINDOW)
        distf = dist.astype(F32)
        bias_ref[...] = jnp.where(valid, -slope * distf, NEG_INF)
        rc = lax.broadcasted_iota(jnp.int32, (rows_q, 1), 0)
        sink_rows = jnp.zeros((rows_q, 1), F32)
        for h in range(N_HEADS):
            sink_rows = jnp.where((rc >= h * dec) & (rc < (h + 1) * dec), sink_ref[h], sink_rows)
        sinkcol_ref[...] = sink_rows
        for j in range(CONV_W):
            wb_ref[j] = jnp.broadcast_to(dww_ref[j:j + 1, :], (SUBLANES, d_conv))
        n_rows = group * dec
        pi = lax.broadcasted_iota(jnp.int32, (n_rows, n_rows), 0)
        pj = lax.broadcasted_iota(jnp.int32, (n_rows, n_rows), 1)
        shift = group.bit_length() - 1
        source = lambda i: (i & (group - 1)) * dec + lax.shift_right_logical(i, shift)
        perm_ref[...] = jnp.where(pj == source(pi), 1.0, 0.0).astype(BF16)
        perm_t_ref[...] = jnp.where(pi == source(pj), 1.0, 0.0).astype(BF16)

    x = x_ref[...]
    h_in = _rms_norm(x, ng_ref[...]).astype(BF16)
    q_ref[...] = _dot(h_in, win_ref[:, :d_attn]) * (HEAD_DIM ** -0.5)
    kv = _dot(h_in, win_ref[:, d_attn:d_attn + 2 * d_kv])
    kn_ref[...] = kv[:, :d_kv]
    vn_ref[...] = kv[:, d_kv:]
    g_a = _dot(h_in, win_ref[:, d_attn + 2 * d_kv:d_attn + 2 * d_kv + d_attn])

    h_tm = _dot(perm_ref[...], h_in).astype(BF16)
    u_off = d_attn + 2 * d_kv + d_attn
    u_ab = _dot(h_tm, win_ref[:, u_off:u_off + 2 * d_conv])
    u_tm = u_ab[:, :d_conv] * jax.nn.sigmoid(u_ab[:, d_conv:])
    g_c_tm = _dot(h_tm, win_ref[:, u_off + 2 * d_conv:])

    lo = lax.broadcasted_iota(jnp.int32, (dec, LANES), 1) < HALF
    zero = jnp.zeros((dec, LANES), F32)
    sink = sinkcol_ref[...]
    bias = bias_ref[...]

    def scores(b):
        r = pl.multiple_of(b * dec, dec)
        k_all = jnp.concatenate([ck_ref[b].T, kn_ref[pl.ds(r, dec), :]], axis=0)
        ko_ref[b] = k_all[dec:, :]
        q_b = q_ref[pl.ds(r, dec), :]
        rows = []
        for h in range(N_HEADS):
            tile_q = q_b[:, (h // 2) * LANES:(h // 2 + 1) * LANES]
            in_lo = h % 2 == 0
            want_lo = h // GQA_GROUP == 0
            src = tile_q if in_lo == want_lo else pltpu.roll(tile_q, HALF, axis=1)
            rows.append(jnp.where(lo, src, zero) if want_lo else jnp.where(lo, zero, src))
        q_rows = jnp.concatenate(rows, axis=0).astype(BF16)
        return _dot_nt(q_rows, k_all.astype(BF16)) + bias

    def attend(b, s):
        r = pl.multiple_of(b * dec, dec)
        v_all = jnp.concatenate([cv_ref[b].T, vn_ref[pl.ds(r, dec), :]], axis=0)
        vo_ref[b] = v_all[dec:, :]
        p, rden = _softmax_sink(s, sink)
        o = _dot(p.astype(BF16), v_all.astype(BF16)) * rden
        tiles = []
        for pair in range(N_PAIRS):
            a = o[(2 * pair) * dec:(2 * pair + 1) * dec, :]
            c = o[(2 * pair + 1) * dec:(2 * pair + 2) * dec, :]
            if pair // PAIRS_PER_KV == 0:
                tiles.append(jnp.where(lo, a, pltpu.roll(c, HALF, axis=1)))
            else:
                tiles.append(jnp.where(lo, pltpu.roll(a, HALF, axis=1), c))
        attn_ref[pl.ds(r, dec), :] = jnp.concatenate(tiles, axis=1)

    upt_ref[:CONV_TAIL] = cs_ref[...]
    for tt in range(dec):
        upt_ref[CONV_TAIL + tt] = u_tm[tt * group:(tt + 1) * group, :]
    co_ref[...] = upt_ref[dec:]

    def conv_step(tt):
        for g0 in range(0, group, SUBLANES):
            acc = jnp.zeros((SUBLANES, d_conv), F32)
            for j in range(CONV_W):
                acc = acc + upt_ref[tt + j, g0:g0 + SUBLANES, :] * wb_ref[j]
            start = tt * group + g0
            if not isinstance(start, int):
                start = pl.multiple_of(start, SUBLANES)
            conv_ref[pl.ds(start, SUBLANES), :] = acc

    trips = group // SAMPLE_UNROLL
    conv_per_trip = dec // trips

    def attention_batch(i, carry):
        elems = [i * SAMPLE_UNROLL + e for e in range(SAMPLE_UNROLL)]
        ss = [scores(b) for b in elems]
        for b, s in zip(elems, ss):
            attend(b, s)
        for c in range(conv_per_trip):
            conv_step(i * conv_per_trip + c)
        return carry

    lax.fori_loop(0, trips, attention_batch, 0)
    for tt in range(trips * conv_per_trip, dec):
        conv_step(tt)

    c2_tm = _conv_branch_tail(conv_ref[...] + dwb_ref[...], lng_ref[...], lnb_ref[...], wpw_ref)
    m_c = _dot(perm_t_ref[...], _gate(c2_tm, g_c_tm)).astype(BF16)
    y_ref[...] = _mix_out(x, _gate(attn_ref[...], g_a), m_c, wout_ref, fg_ref[...], d_attn, final_norm)


def _sample_layer(x, ck, cv, cs, sinks, ng, win, dww, dwb, lng, lnb, wpw, wout, fg, final_norm):
    Bs, dec, D = x.shape
    group = SAMPLE_GROUP
    assert Bs % group == 0 and dec % 8 == 0 and dec <= CONV_TAIL
    d_conv = dww.shape[1]
    d_kv = N_KV_HEADS * HEAD_DIM
    d_attn = N_HEADS * HEAD_DIM
    rows = group * dec
    x2 = x.reshape(Bs * dec, D)
    assert group & (group - 1) == 0
    const = lambda shape: pl.BlockSpec(shape, lambda g: (0,) * len(shape))
    per_group = lambda shape: pl.BlockSpec((group,) + shape, lambda g: (g,) + (0,) * len(shape))
    conv_state = pl.BlockSpec((CONV_TAIL, group, d_conv), lambda g: (0, g, 0))
    y, ko, vo, co = pl.pallas_call(
        functools.partial(_sample_kernel, group=group, dec=dec, final_norm=final_norm),
        grid=(Bs // group,),
        in_specs=[
            pl.BlockSpec((rows, D), lambda g: (g, 0)),
            per_group((d_kv, WINDOW)), per_group((d_kv, WINDOW)), conv_state,
            pl.BlockSpec(memory_space=pltpu.SMEM),
            const(ng.shape), const(win.shape), const(dww.shape), const(dwb.shape),
            const(lng.shape), const(lnb.shape), const(wpw.shape), const(wout.shape), const(fg.shape),
        ],
        out_specs=[
            pl.BlockSpec((rows, D), lambda g: (g, 0)),
            per_group((WINDOW, d_kv)), per_group((WINDOW, d_kv)), conv_state,
        ],
        out_shape=[
            jax.ShapeDtypeStruct((Bs * dec, D), F32),
            jax.ShapeDtypeStruct((Bs, WINDOW, d_kv), F32),
            jax.ShapeDtypeStruct((Bs, WINDOW, d_kv), F32),
            jax.ShapeDtypeStruct((CONV_TAIL, Bs, d_conv), F32),
        ],
        scratch_shapes=[
            pltpu.VMEM((rows, d_attn), F32),
            pltpu.VMEM((rows, d_kv), F32),
            pltpu.VMEM((rows, d_kv), F32),
            pltpu.VMEM((CONV_TAIL + dec, group, d_conv), F32),
            pltpu.VMEM((CONV_W, SUBLANES, d_conv), F32),
            pltpu.VMEM((N_HEADS * dec, WINDOW + dec), F32),
            pltpu.VMEM((N_HEADS * dec, 1), F32),
            pltpu.VMEM((rows, d_attn), F32),
            pltpu.VMEM((rows, d_conv), F32),
            pltpu.VMEM((rows, rows), BF16),
            pltpu.VMEM((rows, rows), BF16),
        ],
        compiler_params=pltpu.CompilerParams(
            dimension_semantics=("arbitrary",), vmem_limit_bytes=VMEM_LIMIT_BYTES),
        name="sample_layer",
    )(x2, ck, cv, cs, sinks, ng, win, dww, dwb, lng, lnb, wpw, wout, fg)
    return y.reshape(Bs, dec, D), ko, vo, co


def kernel(x_prompt, x_sample, cache_k, cache_v, state_conv, norm_g, w_in, attn_sinks, dw_w, dw_b,
           conv_ln_g, conv_ln_b, w_pw2, w_out, final_norm_g):
    depth = w_in.shape[0]
    B = x_prompt.shape[0]
    Bs, dec = x_sample.shape[0], x_sample.shape[1]
    d_kv = N_KV_HEADS * HEAD_DIM
    fg = final_norm_g.reshape(1, -1)
    hp, hs = x_prompt, x_sample
    pk, pv, pc, sk, sv, sc = [], [], [], [], [], []
    kv_major = lambda c: c.transpose(0, 2, 3, 1).reshape(Bs, d_kv, WINDOW)
    for l in range(depth):
        final_norm = l == depth - 1
        row = lambda a: a[l].reshape(1, -1)
        win, wpw, wout = _cast_weights(w_in[l], w_pw2[l], w_out[l])
        shared = (row(norm_g), win, dw_w[l], row(dw_b), row(conv_ln_g), row(conv_ln_b), wpw, wout, fg)
        hp, k_p, v_p, c_p = _prompt_layer(hp, attn_sinks[l], *shared, final_norm)
        hs, k_s, v_s, c_s = _sample_layer(
            hs, kv_major(cache_k[l]), kv_major(cache_v[l]),
            state_conv[l].swapaxes(0, 1), attn_sinks[l], *shared, final_norm)
        key_major = lambda s: s.reshape(B, N_KV_HEADS, HEAD_DIM, WINDOW).transpose(0, 3, 1, 2)
        pk.append(key_major(k_p))
        pv.append(key_major(v_p))
        pc.append(c_p)
        sk.append(k_s.reshape(Bs, WINDOW, N_KV_HEADS, HEAD_DIM))
        sv.append(v_s.reshape(Bs, WINDOW, N_KV_HEADS, HEAD_DIM))
        sc.append(c_s.swapaxes(0, 1))
    stack = lambda xs: xs[0][None] if len(xs) == 1 else jnp.stack(xs)
    return (hp, hs, stack(pk), stack(pv), stack(pc), stack(sk), stack(sv), stack(sc))
```

```python
import functools

import jax
import jax.numpy as jnp
from jax import lax
from jax.experimental import pallas as pl
from jax.experimental.pallas import tpu as pltpu

HEAD_DIM = 64
N_HEADS = 8
N_KV_HEADS = 2
GQA_GROUP = N_HEADS // N_KV_HEADS
N_PAIRS = N_HEADS // 2
PAIRS_PER_KV = N_PAIRS // N_KV_HEADS
WINDOW = 128
CONV_W = 31
CONV_TAIL = CONV_W - 1
EPS = 1e-5
LANES = 128
SUBLANES = 8
HALF = LANES // 2
NEG_INF = float("-inf")
LOG2E = 1.4426950408889634

PROMPT_TILE = 1024
CONV_ROWS = 64
CONV_CH_BLOCKS = 2
CONV_PAD = 32
CAST_STEPS = 4
SAMPLE_GROUP = 32
SAMPLE_UNROLL = 4
VMEM_LIMIT_BYTES = 56 * 1024 * 1024

F32 = jnp.float32
BF16 = jnp.bfloat16


def _alibi_slope(h):
    return 2.0 ** (-8.0 * (h + 1) / N_HEADS)


def _rms_norm(x, g):
    ms = jnp.mean(x * x, axis=-1, keepdims=True)
    return x * lax.rsqrt(ms + EPS) * g


def _silu(x):
    return x * jax.nn.sigmoid(x)


def _dot(a, b):
    return jnp.dot(a, b, preferred_element_type=F32)


def _dot_nt(a, b):
    return lax.dot_general(a, b, (((1,), (1,)), ((), ())), preferred_element_type=F32)


def _conv_branch_tail(c, lng, lnb, wpw_ref):
    mu = jnp.mean(c, axis=-1, keepdims=True)
    xc = c - mu
    var = jnp.mean(xc * xc, axis=-1, keepdims=True)
    y = xc * lax.rsqrt(var + EPS) * lng + lnb
    return _dot(_silu(y).astype(BF16), wpw_ref[...])


def _gate(val, g):
    return (val * _silu(g)).astype(BF16)


def _softmax_sink(s, sink):
    m = jnp.maximum(jnp.max(s, axis=-1, keepdims=True), sink)
    p = jnp.exp(s - m)
    den = jnp.sum(p, axis=-1, keepdims=True) + jnp.exp(sink - m)
    return p, 1.0 / den


def _cast_kernel(*refs):
    n = len(refs) // 2
    for src, dst in zip(refs[:n], refs[n:]):
        dst[...] = src[...].astype(BF16)


def _cast_weights(*ws):
    steps = CAST_STEPS
    assert all(w.shape[0] % (steps * 2 * SUBLANES) == 0 for w in ws)
    spec = lambda w: pl.BlockSpec((w.shape[0] // steps, w.shape[1]), lambda i: (i, 0))
    return pl.pallas_call(
        _cast_kernel,
        grid=(steps,),
        in_specs=[spec(w) for w in ws],
        out_specs=[spec(w) for w in ws],
        out_shape=[jax.ShapeDtypeStruct(w.shape, BF16) for w in ws],
        name="cast_weights",
    )(*ws)


def _prompt_kernel(x_ref, sink_ref, ng_ref, win_ref, dww_ref, dwb_ref, lng_ref, lnb_ref,
                   wpw_ref, wout_ref, fg_ref,
                   y_ref, kst_ref, vst_ref, cst_ref,
                   k_ref, vt_ref, up_ref, shifted_ref, wb_ref, bias_ref, attn_ref, conv_ref,
                   *, tile, final_norm):
    t = pl.program_id(1)
    d_attn = N_HEADS * HEAD_DIM
    d_kv = N_KV_HEADS * HEAD_DIM
    d_conv = up_ref.shape[1]
    n_blocks = tile // WINDOW
    two_w = 2 * WINDOW
    u_off = d_attn + 2 * d_kv + d_attn
    n_shift = shifted_ref.shape[1]

    @pl.when(t == 0)
    def _init():
        k_ref[:WINDOW, :] = jnp.zeros((WINDOW, d_kv), BF16)
        vt_ref[:, :WINDOW] = jnp.zeros((d_kv, WINDOW), BF16)
        up_ref[:CONV_PAD, :] = jnp.zeros((CONV_PAD, d_conv), F32)
        for j in range(CONV_W):
            wb_ref[j] = jnp.broadcast_to(dww_ref[j:j + 1, :], (SUBLANES, d_conv))
        ji = lax.broadcasted_iota(jnp.int32, (two_w, two_w), 0)
        ci = lax.broadcasted_iota(jnp.int32, (two_w, two_w), 1)
        odd = ci >= WINDOW
        dist = jnp.where(odd, ci - WINDOW, ci) + WINDOW - ji
        valid = (dist >= 0) & (dist < WINDOW)
        valid_first = valid & (ji >= WINDOW)
        distf = dist.astype(F32) * LOG2E
        for pair in range(N_PAIRS):
            pen = -jnp.where(odd, _alibi_slope(2 * pair + 1), _alibi_slope(2 * pair)) * distf
            bias_ref[0, pair] = jnp.where(valid, pen, NEG_INF)
            bias_ref[1, pair] = jnp.where(valid_first, pen, NEG_INF)

    x = x_ref[...]
    h_in = _rms_norm(x, ng_ref[...]).astype(BF16)


    cw = d_conv // CONV_CH_BLOCKS
    for cb in range(CONV_CH_BLOCKS):
        ch = slice(cb * cw, (cb + 1) * cw)
        u_a = _dot(h_in, win_ref[:, u_off + cb * cw:u_off + (cb + 1) * cw])
        u_b = _dot(h_in, win_ref[:, u_off + d_conv + cb * cw:u_off + d_conv + (cb + 1) * cw])
        up_ref[CONV_PAD:, ch] = u_a * jax.nn.sigmoid(u_b)
        for s in range(1, SUBLANES):
            shifted_ref[s - 1, :, ch] = up_ref[s:s + n_shift, ch]
    cst_ref[...] = up_ref[CONV_PAD + tile - CONV_TAIL:, :]

    kv = _dot(h_in, win_ref[:, d_attn:d_attn + 2 * d_kv])
    k = kv[:, :d_kv]
    v = kv[:, d_kv:]
    k_ref[WINDOW:, :] = k.astype(BF16)
    v_t = v.T
    vt_ref[:, WINDOW:] = v_t.astype(BF16)
    kst_ref[...] = k[tile - WINDOW:, :].T
    vst_ref[...] = v_t[:, tile - WINDOW:]

    groups = CONV_ROWS // SUBLANES

    def conv_chunk(r, cb):
        ch = slice(cb * cw, (cb + 1) * cw)
        acc = [jnp.zeros((SUBLANES, cw), F32) for _ in range(groups)]
        loaded = {}
        for j in range(CONV_W):
            off = CONV_PAD - CONV_TAIL + j
            a, s = off // SUBLANES, off % SUBLANES
            wb = wb_ref[j, :, ch]
            for g in range(groups):
                if (s, a + g) not in loaded:
                    rows = slice(r + (a + g) * SUBLANES, r + (a + g + 1) * SUBLANES)
                    loaded[s, a + g] = up_ref[rows, ch] if s == 0 else shifted_ref[s - 1, rows, ch]
                acc[g] = acc[g] + loaded[s, a + g] * wb
        for g in range(groups):
            conv_ref[r + g * SUBLANES:r + (g + 1) * SUBLANES, ch] = acc[g]

    q = _dot(h_in, win_ref[:, :d_attn]) * (HEAD_DIM ** -0.5 * LOG2E)
    g_a = _dot(h_in, win_ref[:, d_attn + 2 * d_kv:u_off])
    g_c = _dot(h_in, win_ref[:, u_off + 2 * d_conv:])

    lo = lax.broadcasted_iota(jnp.int32, (WINDOW, LANES), 1) < HALF
    zero = jnp.zeros((WINDOW, LANES), F32)
    odd_head = lax.broadcasted_iota(jnp.int32, (1, two_w), 1) >= WINDOW

    def scores(blk, pair):
        r0 = blk * WINDOW
        first = jnp.where(t == 0, 1, 0) if blk == 0 else 0
        kvh = pair // PAIRS_PER_KV
        qp = q[r0:r0 + WINDOW, pair * LANES:(pair + 1) * LANES]
        rolled = pltpu.roll(qp, HALF, axis=1)
        if kvh == 0:
            q_even, q_odd = jnp.where(lo, qp, zero), jnp.where(lo, rolled, zero)
        else:
            q_even, q_odd = jnp.where(lo, zero, rolled), jnp.where(lo, zero, qp)
        qm = jnp.concatenate([q_even, q_odd], axis=0).astype(BF16)
        return _dot_nt(k_ref[r0:r0 + two_w, :], qm) + bias_ref[first, pair]

    def attend(blk, pair, s):
        r0 = blk * WINDOW
        kvh = pair // PAIRS_PER_KV
        sink = jnp.where(odd_head, sink_ref[2 * pair + 1], sink_ref[2 * pair]) * LOG2E
        m = jnp.maximum(jnp.max(s, axis=0, keepdims=True), sink)
        p = jnp.exp2(s - m)
        den = jnp.sum(p, axis=0, keepdims=True) + jnp.exp2(sink - m)
        vt_win = vt_ref[kvh * HEAD_DIM:(kvh + 1) * HEAD_DIM, r0:r0 + two_w]
        ot = _dot(vt_win, p.astype(BF16)) * (1.0 / den)
        o = jnp.concatenate([ot[:, :WINDOW], ot[:, WINDOW:]], axis=0).T
        attn_ref[r0:r0 + WINDOW, pair * LANES:(pair + 1) * LANES] = o

    steps = [(blk, pair) for blk in range(n_blocks) for pair in range(N_PAIRS)]
    s_next = scores(*steps[0])
    for i, step in enumerate(steps):
        s_cur = s_next
        if i + 1 < len(steps):
            s_next = scores(*steps[i + 1])
        attend(*step, s_cur)

    y_a = _dot(_gate(attn_ref[...], g_a), wout_ref[:d_attn, :])
    for cb in range(CONV_CH_BLOCKS):
        for r in range(0, tile, CONV_ROWS):
            conv_chunk(r, cb)

    c2 = _conv_branch_tail(conv_ref[...] + dwb_ref[...], lng_ref[...], lnb_ref[...], wpw_ref)
    y = x + y_a + _dot(_gate(c2, g_c), wout_ref[d_attn:, :])
    y_ref[...] = _rms_norm(y, fg_ref[...]) if final_norm else y

    k_ref[:WINDOW, :] = k_ref[tile:, :]
    vt_ref[:, :WINDOW] = vt_ref[:, tile:]
    up_ref[:CONV_PAD, :] = up_ref[tile:, :]


def _prompt_layer(x, sinks, ng, win, dww, dwb, lng, lnb, wpw, wout, fg, final_norm):
    B, T, D = x.shape
    tile = PROMPT_TILE
    assert T % tile == 0 and tile % WINDOW == 0 and tile % CONV_ROWS == 0
    d_conv = dww.shape[1]
    d_kv = N_KV_HEADS * HEAD_DIM
    d_attn = N_HEADS * HEAD_DIM
    n_t = T // tile
    const = lambda shape: pl.BlockSpec(shape, lambda b, t: (0,) * len(shape))
    return pl.pallas_call(
        functools.partial(_prompt_kernel, tile=tile, final_norm=final_norm),
        grid=(B, n_t),
        in_specs=[
            pl.BlockSpec((None, tile, D), lambda b, t: (b, t, 0)),
            pl.BlockSpec(memory_space=pltpu.SMEM),
            const(ng.shape), const(win.shape), const(dww.shape), const(dwb.shape),
            const(lng.shape), const(lnb.shape), const(wpw.shape), const(wout.shape), const(fg.shape),
        ],
        out_specs=[
            pl.BlockSpec((None, tile, D), lambda b, t: (b, t, 0)),
            pl.BlockSpec((None, d_kv, WINDOW), lambda b, t: (b, 0, 0)),
            pl.BlockSpec((None, d_kv, WINDOW), lambda b, t: (b, 0, 0)),
            pl.BlockSpec((None, CONV_TAIL, d_conv), lambda b, t: (b, 0, 0)),
        ],
        scratch_shapes=[
            pltpu.VMEM((WINDOW + tile, d_kv), BF16),
            pltpu.VMEM((d_kv, WINDOW + tile), BF16),
            pltpu.VMEM((CONV_PAD + tile, d_conv), F32),
            pltpu.VMEM((SUBLANES - 1, CONV_PAD - SUBLANES + tile, d_conv), F32),
            pltpu.VMEM((CONV_W, SUBLANES, d_conv), F32),
            pltpu.VMEM((2, N_PAIRS, 2 * WINDOW, 2 * WINDOW), F32),
            pltpu.VMEM((tile, d_attn), F32),
            pltpu.VMEM((tile, d_conv), F32),
        ],
        out_shape=[
            jax.ShapeDtypeStruct((B, T, D), F32),
            jax.ShapeDtypeStruct((B, d_kv, WINDOW), F32),
            jax.ShapeDtypeStruct((B, d_kv, WINDOW), F32),
            jax.ShapeDtypeStruct((B, CONV_TAIL, d_conv), F32),
        ],
        compiler_params=pltpu.CompilerParams(
            dimension_semantics=("arbitrary", "arbitrary"), vmem_limit_bytes=VMEM_LIMIT_BYTES),
        name="prompt_layer",
    )(x, sinks, ng, win, dww, dwb, lng, lnb, wpw, wout, fg)


def _sample_kernel(x_ref, ck_ref, cv_ref, cs_ref, sink_ref, ng_ref, win_ref,
                   dww_ref, dwb_ref, lng_ref, lnb_ref, wpw_ref, wout_ref, fg_ref,
                   y_ref, ko_ref, vo_ref, co_ref,
                   q_ref, kn_ref, vn_ref, upt_ref, wb_ref, bias_ref, sinkcol_ref, attn_ref, conv_ref,
                   perm_ref, perm_t_ref, *, group, dec, final_norm):
    d_attn = N_HEADS * HEAD_DIM
    d_kv = N_KV_HEADS * HEAD_DIM
    d_conv = conv_ref.shape[1]
    n_keys = WINDOW + dec
    rows_q = N_HEADS * dec

    @pl.when(pl.program_id(0) == 0)
    def _init():
        ri = lax.broadcasted_iota(jnp.int32, (rows_q, n_keys), 0)
        si = lax.broadcasted_iota(jnp.int32, (rows_q, n_keys), 1)
        ti = ri
        slope = jnp.zeros((rows_q, n_keys), F32)
        for h in range(N_HEADS):
            in_head = (ri >= h * dec) & (ri < (h + 1) * dec)
            ti = jnp.where(in_head, ri - h * dec, ti)
            slope = jnp.where(in_head, _alibi_slope(h), slope)
        dist = WINDOW + ti - si
        valid = (dist >= 0) & (dist < WINDOW)
        distf = dist.astype(F32)
        bias_ref[...] = jnp.where(valid, -slope * distf, NEG_INF)
        rc = lax.broadcasted_iota(jnp.int32, (rows_q, 1), 0)
        sink_rows = jnp.zeros((rows_q, 1), F32)
        for h in range(N_HEADS):
            sink_rows = jnp.where((rc >= h * dec) & (rc < (h + 1) * dec), sink_ref[h], sink_rows)
        sinkcol_ref[...] = sink_rows
        for j in range(CONV_W):
            wb_ref[j] = jnp.broadcast_to(dww_ref[j:j + 1, :], (SUBLANES, d_conv))
        n_rows = group * dec
        pi = lax.broadcasted_iota(jnp.int32, (n_rows, n_rows), 0)
        pj = lax.broadcasted_iota(jnp.int32, (n_rows, n_rows), 1)
        shift = group.bit_length() - 1
        source = lambda i: (i & (group - 1)) * dec + lax.shift_right_logical(i, shift)
        perm_ref[...] = jnp.where(pj == source(pi), 1.0, 0.0).astype(BF16)
        perm_t_ref[...] = jnp.where(pi == source(pj), 1.0, 0.0).astype(BF16)

    x = x_ref[...]
    h_in = _rms_norm(x, ng_ref[...]).astype(BF16)
    q_ref[...] = _dot(h_in, win_ref[:, :d_attn]) * (HEAD_DIM ** -0.5)
    kv = _dot(h_in, win_ref[:, d_attn:d_attn + 2 * d_kv])
    kn_ref[...] = kv[:, :d_kv]
    vn_ref[...] = kv[:, d_kv:]
    g_a = _dot(h_in, win_ref[:, d_attn + 2 * d_kv:d_attn + 2 * d_kv + d_attn])

    h_tm = _dot(perm_ref[...], h_in).astype(BF16)
    u_off = d_attn + 2 * d_kv + d_attn
    u_ab = _dot(h_tm, win_ref[:, u_off:u_off + 2 * d_conv])
    u_tm = u_ab[:, :d_conv] * jax.nn.sigmoid(u_ab[:, d_conv:])
    g_c_tm = _dot(h_tm, win_ref[:, u_off + 2 * d_conv:])

    lo = lax.broadcasted_iota(jnp.int32, (dec, LANES), 1) < HALF
    zero = jnp.zeros((dec, LANES), F32)
    sink = sinkcol_ref[...]
    bias = bias_ref[...]

    def scores(b):
        r = pl.multiple_of(b * dec, dec)
        k_all = jnp.concatenate([ck_ref[b].T, kn_ref[pl.ds(r, dec), :]], axis=0)
        ko_ref[b] = k_all[dec:, :]
        q_b = q_ref[pl.ds(r, dec), :]
        rows = []
        for h in range(N_HEADS):
            tile_q = q_b[:, (h // 2) * LANES:(h // 2 + 1) * LANES]
            in_lo = h % 2 == 0
            want_lo = h // GQA_GROUP == 0
            src = tile_q if in_lo == want_lo else pltpu.roll(tile_q, HALF, axis=1)
            rows.append(jnp.where(lo, src, zero) if want_lo else jnp.where(lo, zero, src))
        q_rows = jnp.concatenate(rows, axis=0).astype(BF16)
        return _dot_nt(q_rows, k_all.astype(BF16)) + bias

    def attend(b, s):
        r = pl.multiple_of(b * dec, dec)
        v_all = jnp.concatenate([cv_ref[b].T, vn_ref[pl.ds(r, dec), :]], axis=0)
        vo_ref[b] = v_all[dec:, :]
        p, rden = _softmax_sink(s, sink)
        o = _dot(p.astype(BF16), v_all.astype(BF16)) * rden
        tiles = []
        for pair in range(N_PAIRS):
            a = o[(2 * pair) * dec:(2 * pair + 1) * dec, :]
            c = o[(2 * pair + 1) * dec:(2 * pair + 2) * dec, :]
            if pair // PAIRS_PER_KV == 0:
                tiles.append(jnp.where(lo, a, pltpu.roll(c, HALF, axis=1)))
            else:
                tiles.append(jnp.where(lo, pltpu.roll(a, HALF, axis=1), c))
        attn_ref[pl.ds(r, dec), :] = jnp.concatenate(tiles, axis=1)

    upt_ref[:CONV_TAIL] = cs_ref[...]
    for tt in range(dec):
        upt_ref[CONV_TAIL + tt] = u_tm[tt * group:(tt + 1) * group, :]
    co_ref[...] = upt_ref[dec:]

    def conv_step(tt):
        for g0 in range(0, group, SUBLANES):
            acc = jnp.zeros((SUBLANES, d_conv), F32)
            for j in range(CONV_W):
                acc = acc + upt_ref[tt + j, g0:g0 + SUBLANES, :] * wb_ref[j]
            start = tt * group + g0
            if not isinstance(start, int):
                start = pl.multiple_of(start, SUBLANES)
            conv_ref[pl.ds(start, SUBLANES), :] = acc

    trips = group // SAMPLE_UNROLL
    conv_per_trip = dec // trips

    def attention_batch(i, carry):
        elems = [i * SAMPLE_UNROLL + e for e in range(SAMPLE_UNROLL)]
        ss = [scores(b) for b in elems]
        for b, s in zip(elems, ss):
            attend(b, s)
        for c in range(conv_per_trip):
            conv_step(i * conv_per_trip + c)
        return carry

    lax.fori_loop(0, trips, attention_batch, 0)
    for tt in range(trips * conv_per_trip, dec):
        conv_step(tt)

    y_a = _dot(_gate(attn_ref[...], g_a), wout_ref[:d_attn, :])
    c2_tm = _conv_branch_tail(conv_ref[...] + dwb_ref[...], lng_ref[...], lnb_ref[...], wpw_ref)
    m_c = _dot(perm_t_ref[...], _gate(c2_tm, g_c_tm)).astype(BF16)
    y = x + y_a + _dot(m_c, wout_ref[d_attn:, :])
    y_ref[...] = _rms_norm(y, fg_ref[...]) if final_norm else y


def _sample_layer(x, ck, cv, cs, sinks, ng, win, dww, dwb, lng, lnb, wpw, wout, fg, final_norm):
    Bs, dec, D = x.shape
    group = SAMPLE_GROUP
    assert Bs % group == 0 and dec % 8 == 0 and dec <= CONV_TAIL
    d_conv = dww.shape[1]
    d_kv = N_KV_HEADS * HEAD_DIM
    d_attn = N_HEADS * HEAD_DIM
    rows = group * dec
    x2 = x.reshape(Bs * dec, D)
    assert group & (group - 1) == 0
    const = lambda shape: pl.BlockSpec(shape, lambda g: (0,) * len(shape))
    per_group = lambda shape: pl.BlockSpec((group,) + shape, lambda g: (g,) + (0,) * len(shape))
    conv_state = pl.BlockSpec((CONV_TAIL, group, d_conv), lambda g: (0, g, 0))
    y, ko, vo, co = pl.pallas_call(
        functools.partial(_sample_kernel, group=group, dec=dec, final_norm=final_norm),
        grid=(Bs // group,),
        in_specs=[
            pl.BlockSpec((rows, D), lambda g: (g, 0)),
            per_group((d_kv, WINDOW)), per_group((d_kv, WINDOW)), conv_state,
            pl.BlockSpec(memory_space=pltpu.SMEM),
            const(ng.shape), const(win.shape), const(dww.shape), const(dwb.shape),
            const(lng.shape), const(lnb.shape), const(wpw.shape), const(wout.shape), const(fg.shape),
        ],
        out_specs=[
            pl.BlockSpec((rows, D), lambda g: (g, 0)),
            per_group((WINDOW, d_kv)), per_group((WINDOW, d_kv)), conv_state,
        ],
        out_shape=[
            jax.ShapeDtypeStruct((Bs * dec, D), F32),
            jax.ShapeDtypeStruct((Bs, WINDOW, d_kv), F32),
            jax.ShapeDtypeStruct((Bs, WINDOW, d_kv), F32),
            jax.ShapeDtypeStruct((CONV_TAIL, Bs, d_conv), F32),
        ],
        scratch_shapes=[
            pltpu.VMEM((rows, d_attn), F32),
            pltpu.VMEM((rows, d_kv), F32),
            pltpu.VMEM((rows, d_kv), F32),
            pltpu.VMEM((CONV_TAIL + dec, group, d_conv), F32),
            pltpu.VMEM((CONV_W, SUBLANES, d_conv), F32),
            pltpu.VMEM((N_HEADS * dec, WINDOW + dec), F32),
            pltpu.VMEM((N_HEADS * dec, 1), F32),
            pltpu.VMEM((rows, d_attn), F32),
            pltpu.VMEM((rows, d_conv), F32),
            pltpu.VMEM((rows, rows), BF16),
            pltpu.VMEM((rows, rows), BF16),
        ],
        compiler_params=pltpu.CompilerParams(
            dimension_semantics=("arbitrary",), vmem_limit_bytes=VMEM_LIMIT_BYTES),
        name="sample_layer",
    )(x2, ck, cv, cs, sinks, ng, win, dww, dwb, lng, lnb, wpw, wout, fg)
    return y.reshape(Bs, dec, D), ko, vo, co


def kernel(x_prompt, x_sample, cache_k, cache_v, state_conv, norm_g, w_in, attn_sinks, dw_w, dw_b,
           conv_ln_g, conv_ln_b, w_pw2, w_out, final_norm_g):
    depth = w_in.shape[0]
    B = x_prompt.shape[0]
    Bs, dec = x_sample.shape[0], x_sample.shape[1]
    d_kv = N_KV_HEADS * HEAD_DIM
    fg = final_norm_g.reshape(1, -1)
    hp, hs = x_prompt, x_sample
    pk, pv, pc, sk, sv, sc = [], [], [], [], [], []
    kv_major = lambda c: c.transpose(0, 2, 3, 1).reshape(Bs, d_kv, WINDOW)
    for l in range(depth):
        final_norm = l == depth - 1
        row = lambda a: a[l].reshape(1, -1)
        win, wpw, wout = _cast_weights(w_in[l], w_pw2[l], w_out[l])
        shared = (row(norm_g), win, dw_w[l], row(dw_b), row(conv_ln_g), row(conv_ln_b), wpw, wout, fg)
        hp, k_p, v_p, c_p = _prompt_layer(hp, attn_sinks[l], *shared, final_norm)
        hs, k_s, v_s, c_s = _sample_layer(
            hs, kv_major(cache_k[l]), kv_major(cache_v[l]),
            state_conv[l].swapaxes(0, 1), attn_sinks[l], *shared, final_norm)
        key_major = lambda s: s.reshape(B, N_KV_HEADS, HEAD_DIM, WINDOW).transpose(0, 3, 1, 2)
        pk.append(key_major(k_p))
        pv.append(key_major(v_p))
        pc.append(c_p)
        sk.append(k_s.reshape(Bs, WINDOW, N_KV_HEADS, HEAD_DIM))
        sv.append(v_s.reshape(Bs, WINDOW, N_KV_HEADS, HEAD_DIM))
        sc.append(c_s.swapaxes(0, 1))
    stack = lambda xs: xs[0][None] if len(xs) == 1 else jnp.stack(xs)
    return (hp, hs, stack(pk), stack(pv), stack(pc), stack(sk), stack(sv), stack(sc))
```

```python
import functools

import jax
import jax.numpy as jnp
from jax import lax
from jax.experimental import pallas as pl
from jax.experimental.pallas import tpu as pltpu

HEAD_DIM = 64
N_HEADS = 8
N_KV_HEADS = 2
GQA_GROUP = N_HEADS // N_KV_HEADS
N_PAIRS = N_HEADS // 2
PAIRS_PER_KV = N_PAIRS // N_KV_HEADS
WINDOW = 128
CONV_W = 31
CONV_TAIL = CONV_W - 1
EPS = 1e-5
LANES = 128
SUBLANES = 8
HALF = LANES // 2
NEG_INF = float("-inf")
LOG2E = 1.4426950408889634

PROMPT_TILE = 1024
CONV_ROWS = 64
CONV_CH_BLOCKS = 2
CONV_PAD = 32
CAST_STEPS = 4
SAMPLE_GROUP = 32
SAMPLE_UNROLL = 4
VMEM_LIMIT_BYTES = 56 * 1024 * 1024

F32 = jnp.float32
BF16 = jnp.bfloat16


def _alibi_slope(h):
    return 2.0 ** (-8.0 * (h + 1) / N_HEADS)


def _rms_norm(x, g):
    ms = jnp.mean(x * x, axis=-1, keepdims=True)
    return x * lax.rsqrt(ms + EPS) * g


def _silu(x):
    return x * jax.nn.sigmoid(x)


def _dot(a, b):
    return jnp.dot(a, b, preferred_element_type=F32)


def _dot_nt(a, b):
    return lax.dot_general(a, b, (((1,), (1,)), ((), ())), preferred_element_type=F32)


def _conv_branch_tail(c, lng, lnb, wpw_ref):
    mu = jnp.mean(c, axis=-1, keepdims=True)
    xc = c - mu
    var = jnp.mean(xc * xc, axis=-1, keepdims=True)
    y = xc * lax.rsqrt(var + EPS) * lng + lnb
    return _dot(_silu(y).astype(BF16), wpw_ref[...])


def _gate(val, g):
    return (val * _silu(g)).astype(BF16)


def _softmax_sink(s, sink):
    m = jnp.maximum(jnp.max(s, axis=-1, keepdims=True), sink)
    p = jnp.exp(s - m)
    den = jnp.sum(p, axis=-1, keepdims=True) + jnp.exp(sink - m)
    return p, 1.0 / den


def _cast_kernel(*refs):
    n = len(refs) // 2
    for src, dst in zip(refs[:n], refs[n:]):
        dst[...] = src[...].astype(BF16)


def _cast_weights(*ws):
    steps = CAST_STEPS
    assert all(w.shape[0] % (steps * 2 * SUBLANES) == 0 for w in ws)
    spec = lambda w: pl.BlockSpec((w.shape[0] // steps, w.shape[1]), lambda i: (i, 0))
    return pl.pallas_call(
        _cast_kernel,
        grid=(steps,),
        in_specs=[spec(w) for w in ws],
        out_specs=[spec(w) for w in ws],
        out_shape=[jax.ShapeDtypeStruct(w.shape, BF16) for w in ws],
        name="cast_weights",
    )(*ws)


def _prompt_kernel(x_ref, sink_ref, ng_ref, win_ref, dww_ref, dwb_ref, lng_ref, lnb_ref,
                   wpw_ref, wout_ref, fg_ref,
                   y_ref, kst_ref, vst_ref, cst_ref,
                   k_ref, vt_ref, up_ref, shifted_ref, wb_ref, bias_ref, attn_ref, conv_ref,
                   *, tile, final_norm):
    t = pl.program_id(1)
    d_attn = N_HEADS * HEAD_DIM
    d_kv = N_KV_HEADS * HEAD_DIM
    d_conv = up_ref.shape[1]
    n_blocks = tile // WINDOW
    two_w = 2 * WINDOW
    u_off = d_attn + 2 * d_kv + d_attn
    n_shift = shifted_ref.shape[1]

    @pl.when(t == 0)
    def _init():
        k_ref[:WINDOW, :] = jnp.zeros((WINDOW, d_kv), BF16)
        vt_ref[:, :WINDOW] = jnp.zeros((d_kv, WINDOW), BF16)
        up_ref[:CONV_PAD, :] = jnp.zeros((CONV_PAD, d_conv), F32)
        for j in range(CONV_W):
            wb_ref[j] = jnp.broadcast_to(dww_ref[j], (SUBLANES, d_conv))
        ji = lax.broadcasted_iota(jnp.int32, (two_w, two_w), 0)
        ci = lax.broadcasted_iota(jnp.int32, (two_w, two_w), 1)
        odd = ci >= WINDOW
        dist = jnp.where(odd, ci - WINDOW, ci) + WINDOW - ji
        valid = (dist >= 0) & (dist < WINDOW)
        valid_first = valid & (ji >= WINDOW)
        distf = dist.astype(F32) * LOG2E
        for pair in range(N_PAIRS):
            pen = -jnp.where(odd, _alibi_slope(2 * pair + 1), _alibi_slope(2 * pair)) * distf
            bias_ref[0, pair] = jnp.where(valid, pen, NEG_INF)
            bias_ref[1, pair] = jnp.where(valid_first, pen, NEG_INF)

    x = x_ref[...]
    h_in = _rms_norm(x, ng_ref[...]).astype(BF16)


    cw = d_conv // CONV_CH_BLOCKS
    for cb in range(CONV_CH_BLOCKS):
        ch = slice(cb * cw, (cb + 1) * cw)
        u_a = _dot(h_in, win_ref[:, u_off + cb * cw:u_off + (cb + 1) * cw])
        u_b = _dot(h_in, win_ref[:, u_off + d_conv + cb * cw:u_off + d_conv + (cb + 1) * cw])
        up_ref[CONV_PAD:, ch] = u_a * jax.nn.sigmoid(u_b)
        for s in range(1, SUBLANES):
            shifted_ref[s - 1, :, ch] = up_ref[s:s + n_shift, ch]
    for tau in range(CONV_TAIL):
        row = CONV_PAD + tile - CONV_TAIL + tau
        cst_ref[tau] = up_ref[row:row + 1, :]

    kv = _dot(h_in, win_ref[:, d_attn:d_attn + 2 * d_kv])
    k = kv[:, :d_kv]
    v = kv[:, d_kv:]
    k_ref[WINDOW:, :] = k.astype(BF16)
    v_t = v.T
    vt_ref[:, WINDOW:] = v_t.astype(BF16)
    kst_ref[...] = k[tile - WINDOW:, :].T
    vst_ref[...] = v_t[:, tile - WINDOW:]

    groups = CONV_ROWS // SUBLANES

    def conv_chunk(r, cb):
        ch = slice(cb * cw, (cb + 1) * cw)
        acc = [jnp.zeros((SUBLANES, cw), F32) for _ in range(groups)]
        loaded = {}
        for j in range(CONV_W):
            off = CONV_PAD - CONV_TAIL + j
            a, s = off // SUBLANES, off % SUBLANES
            wb = wb_ref[j, :, ch]
            for g in range(groups):
                if (s, a + g) not in loaded:
                    rows = slice(r + (a + g) * SUBLANES, r + (a + g + 1) * SUBLANES)
                    loaded[s, a + g] = up_ref[rows, ch] if s == 0 else shifted_ref[s - 1, rows, ch]
                acc[g] = acc[g] + loaded[s, a + g] * wb
        for g in range(groups):
            conv_ref[r + g * SUBLANES:r + (g + 1) * SUBLANES, ch] = acc[g]

    q = _dot(h_in, win_ref[:, :d_attn]) * (HEAD_DIM ** -0.5 * LOG2E)
    g_a = _dot(h_in, win_ref[:, d_attn + 2 * d_kv:u_off])
    g_c = _dot(h_in, win_ref[:, u_off + 2 * d_conv:])

    lo = lax.broadcasted_iota(jnp.int32, (WINDOW, LANES), 1) < HALF
    zero = jnp.zeros((WINDOW, LANES), F32)
    odd_head = lax.broadcasted_iota(jnp.int32, (1, two_w), 1) >= WINDOW

    def scores(blk, pair):
        r0 = blk * WINDOW
        first = jnp.where(t == 0, 1, 0) if blk == 0 else 0
        kvh = pair // PAIRS_PER_KV
        qp = q[r0:r0 + WINDOW, pair * LANES:(pair + 1) * LANES]
        rolled = pltpu.roll(qp, HALF, axis=1)
        if kvh == 0:
            q_even, q_odd = jnp.where(lo, qp, zero), jnp.where(lo, rolled, zero)
        else:
            q_even, q_odd = jnp.where(lo, zero, rolled), jnp.where(lo, zero, qp)
        qm = jnp.concatenate([q_even, q_odd], axis=0).astype(BF16)
        return _dot_nt(k_ref[r0:r0 + two_w, :], qm) + bias_ref[first, pair]

    def attend(blk, pair, s):
        r0 = blk * WINDOW
        kvh = pair // PAIRS_PER_KV
        sink = jnp.where(odd_head, sink_ref[2 * pair + 1], sink_ref[2 * pair]) * LOG2E
        m = jnp.maximum(jnp.max(s, axis=0, keepdims=True), sink)
        p = jnp.exp2(s - m)
        den = jnp.sum(p, axis=0, keepdims=True) + jnp.exp2(sink - m)
        vt_win = vt_ref[kvh * HEAD_DIM:(kvh + 1) * HEAD_DIM, r0:r0 + two_w]
        ot = _dot(vt_win, p.astype(BF16)) * (1.0 / den)
        o = jnp.concatenate([ot[:, :WINDOW], ot[:, WINDOW:]], axis=0).T
        attn_ref[r0:r0 + WINDOW, pair * LANES:(pair + 1) * LANES] = o

    steps = [(blk, pair) for blk in range(n_blocks) for pair in range(N_PAIRS)]
    s_next = scores(*steps[0])
    for i, step in enumerate(steps):
        s_cur = s_next
        if i + 1 < len(steps):
            s_next = scores(*steps[i + 1])
        attend(*step, s_cur)

    y_a = _dot(_gate(attn_ref[...], g_a), wout_ref[:d_attn, :])
    for cb in range(CONV_CH_BLOCKS):
        for r in range(0, tile, CONV_ROWS):
            conv_chunk(r, cb)

    c2 = _conv_branch_tail(conv_ref[...] + dwb_ref[...], lng_ref[...], lnb_ref[...], wpw_ref)
    y = x + y_a + _dot(_gate(c2, g_c), wout_ref[d_attn:, :])
    y_ref[...] = _rms_norm(y, fg_ref[...]) if final_norm else y

    k_ref[:WINDOW, :] = k_ref[tile:, :]
    vt_ref[:, :WINDOW] = vt_ref[:, tile:]
    up_ref[:CONV_PAD, :] = up_ref[tile:, :]


def _prompt_layer(x, sinks, ng, win, dww, dwb, lng, lnb, wpw, wout, fg, final_norm):
    B, T, D = x.shape
    tile = PROMPT_TILE
    assert T % tile == 0 and tile % WINDOW == 0 and tile % CONV_ROWS == 0
    d_conv = dww.shape[-1]
    d_kv = N_KV_HEADS * HEAD_DIM
    d_attn = N_HEADS * HEAD_DIM
    n_t = T // tile
    const = lambda shape: pl.BlockSpec(shape, lambda b, t: (0,) * len(shape))
    return pl.pallas_call(
        functools.partial(_prompt_kernel, tile=tile, final_norm=final_norm),
        grid=(B, n_t),
        in_specs=[
            pl.BlockSpec((None, tile, D), lambda b, t: (b, t, 0)),
            pl.BlockSpec(memory_space=pltpu.SMEM),
            const(ng.shape), const(win.shape), const(dww.shape), const(dwb.shape),
            const(lng.shape), const(lnb.shape), const(wpw.shape), const(wout.shape), const(fg.shape),
        ],
        out_specs=[
            pl.BlockSpec((None, tile, D), lambda b, t: (b, t, 0)),
            pl.BlockSpec((None, d_kv, WINDOW), lambda b, t: (b, 0, 0)),
            pl.BlockSpec((None, d_kv, WINDOW), lambda b, t: (b, 0, 0)),
            pl.BlockSpec((CONV_TAIL, 1, d_conv), lambda b, t: (0, b, 0)),
        ],
        scratch_shapes=[
            pltpu.VMEM((WINDOW + tile, d_kv), BF16),
            pltpu.VMEM((d_kv, WINDOW + tile), BF16),
            pltpu.VMEM((CONV_PAD + tile, d_conv), F32),
            pltpu.VMEM((SUBLANES - 1, CONV_PAD - SUBLANES + tile, d_conv), F32),
            pltpu.VMEM((CONV_W, SUBLANES, d_conv), F32),
            pltpu.VMEM((2, N_PAIRS, 2 * WINDOW, 2 * WINDOW), F32),
            pltpu.VMEM((tile, d_attn), F32),
            pltpu.VMEM((tile, d_conv), F32),
        ],
        out_shape=[
            jax.ShapeDtypeStruct((B, T, D), F32),
            jax.ShapeDtypeStruct((B, d_kv, WINDOW), F32),
            jax.ShapeDtypeStruct((B, d_kv, WINDOW), F32),
            jax.ShapeDtypeStruct((CONV_TAIL, B, d_conv), F32),
        ],
        compiler_params=pltpu.CompilerParams(
            dimension_semantics=("arbitrary", "arbitrary"), vmem_limit_bytes=VMEM_LIMIT_BYTES),
        name="prompt_layer",
    )(x, sinks, ng, win, dww, dwb, lng, lnb, wpw, wout, fg)


def _sample_kernel(x_ref, ck_ref, cv_ref, cs_ref, sink_ref, ng_ref, win_ref,
                   dww_ref, dwb_ref, lng_ref, lnb_ref, wpw_ref, wout_ref, fg_ref,
                   y_ref, ko_ref, vo_ref, co_ref,
                   q_ref, kn_ref, vn_ref, upt_ref, wb_ref, bias_ref, sinkcol_ref, attn_ref, conv_ref,
                   perm_ref, perm_t_ref, *, group, dec, final_norm):
    d_attn = N_HEADS * HEAD_DIM
    d_kv = N_KV_HEADS * HEAD_DIM
    d_conv = conv_ref.shape[1]
    n_keys = WINDOW + dec
    rows_q = N_HEADS * dec

    @pl.when(pl.program_id(0) == 0)
    def _init():
        ri = lax.broadcasted_iota(jnp.int32, (rows_q, n_keys), 0)
        si = lax.broadcasted_iota(jnp.int32, (rows_q, n_keys), 1)
        ti = ri
        slope = jnp.zeros((rows_q, n_keys), F32)
        for h in range(N_HEADS):
            in_head = (ri >= h * dec) & (ri < (h + 1) * dec)
            ti = jnp.where(in_head, ri - h * dec, ti)
            slope = jnp.where(in_head, _alibi_slope(h), slope)
        dist = WINDOW + ti - si
        valid = (dist >= 0) & (dist < WINDOW)
        distf = dist.astype(F32)
        bias_ref[...] = jnp.where(valid, -slope * distf, NEG_INF)
        rc = lax.broadcasted_iota(jnp.int32, (rows_q, 1), 0)
        sink_rows = jnp.zeros((rows_q, 1), F32)
        for h in range(N_HEADS):
            sink_rows = jnp.where((rc >= h * dec) & (rc < (h + 1) * dec), sink_ref[h], sink_rows)
        sinkcol_ref[...] = sink_rows
        for j in range(CONV_W):
            wb_ref[j] = jnp.broadcast_to(dww_ref[j], (SUBLANES, d_conv))
        n_rows = group * dec
        pi = lax.broadcasted_iota(jnp.int32, (n_rows, n_rows), 0)
        pj = lax.broadcasted_iota(jnp.int32, (n_rows, n_rows), 1)
        shift = group.bit_length() - 1
        source = lambda i: (i & (group - 1)) * dec + lax.shift_right_logical(i, shift)
        perm_ref[...] = jnp.where(pj == source(pi), 1.0, 0.0).astype(BF16)
        perm_t_ref[...] = jnp.where(pi == source(pj), 1.0, 0.0).astype(BF16)

    x = x_ref[...]
    h_in = _rms_norm(x, ng_ref[...]).astype(BF16)
    q_ref[...] = _dot(h_in, win_ref[:, :d_attn]) * (HEAD_DIM ** -0.5)
    kv = _dot(h_in, win_ref[:, d_attn:d_attn + 2 * d_kv])
    kn_ref[...] = kv[:, :d_kv]
    vn_ref[...] = kv[:, d_kv:]
    g_a = _dot(h_in, win_ref[:, d_attn + 2 * d_kv:d_attn + 2 * d_kv + d_attn])

    h_tm = _dot(perm_ref[...], h_in).astype(BF16)
    u_off = d_attn + 2 * d_kv + d_attn
    u_ab = _dot(h_tm, win_ref[:, u_off:u_off + 2 * d_conv])
    u_tm = u_ab[:, :d_conv] * jax.nn.sigmoid(u_ab[:, d_conv:])
    g_c_tm = _dot(h_tm, win_ref[:, u_off + 2 * d_conv:])

    lo = lax.broadcasted_iota(jnp.int32, (dec, LANES), 1) < HALF
    zero = jnp.zeros((dec, LANES), F32)
    sink = sinkcol_ref[...]
    bias = bias_ref[...]

    def scores(b):
        r = pl.multiple_of(b * dec, dec)
        k_all = jnp.concatenate([ck_ref[b].T, kn_ref[pl.ds(r, dec), :]], axis=0)
        ko_ref[b] = k_all[dec:, :]
        q_b = q_ref[pl.ds(r, dec), :]
        rows = []
        for h in range(N_HEADS):
            tile_q = q_b[:, (h // 2) * LANES:(h // 2 + 1) * LANES]
            in_lo = h % 2 == 0
            want_lo = h // GQA_GROUP == 0
            src = tile_q if in_lo == want_lo else pltpu.roll(tile_q, HALF, axis=1)
            rows.append(jnp.where(lo, src, zero) if want_lo else jnp.where(lo, zero, src))
        q_rows = jnp.concatenate(rows, axis=0).astype(BF16)
        return _dot_nt(q_rows, k_all.astype(BF16)) + bias

    def attend(b, s):
        r = pl.multiple_of(b * dec, dec)
        v_all = jnp.concatenate([cv_ref[b].T, vn_ref[pl.ds(r, dec), :]], axis=0)
        vo_ref[b] = v_all[dec:, :]
        p, rden = _softmax_sink(s, sink)
        o = _dot(p.astype(BF16), v_all.astype(BF16)) * rden
        tiles = []
        for pair in range(N_PAIRS):
            a = o[(2 * pair) * dec:(2 * pair + 1) * dec, :]
            c = o[(2 * pair + 1) * dec:(2 * pair + 2) * dec, :]
            if pair // PAIRS_PER_KV == 0:
                tiles.append(jnp.where(lo, a, pltpu.roll(c, HALF, axis=1)))
            else:
                tiles.append(jnp.where(lo, pltpu.roll(a, HALF, axis=1), c))
        attn_ref[pl.ds(r, dec), :] = jnp.concatenate(tiles, axis=1)

    upt_ref[:CONV_TAIL] = cs_ref[...]
    for tt in range(dec):
        upt_ref[CONV_TAIL + tt] = u_tm[tt * group:(tt + 1) * group, :]
    co_ref[...] = upt_ref[dec:]

    def conv_step(tt):
        for g0 in range(0, group, SUBLANES):
            acc = jnp.zeros((SUBLANES, d_conv), F32)
            for j in range(CONV_W):
                acc = acc + upt_ref[tt + j, g0:g0 + SUBLANES, :] * wb_ref[j]
            start = tt * group + g0
            if not isinstance(start, int):
                start = pl.multiple_of(start, SUBLANES)
            conv_ref[pl.ds(start, SUBLANES), :] = acc

    trips = group // SAMPLE_UNROLL
    conv_per_trip = dec // trips

    def attention_batch(i, carry):
        elems = [i * SAMPLE_UNROLL + e for e in range(SAMPLE_UNROLL)]
        ss = [scores(b) for b in elems]
        for b, s in zip(elems, ss):
            attend(b, s)
        for c in range(conv_per_trip):
            conv_step(i * conv_per_trip + c)
        return carry

    lax.fori_loop(0, trips, attention_batch, 0)
    for tt in range(trips * conv_per_trip, dec):
        conv_step(tt)

    y_a = _dot(_gate(attn_ref[...], g_a), wout_ref[:d_attn, :])
    c2_tm = _conv_branch_tail(conv_ref[...] + dwb_ref[...], lng_ref[...], lnb_ref[...], wpw_ref)
    m_c = _dot(perm_t_ref[...], _gate(c2_tm, g_c_tm)).astype(BF16)
    y = x + y_a + _dot(m_c, wout_ref[d_attn:, :])
    y_ref[...] = _rms_norm(y, fg_ref[...]) if final_norm else y


def _sample_layer(x, ck, cv, cs, sinks, ng, win, dww, dwb, lng, lnb, wpw, wout, fg, final_norm):
    Bs, dec, D = x.shape
    group = SAMPLE_GROUP
    assert Bs % group == 0 and dec % 8 == 0 and dec <= CONV_TAIL
    d_conv = dww.shape[-1]
    d_kv = N_KV_HEADS * HEAD_DIM
    d_attn = N_HEADS * HEAD_DIM
    rows = group * dec
    x2 = x.reshape(Bs * dec, D)
    assert group & (group - 1) == 0
    const = lambda shape: pl.BlockSpec(shape, lambda g: (0,) * len(shape))
    per_group = lambda shape: pl.BlockSpec((group,) + shape, lambda g: (g,) + (0,) * len(shape))
    conv_state = pl.BlockSpec((CONV_TAIL, group, d_conv), lambda g: (0, g, 0))
    y, ko, vo, co = pl.pallas_call(
        functools.partial(_sample_kernel, group=group, dec=dec, final_norm=final_norm),
        grid=(Bs // group,),
        in_specs=[
            pl.BlockSpec((rows, D), lambda g: (g, 0)),
            per_group((d_kv, WINDOW)), per_group((d_kv, WINDOW)), conv_state,
            pl.BlockSpec(memory_space=pltpu.SMEM),
            const(ng.shape), const(win.shape), const(dww.shape), const(dwb.shape),
            const(lng.shape), const(lnb.shape), const(wpw.shape), const(wout.shape), const(fg.shape),
        ],
        out_specs=[
            pl.BlockSpec((rows, D), lambda g: (g, 0)),
            per_group((WINDOW, d_kv)), per_group((WINDOW, d_kv)), conv_state,
        ],
        out_shape=[
            jax.ShapeDtypeStruct((Bs * dec, D), F32),
            jax.ShapeDtypeStruct((Bs, WINDOW, d_kv), F32),
            jax.ShapeDtypeStruct((Bs, WINDOW, d_kv), F32),
            jax.ShapeDtypeStruct((CONV_TAIL, Bs, d_conv), F32),
        ],
        scratch_shapes=[
            pltpu.VMEM((rows, d_attn), F32),
            pltpu.VMEM((rows, d_kv), F32),
            pltpu.VMEM((rows, d_kv), F32),
            pltpu.VMEM((CONV_TAIL + dec, group, d_conv), F32),
            pltpu.VMEM((CONV_W, SUBLANES, d_conv), F32),
            pltpu.VMEM((N_HEADS * dec, WINDOW + dec), F32),
            pltpu.VMEM((N_HEADS * dec, 1), F32),
            pltpu.VMEM((rows, d_attn), F32),
            pltpu.VMEM((rows, d_conv), F32),
            pltpu.VMEM((rows, rows), BF16),
            pltpu.VMEM((rows, rows), BF16),
        ],
        compiler_params=pltpu.CompilerParams(
            dimension_semantics=("arbitrary",), vmem_limit_bytes=VMEM_LIMIT_BYTES),
        name="sample_layer",
    )(x2, ck, cv, cs, sinks, ng, win, dww, dwb, lng, lnb, wpw, wout, fg)
    return y.reshape(Bs, dec, D), ko, vo, co


def kernel(x_prompt, x_sample, cache_k, cache_v, state_conv, norm_g, w_in, attn_sinks, dw_w, dw_b,
           conv_ln_g, conv_ln_b, w_pw2, w_out, final_norm_g):
    depth = w_in.shape[0]
    B = x_prompt.shape[0]
    Bs, dec = x_sample.shape[0], x_sample.shape[1]
    d_kv = N_KV_HEADS * HEAD_DIM
    fg = final_norm_g.reshape(1, -1)
    hp, hs = x_prompt, x_sample
    pk, pv, pc, sk, sv, sc = [], [], [], [], [], []
    kv_major = lambda c: c.transpose(0, 2, 3, 1).reshape(Bs, d_kv, WINDOW)
    for l in range(depth):
        final_norm = l == depth - 1
        row = lambda a: a[l].reshape(1, -1)
        win, wpw, wout = _cast_weights(w_in[l], w_pw2[l], w_out[l])
        shared = (row(norm_g), win, dw_w[l][:, None, :], row(dw_b), row(conv_ln_g), row(conv_ln_b),
                  wpw, wout, fg)
        hp, k_p, v_p, c_p = _prompt_layer(hp, attn_sinks[l], *shared, final_norm)
        hs, k_s, v_s, c_s = _sample_layer(
            hs, kv_major(cache_k[l]), kv_major(cache_v[l]),
            state_conv[l].swapaxes(0, 1), attn_sinks[l], *shared, final_norm)
        key_major = lambda s: s.reshape(B, N_KV_HEADS, HEAD_DIM, WINDOW).transpose(0, 3, 1, 2)
        pk.append(key_major(k_p))
        pv.append(key_major(v_p))
        pc.append(c_p.swapaxes(0, 1))
        sk.append(k_s.reshape(Bs, WINDOW, N_KV_HEADS, HEAD_DIM))
        sv.append(v_s.reshape(Bs, WINDOW, N_KV_HEADS, HEAD_DIM))
        sc.append(c_s.swapaxes(0, 1))
    stack = lambda xs: xs[0][None] if len(xs) == 1 else jnp.stack(xs)
    return (hp, hs, stack(pk), stack(pv), stack(pc), stack(sk), stack(sv), stack(sc))
```

```python
import functools

import jax
import jax.numpy as jnp
from jax import lax
from jax.experimental import pallas as pl
from jax.experimental.pallas import tpu as pltpu

HEAD_DIM = 64
N_HEADS = 8
N_KV_HEADS = 2
GQA_GROUP = N_HEADS // N_KV_HEADS
N_PAIRS = N_HEADS // 2
PAIRS_PER_KV = N_PAIRS // N_KV_HEADS
WINDOW = 128
CONV_W = 31
CONV_TAIL = CONV_W - 1
EPS = 1e-5
LANES = 128
SUBLANES = 8
HALF = LANES // 2
NEG_INF = float("-inf")
LOG2E = 1.4426950408889634

PROMPT_TILE = 1024
CONV_ROWS = 64
CONV_CH_BLOCKS = 2
CONV_PAD = 32
CAST_STEPS = 4
SAMPLE_GROUP = 32
SAMPLE_FUSE = 2
SAMPLE_UNROLL = 4
VMEM_LIMIT_BYTES = 56 * 1024 * 1024

F32 = jnp.float32
BF16 = jnp.bfloat16


def _alibi_slope(h):
    return 2.0 ** (-8.0 * (h + 1) / N_HEADS)


def _rms_norm(x, g):
    ms = jnp.mean(x * x, axis=-1, keepdims=True)
    return x * lax.rsqrt(ms + EPS) * g


def _silu(x):
    return x * jax.nn.sigmoid(x)


def _dot(a, b):
    return jnp.dot(a, b, preferred_element_type=F32)


def _dot_nt(a, b):
    return lax.dot_general(a, b, (((1,), (1,)), ((), ())), preferred_element_type=F32)


def _conv_branch_tail(c, lng, lnb, wpw_ref):
    mu = jnp.mean(c, axis=-1, keepdims=True)
    xc = c - mu
    var = jnp.mean(xc * xc, axis=-1, keepdims=True)
    y = xc * lax.rsqrt(var + EPS) * lng + lnb
    return _dot(_silu(y).astype(BF16), wpw_ref[...])


def _gate(val, g):
    return (val * _silu(g)).astype(BF16)


def _softmax_sink(s, sink):
    m = jnp.maximum(jnp.max(s, axis=-1, keepdims=True), sink)
    p = jnp.exp(s - m)
    den = jnp.sum(p, axis=-1, keepdims=True) + jnp.exp(sink - m)
    return p, 1.0 / den


def _cast_kernel(*refs):
    n = len(refs) // 2
    for src, dst in zip(refs[:n], refs[n:]):
        dst[...] = src[...].astype(BF16)


def _cast_weights(*ws):
    steps = CAST_STEPS
    assert all(w.shape[0] % (steps * 2 * SUBLANES) == 0 for w in ws)
    spec = lambda w: pl.BlockSpec((w.shape[0] // steps, w.shape[1]), lambda i: (i, 0))
    return pl.pallas_call(
        _cast_kernel,
        grid=(steps,),
        in_specs=[spec(w) for w in ws],
        out_specs=[spec(w) for w in ws],
        out_shape=[jax.ShapeDtypeStruct(w.shape, BF16) for w in ws],
        name="cast_weights",
    )(*ws)


def _prompt_kernel(x_ref, sink_ref, ng_ref, win_ref, dww_ref, dwb_ref, lng_ref, lnb_ref,
                   wpw_ref, wout_ref, fg_ref,
                   y_ref, kst_ref, vst_ref, cst_ref,
                   k_ref, vt_ref, up_ref, shifted_ref, wb_ref, bias_ref, attn_ref, conv_ref,
                   *, tile, final_norm):
    t = pl.program_id(1)
    d_attn = N_HEADS * HEAD_DIM
    d_kv = N_KV_HEADS * HEAD_DIM
    d_conv = up_ref.shape[1]
    n_blocks = tile // WINDOW
    two_w = 2 * WINDOW
    u_off = d_attn + 2 * d_kv + d_attn
    n_shift = shifted_ref.shape[1]

    @pl.when(t == 0)
    def _init():
        k_ref[:WINDOW, :] = jnp.zeros((WINDOW, d_kv), BF16)
        vt_ref[:, :WINDOW] = jnp.zeros((d_kv, WINDOW), BF16)
        up_ref[:CONV_PAD, :] = jnp.zeros((CONV_PAD, d_conv), F32)
        for j in range(CONV_W):
            wb_ref[j] = jnp.broadcast_to(dww_ref[j], (SUBLANES, d_conv))
        ji = lax.broadcasted_iota(jnp.int32, (two_w, two_w), 0)
        ci = lax.broadcasted_iota(jnp.int32, (two_w, two_w), 1)
        odd = ci >= WINDOW
        dist = jnp.where(odd, ci - WINDOW, ci) + WINDOW - ji
        valid = (dist >= 0) & (dist < WINDOW)
        valid_first = valid & (ji >= WINDOW)
        distf = dist.astype(F32) * LOG2E
        for pair in range(N_PAIRS):
            pen = -jnp.where(odd, _alibi_slope(2 * pair + 1), _alibi_slope(2 * pair)) * distf
            bias_ref[0, pair] = jnp.where(valid, pen, NEG_INF)
            bias_ref[1, pair] = jnp.where(valid_first, pen, NEG_INF)

    x = x_ref[...]
    h_in = _rms_norm(x, ng_ref[...]).astype(BF16)


    cw = d_conv // CONV_CH_BLOCKS
    for cb in range(CONV_CH_BLOCKS):
        ch = slice(cb * cw, (cb + 1) * cw)
        u_a = _dot(h_in, win_ref[:, u_off + cb * cw:u_off + (cb + 1) * cw])
        u_b = _dot(h_in, win_ref[:, u_off + d_conv + cb * cw:u_off + d_conv + (cb + 1) * cw])
        up_ref[CONV_PAD:, ch] = u_a * jax.nn.sigmoid(u_b)
        for s in range(1, SUBLANES):
            shifted_ref[s - 1, :, ch] = up_ref[s:s + n_shift, ch]
    for tau in range(CONV_TAIL):
        row = CONV_PAD + tile - CONV_TAIL + tau
        cst_ref[tau] = up_ref[row:row + 1, :]

    kv = _dot(h_in, win_ref[:, d_attn:d_attn + 2 * d_kv])
    k = kv[:, :d_kv]
    v = kv[:, d_kv:]
    k_ref[WINDOW:, :] = k.astype(BF16)
    v_t = v.T
    vt_ref[:, WINDOW:] = v_t.astype(BF16)
    kst_ref[...] = k[tile - WINDOW:, :].T
    vst_ref[...] = v_t[:, tile - WINDOW:]

    groups = CONV_ROWS // SUBLANES

    def conv_chunk(r, cb):
        ch = slice(cb * cw, (cb + 1) * cw)
        acc = [jnp.zeros((SUBLANES, cw), F32) for _ in range(groups)]
        loaded = {}
        for j in range(CONV_W):
            off = CONV_PAD - CONV_TAIL + j
            a, s = off // SUBLANES, off % SUBLANES
            wb = wb_ref[j, :, ch]
            for g in range(groups):
                if (s, a + g) not in loaded:
                    rows = slice(r + (a + g) * SUBLANES, r + (a + g + 1) * SUBLANES)
                    loaded[s, a + g] = up_ref[rows, ch] if s == 0 else shifted_ref[s - 1, rows, ch]
                acc[g] = acc[g] + loaded[s, a + g] * wb
        for g in range(groups):
            conv_ref[r + g * SUBLANES:r + (g + 1) * SUBLANES, ch] = acc[g]

    q = _dot(h_in, win_ref[:, :d_attn]) * (HEAD_DIM ** -0.5 * LOG2E)
    g_a = _dot(h_in, win_ref[:, d_attn + 2 * d_kv:u_off])
    g_c = _dot(h_in, win_ref[:, u_off + 2 * d_conv:])

    lo = lax.broadcasted_iota(jnp.int32, (WINDOW, LANES), 1) < HALF
    zero = jnp.zeros((WINDOW, LANES), F32)
    odd_head = lax.broadcasted_iota(jnp.int32, (1, two_w), 1) >= WINDOW

    def scores(blk, pair):
        r0 = blk * WINDOW
        first = jnp.where(t == 0, 1, 0) if blk == 0 else 0
        kvh = pair // PAIRS_PER_KV
        qp = q[r0:r0 + WINDOW, pair * LANES:(pair + 1) * LANES]
        rolled = pltpu.roll(qp, HALF, axis=1)
        if kvh == 0:
            q_even, q_odd = jnp.where(lo, qp, zero), jnp.where(lo, rolled, zero)
        else:
            q_even, q_odd = jnp.where(lo, zero, rolled), jnp.where(lo, zero, qp)
        qm = jnp.concatenate([q_even, q_odd], axis=0).astype(BF16)
        return _dot_nt(k_ref[r0:r0 + two_w, :], qm) + bias_ref[first, pair]

    def attend(blk, pair, s):
        r0 = blk * WINDOW
        kvh = pair // PAIRS_PER_KV
        sink = jnp.where(odd_head, sink_ref[2 * pair + 1], sink_ref[2 * pair]) * LOG2E
        m = jnp.maximum(jnp.max(s, axis=0, keepdims=True), sink)
        p = jnp.exp2(s - m)
        den = jnp.sum(p, axis=0, keepdims=True) + jnp.exp2(sink - m)
        vt_win = vt_ref[kvh * HEAD_DIM:(kvh + 1) * HEAD_DIM, r0:r0 + two_w]
        ot = _dot(vt_win, p.astype(BF16)) * (1.0 / den)
        o = jnp.concatenate([ot[:, :WINDOW], ot[:, WINDOW:]], axis=0).T
        attn_ref[r0:r0 + WINDOW, pair * LANES:(pair + 1) * LANES] = o

    steps = [(blk, pair) for blk in range(n_blocks) for pair in range(N_PAIRS)]
    s_next = scores(*steps[0])
    for i, step in enumerate(steps):
        s_cur = s_next
        if i + 1 < len(steps):
            s_next = scores(*steps[i + 1])
        attend(*step, s_cur)

    y_a = _dot(_gate(attn_ref[...], g_a), wout_ref[:d_attn, :])
    for cb in range(CONV_CH_BLOCKS):
        for r in range(0, tile, CONV_ROWS):
            conv_chunk(r, cb)

    c2 = _conv_branch_tail(conv_ref[...] + dwb_ref[...], lng_ref[...], lnb_ref[...], wpw_ref)
    y = x + y_a + _dot(_gate(c2, g_c), wout_ref[d_attn:, :])
    y_ref[...] = _rms_norm(y, fg_ref[...]) if final_norm else y

    k_ref[:WINDOW, :] = k_ref[tile:, :]
    vt_ref[:, :WINDOW] = vt_ref[:, tile:]
    up_ref[:CONV_PAD, :] = up_ref[tile:, :]


def _prompt_layer(x, sinks, ng, win, dww, dwb, lng, lnb, wpw, wout, fg, final_norm):
    B, T, D = x.shape
    tile = PROMPT_TILE
    assert T % tile == 0 and tile % WINDOW == 0 and tile % CONV_ROWS == 0
    d_conv = dww.shape[-1]
    d_kv = N_KV_HEADS * HEAD_DIM
    d_attn = N_HEADS * HEAD_DIM
    n_t = T // tile
    const = lambda shape: pl.BlockSpec(shape, lambda b, t: (0,) * len(shape))
    return pl.pallas_call(
        functools.partial(_prompt_kernel, tile=tile, final_norm=final_norm),
        grid=(B, n_t),
        in_specs=[
            pl.BlockSpec((None, tile, D), lambda b, t: (b, t, 0)),
            pl.BlockSpec(memory_space=pltpu.SMEM),
            const(ng.shape), const(win.shape), const(dww.shape), const(dwb.shape),
            const(lng.shape), const(lnb.shape), const(wpw.shape), const(wout.shape), const(fg.shape),
        ],
        out_specs=[
            pl.BlockSpec((None, tile, D), lambda b, t: (b, t, 0)),
            pl.BlockSpec((None, d_kv, WINDOW), lambda b, t: (b, 0, 0)),
            pl.BlockSpec((None, d_kv, WINDOW), lambda b, t: (b, 0, 0)),
            pl.BlockSpec((CONV_TAIL, 1, d_conv), lambda b, t: (0, b, 0)),
        ],
        scratch_shapes=[
            pltpu.VMEM((WINDOW + tile, d_kv), BF16),
            pltpu.VMEM((d_kv, WINDOW + tile), BF16),
            pltpu.VMEM((CONV_PAD + tile, d_conv), F32),
            pltpu.VMEM((SUBLANES - 1, CONV_PAD - SUBLANES + tile, d_conv), F32),
            pltpu.VMEM((CONV_W, SUBLANES, d_conv), F32),
            pltpu.VMEM((2, N_PAIRS, 2 * WINDOW, 2 * WINDOW), F32),
            pltpu.VMEM((tile, d_attn), F32),
            pltpu.VMEM((tile, d_conv), F32),
        ],
        out_shape=[
            jax.ShapeDtypeStruct((B, T, D), F32),
            jax.ShapeDtypeStruct((B, d_kv, WINDOW), F32),
            jax.ShapeDtypeStruct((B, d_kv, WINDOW), F32),
            jax.ShapeDtypeStruct((CONV_TAIL, B, d_conv), F32),
        ],
        compiler_params=pltpu.CompilerParams(
            dimension_semantics=("arbitrary", "arbitrary"), vmem_limit_bytes=VMEM_LIMIT_BYTES),
        name="prompt_layer",
    )(x, sinks, ng, win, dww, dwb, lng, lnb, wpw, wout, fg)


def _sample_kernel(x_ref, ck_ref, cv_ref, cs_ref, sink_ref, ng_ref, win_ref,
                   dww_ref, dwb_ref, lng_ref, lnb_ref, wpw_ref, wout_ref, fg_ref,
                   y_ref, ko_ref, vo_ref, co_ref,
                   q_ref, kn_ref, vn_ref, upt_ref, wb_ref, bias_ref, sinkcol_ref, attn_ref, conv_ref,
                   perm_ref, perm_t_ref, *, group, dec, final_norm):
    d_attn = N_HEADS * HEAD_DIM
    d_kv = N_KV_HEADS * HEAD_DIM
    d_conv = conv_ref.shape[1]
    n_keys = WINDOW + dec
    rows_q = N_HEADS * dec

    @pl.when(pl.program_id(0) == 0)
    def _init():
        shape = (SAMPLE_FUSE * rows_q, SAMPLE_FUSE * n_keys)
        n_cache = SAMPLE_FUSE * WINDOW
        log2 = lambda n: n.bit_length() - 1
        ri = lax.broadcasted_iota(jnp.int32, shape, 0)
        ci = lax.broadcasted_iota(jnp.int32, shape, 1)
        row_elem = lax.shift_right_logical(ri, log2(rows_q))
        row_head = lax.shift_right_logical(ri & (rows_q - 1), log2(dec))
        row_tok = ri & (dec - 1)
        is_cache = ci < n_cache
        cn = ci - n_cache
        col_elem = jnp.where(is_cache, lax.shift_right_logical(ci, log2(WINDOW)),
                             lax.shift_right_logical(cn, log2(dec)))
        dist = jnp.where(is_cache, WINDOW + row_tok - (ci & (WINDOW - 1)), row_tok - (cn & (dec - 1)))
        valid = (row_elem == col_elem) & (dist >= 0) & (dist < WINDOW)
        slope = jnp.zeros(shape, F32)
        for h in range(N_HEADS):
            slope = jnp.where(row_head == h, _alibi_slope(h), slope)
        bias_ref[...] = jnp.where(valid, -slope * dist.astype(F32), NEG_INF)
        rc = lax.broadcasted_iota(jnp.int32, (shape[0], 1), 0)
        rc_head = lax.shift_right_logical(rc & (rows_q - 1), log2(dec))
        sink_rows = jnp.zeros((shape[0], 1), F32)
        for h in range(N_HEADS):
            sink_rows = jnp.where(rc_head == h, sink_ref[h], sink_rows)
        sinkcol_ref[...] = sink_rows
        for j in range(CONV_W):
            wb_ref[j] = jnp.broadcast_to(dww_ref[j], (SUBLANES, d_conv))
        n_rows = group * dec
        pi = lax.broadcasted_iota(jnp.int32, (n_rows, n_rows), 0)
        pj = lax.broadcasted_iota(jnp.int32, (n_rows, n_rows), 1)
        shift = group.bit_length() - 1
        source = lambda i: (i & (group - 1)) * dec + lax.shift_right_logical(i, shift)
        perm_ref[...] = jnp.where(pj == source(pi), 1.0, 0.0).astype(BF16)
        perm_t_ref[...] = jnp.where(pi == source(pj), 1.0, 0.0).astype(BF16)

    x = x_ref[...]
    h_in = _rms_norm(x, ng_ref[...]).astype(BF16)
    q_ref[...] = _dot(h_in, win_ref[:, :d_attn]) * (HEAD_DIM ** -0.5)
    kv = _dot(h_in, win_ref[:, d_attn:d_attn + 2 * d_kv])
    kn_ref[...] = kv[:, :d_kv]
    vn_ref[...] = kv[:, d_kv:]
    g_a = _dot(h_in, win_ref[:, d_attn + 2 * d_kv:d_attn + 2 * d_kv + d_attn])

    h_tm = _dot(perm_ref[...], h_in).astype(BF16)
    u_off = d_attn + 2 * d_kv + d_attn
    u_ab = _dot(h_tm, win_ref[:, u_off:u_off + 2 * d_conv])
    u_tm = u_ab[:, :d_conv] * jax.nn.sigmoid(u_ab[:, d_conv:])
    g_c_tm = _dot(h_tm, win_ref[:, u_off + 2 * d_conv:])

    lo = lax.broadcasted_iota(jnp.int32, (dec, LANES), 1) < HALF
    zero = jnp.zeros((dec, LANES), F32)
    sink = sinkcol_ref[...]
    bias = bias_ref[...]

    def keys_of(bs, cache_ref, new_ref, state_ref):
        cached = [cache_ref[b].T for b in bs]
        new = [new_ref[pl.ds(pl.multiple_of(b * dec, dec), dec), :] for b in bs]
        for b, c, n in zip(bs, cached, new):
            state_ref[b] = jnp.concatenate([c[dec:, :], n], axis=0)
        return jnp.concatenate(cached + new, axis=0).astype(BF16)

    def scores(bs):
        rows = []
        for b in bs:
            q_b = q_ref[pl.ds(pl.multiple_of(b * dec, dec), dec), :]
            for h in range(N_HEADS):
                tile_q = q_b[:, (h // 2) * LANES:(h // 2 + 1) * LANES]
                in_lo = h % 2 == 0
                want_lo = h // GQA_GROUP == 0
                src = tile_q if in_lo == want_lo else pltpu.roll(tile_q, HALF, axis=1)
                rows.append(jnp.where(lo, src, zero) if want_lo else jnp.where(lo, zero, src))
        q_rows = jnp.concatenate(rows, axis=0).astype(BF16)
        return _dot_nt(q_rows, keys_of(bs, ck_ref, kn_ref, ko_ref)) + bias

    def attend(bs, s):
        p, rden = _softmax_sink(s, sink)
        o_all = _dot(p.astype(BF16), keys_of(bs, cv_ref, vn_ref, vo_ref)) * rden
        for e, b in enumerate(bs):
            o = o_all[e * rows_q:(e + 1) * rows_q, :]
            tiles = []
            for pair in range(N_PAIRS):
                a = o[(2 * pair) * dec:(2 * pair + 1) * dec, :]
                c = o[(2 * pair + 1) * dec:(2 * pair + 2) * dec, :]
                if pair // PAIRS_PER_KV == 0:
                    tiles.append(jnp.where(lo, a, pltpu.roll(c, HALF, axis=1)))
                else:
                    tiles.append(jnp.where(lo, pltpu.roll(a, HALF, axis=1), c))
            attn_ref[pl.ds(pl.multiple_of(b * dec, dec), dec), :] = jnp.concatenate(tiles, axis=1)

    upt_ref[:CONV_TAIL] = cs_ref[...]
    for tt in range(dec):
        upt_ref[CONV_TAIL + tt] = u_tm[tt * group:(tt + 1) * group, :]
    co_ref[...] = upt_ref[dec:]

    def conv_step(tt):
        for g0 in range(0, group, SUBLANES):
            acc = jnp.zeros((SUBLANES, d_conv), F32)
            for j in range(CONV_W):
                acc = acc + upt_ref[tt + j, g0:g0 + SUBLANES, :] * wb_ref[j]
            start = tt * group + g0
            if not isinstance(start, int):
                start = pl.multiple_of(start, SUBLANES)
            conv_ref[pl.ds(start, SUBLANES), :] = acc

    trips = group // SAMPLE_UNROLL
    conv_per_trip = dec // trips

    def attention_batch(i, carry):
        chains = [[i * SAMPLE_UNROLL + c * SAMPLE_FUSE + e for e in range(SAMPLE_FUSE)]
                  for c in range(SAMPLE_UNROLL // SAMPLE_FUSE)]
        ss = [scores(bs) for bs in chains]
        for bs, s in zip(chains, ss):
            attend(bs, s)
        for c in range(conv_per_trip):
            conv_step(i * conv_per_trip + c)
        return carry

    lax.fori_loop(0, trips, attention_batch, 0)
    for tt in range(trips * conv_per_trip, dec):
        conv_step(tt)

    y_a = _dot(_gate(attn_ref[...], g_a), wout_ref[:d_attn, :])
    c2_tm = _conv_branch_tail(conv_ref[...] + dwb_ref[...], lng_ref[...], lnb_ref[...], wpw_ref)
    m_c = _dot(perm_t_ref[...], _gate(c2_tm, g_c_tm)).astype(BF16)
    y = x + y_a + _dot(m_c, wout_ref[d_attn:, :])
    y_ref[...] = _rms_norm(y, fg_ref[...]) if final_norm else y


def _sample_layer(x, ck, cv, cs, sinks, ng, win, dww, dwb, lng, lnb, wpw, wout, fg, final_norm):
    Bs, dec, D = x.shape
    group = SAMPLE_GROUP
    assert Bs % group == 0 and dec % 8 == 0 and dec <= CONV_TAIL
    d_conv = dww.shape[-1]
    d_kv = N_KV_HEADS * HEAD_DIM
    d_attn = N_HEADS * HEAD_DIM
    rows = group * dec
    x2 = x.reshape(Bs * dec, D)
    assert group & (group - 1) == 0
    const = lambda shape: pl.BlockSpec(shape, lambda g: (0,) * len(shape))
    per_group = lambda shape: pl.BlockSpec((group,) + shape, lambda g: (g,) + (0,) * len(shape))
    conv_state = pl.BlockSpec((CONV_TAIL, group, d_conv), lambda g: (0, g, 0))
    y, ko, vo, co = pl.pallas_call(
        functools.partial(_sample_kernel, group=group, dec=dec, final_norm=final_norm),
        grid=(Bs // group,),
        in_specs=[
            pl.BlockSpec((rows, D), lambda g: (g, 0)),
            per_group((d_kv, WINDOW)), per_group((d_kv, WINDOW)), conv_state,
            pl.BlockSpec(memory_space=pltpu.SMEM),
            const(ng.shape), const(win.shape), const(dww.shape), const(dwb.shape),
            const(lng.shape), const(lnb.shape), const(wpw.shape), const(wout.shape), const(fg.shape),
        ],
        out_specs=[
            pl.BlockSpec((rows, D), lambda g: (g, 0)),
            per_group((WINDOW, d_kv)), per_group((WINDOW, d_kv)), conv_state,
        ],
        out_shape=[
            jax.ShapeDtypeStruct((Bs * dec, D), F32),
            jax.ShapeDtypeStruct((Bs, WINDOW, d_kv), F32),
            jax.ShapeDtypeStruct((Bs, WINDOW, d_kv), F32),
            jax.ShapeDtypeStruct((CONV_TAIL, Bs, d_conv), F32),
        ],
        scratch_shapes=[
            pltpu.VMEM((rows, d_attn), F32),
            pltpu.VMEM((rows, d_kv), F32),
            pltpu.VMEM((rows, d_kv), F32),
            pltpu.VMEM((CONV_TAIL + dec, group, d_conv), F32),
            pltpu.VMEM((CONV_W, SUBLANES, d_conv), F32),
            pltpu.VMEM((SAMPLE_FUSE * N_HEADS * dec, SAMPLE_FUSE * (WINDOW + dec)), F32),
            pltpu.VMEM((SAMPLE_FUSE * N_HEADS * dec, 1), F32),
            pltpu.VMEM((rows, d_attn), F32),
            pltpu.VMEM((rows, d_conv), F32),
            pltpu.VMEM((rows, rows), BF16),
            pltpu.VMEM((rows, rows), BF16),
        ],
        compiler_params=pltpu.CompilerParams(
            dimension_semantics=("arbitrary",), vmem_limit_bytes=VMEM_LIMIT_BYTES),
        name="sample_layer",
    )(x2, ck, cv, cs, sinks, ng, win, dww, dwb, lng, lnb, wpw, wout, fg)
    return y.reshape(Bs, dec, D), ko, vo, co


def kernel(x_prompt, x_sample, cache_k, cache_v, state_conv, norm_g, w_in, attn_sinks, dw_w, dw_b,
           conv_ln_g, conv_ln_b, w_pw2, w_out, final_norm_g):
    depth = w_in.shape[0]
    B = x_prompt.shape[0]
    Bs, dec = x_sample.shape[0], x_sample.shape[1]
    d_kv = N_KV_HEADS * HEAD_DIM
    fg = final_norm_g.reshape(1, -1)
    hp, hs = x_prompt, x_sample
    pk, pv, pc, sk, sv, sc = [], [], [], [], [], []
    kv_major = lambda c: c.transpose(0, 2, 3, 1).reshape(Bs, d_kv, WINDOW)
    for l in range(depth):
        final_norm = l == depth - 1
        row = lambda a: a[l].reshape(1, -1)
        win, wpw, wout = _cast_weights(w_in[l], w_pw2[l], w_out[l])
        shared = (row(norm_g), win, dw_w[l][:, None, :], row(dw_b), row(conv_ln_g), row(conv_ln_b),
                  wpw, wout, fg)
        hp, k_p, v_p, c_p = _prompt_layer(hp, attn_sinks[l], *shared, final_norm)
        hs, k_s, v_s, c_s = _sample_layer(
            hs, kv_major(cache_k[l]), kv_major(cache_v[l]),
            state_conv[l].swapaxes(0, 1), attn_sinks[l], *shared, final_norm)
        key_major = lambda s: s.reshape(B, N_KV_HEADS, HEAD_DIM, WINDOW).transpose(0, 3, 1, 2)
        pk.append(key_major(k_p))
        pv.append(key_major(v_p))
        pc.append(c_p.swapaxes(0, 1))
        sk.append(k_s.reshape(Bs, WINDOW, N_KV_HEADS, HEAD_DIM))
        sv.append(v_s.reshape(Bs, WINDOW, N_KV_HEADS, HEAD_DIM))
        sc.append(c_s.swapaxes(0, 1))
    stack = lambda xs: xs[0][None] if len(xs) == 1 else jnp.stack(xs)
    return (hp, hs, stack(pk), stack(pv), stack(pc), stack(sk), stack(sv), stack(sc))
```

```python
import functools

import jax
import jax.numpy as jnp
from jax import lax
from jax.experimental import pallas as pl
from jax.experimental.pallas import tpu as pltpu

HEAD_DIM = 64
N_HEADS = 8
N_KV_HEADS = 2
GQA_GROUP = N_HEADS // N_KV_HEADS
N_PAIRS = N_HEADS // 2
PAIRS_PER_KV = N_PAIRS // N_KV_HEADS
WINDOW = 128
CONV_W = 31
CONV_TAIL = CONV_W - 1
EPS = 1e-5
LANES = 128
SUBLANES = 8
HALF = LANES // 2
NEG_INF = float("-inf")
LOG2E = 1.4426950408889634

PROMPT_TILE = 1024
CONV_ROWS = 64
CONV_CH_BLOCKS = 2
CONV_PAD = 32
CAST_STEPS = 4
SAMPLE_GROUP = 32
SAMPLE_FUSE = 2
SAMPLE_UNROLL = 4
VMEM_LIMIT_BYTES = 56 * 1024 * 1024

F32 = jnp.float32
BF16 = jnp.bfloat16


def _alibi_slope(h):
    return 2.0 ** (-8.0 * (h + 1) / N_HEADS)


def _rms_norm(x, g):
    ms = jnp.mean(x * x, axis=-1, keepdims=True)
    return x * lax.rsqrt(ms + EPS) * g


def _silu(x):
    return x * jax.nn.sigmoid(x)


def _dot(a, b):
    return jnp.dot(a, b, preferred_element_type=F32)


def _dot_nt(a, b):
    return lax.dot_general(a, b, (((1,), (1,)), ((), ())), preferred_element_type=F32)


def _conv_branch_tail(c, lng, lnb, wpw_ref):
    mu = jnp.mean(c, axis=-1, keepdims=True)
    xc = c - mu
    var = jnp.mean(xc * xc, axis=-1, keepdims=True)
    y = xc * lax.rsqrt(var + EPS) * lng + lnb
    return _dot(_silu(y).astype(BF16), wpw_ref[...])


def _gate(val, g):
    return (val * _silu(g)).astype(BF16)


def _softmax_sink(s, sink):
    m = jnp.maximum(jnp.max(s, axis=-1, keepdims=True), sink)
    p = jnp.exp(s - m)
    den = jnp.sum(p, axis=-1, keepdims=True) + jnp.exp(sink - m)
    return p, 1.0 / den


def _cast_kernel(*refs):
    n = len(refs) // 2
    for src, dst in zip(refs[:n], refs[n:]):
        dst[...] = src[...].astype(BF16)


def _cast_weights(*ws):
    steps = CAST_STEPS
    assert all(w.shape[0] % (steps * 2 * SUBLANES) == 0 for w in ws)
    spec = lambda w: pl.BlockSpec((w.shape[0] // steps, w.shape[1]), lambda i: (i, 0))
    return pl.pallas_call(
        _cast_kernel,
        grid=(steps,),
        in_specs=[spec(w) for w in ws],
        out_specs=[spec(w) for w in ws],
        out_shape=[jax.ShapeDtypeStruct(w.shape, BF16) for w in ws],
        name="cast_weights",
    )(*ws)


def _prompt_kernel(x_ref, sink_ref, ng_ref, win_ref, dww_ref, dwb_ref, lng_ref, lnb_ref,
                   wpw_ref, wout_ref, fg_ref,
                   y_ref, kst_ref, vst_ref, cst_ref,
                   k_ref, vt_ref, up_ref, shifted_ref, wb_ref, bias_ref, attn_ref, conv_ref,
                   *, tile, final_norm):
    t = pl.program_id(1)
    d_attn = N_HEADS * HEAD_DIM
    d_kv = N_KV_HEADS * HEAD_DIM
    d_conv = up_ref.shape[1]
    n_blocks = tile // WINDOW
    two_w = 2 * WINDOW
    u_off = d_attn + 2 * d_kv + d_attn
    n_shift = shifted_ref.shape[1]

    @pl.when(t == 0)
    def _init():
        k_ref[:WINDOW, :] = jnp.zeros((WINDOW, d_kv), BF16)
        vt_ref[:, :WINDOW] = jnp.zeros((d_kv, WINDOW), BF16)
        up_ref[:CONV_PAD, :] = jnp.zeros((CONV_PAD, d_conv), F32)
        for j in range(CONV_W):
            wb_ref[j] = jnp.broadcast_to(dww_ref[j], (SUBLANES, d_conv))
        ji = lax.broadcasted_iota(jnp.int32, (two_w, two_w), 0)
        ci = lax.broadcasted_iota(jnp.int32, (two_w, two_w), 1)
        odd = ci >= WINDOW
        dist = jnp.where(odd, ci - WINDOW, ci) + WINDOW - ji
        valid = (dist >= 0) & (dist < WINDOW)
        valid_first = valid & (ji >= WINDOW)
        distf = dist.astype(F32) * LOG2E
        for pair in range(N_PAIRS):
            pen = -jnp.where(odd, _alibi_slope(2 * pair + 1), _alibi_slope(2 * pair)) * distf
            bias_ref[0, pair] = jnp.where(valid, pen, NEG_INF)
            bias_ref[1, pair] = jnp.where(valid_first, pen, NEG_INF)

    x = x_ref[...]
    h_in = _rms_norm(x, ng_ref[...]).astype(BF16)


    cw = d_conv // CONV_CH_BLOCKS
    for cb in range(CONV_CH_BLOCKS):
        ch = slice(cb * cw, (cb + 1) * cw)
        u_a = _dot(h_in, win_ref[:, u_off + cb * cw:u_off + (cb + 1) * cw])
        u_b = _dot(h_in, win_ref[:, u_off + d_conv + cb * cw:u_off + d_conv + (cb + 1) * cw])
        up_ref[CONV_PAD:, ch] = u_a * jax.nn.sigmoid(u_b)
        for s in range(1, SUBLANES):
            shifted_ref[s - 1, :, ch] = up_ref[s:s + n_shift, ch]
    for tau in range(CONV_TAIL):
        row = CONV_PAD + tile - CONV_TAIL + tau
        cst_ref[tau] = up_ref[row:row + 1, :]

    kv = _dot(h_in, win_ref[:, d_attn:d_attn + 2 * d_kv])
    k = kv[:, :d_kv]
    v = kv[:, d_kv:]
    k_ref[WINDOW:, :] = k.astype(BF16)
    v_t = v.T
    vt_ref[:, WINDOW:] = v_t.astype(BF16)
    kst_ref[...] = k[tile - WINDOW:, :].T
    vst_ref[...] = v_t[:, tile - WINDOW:]

    groups = CONV_ROWS // SUBLANES

    def conv_chunk(r, cb):
        ch = slice(cb * cw, (cb + 1) * cw)
        acc = [jnp.zeros((SUBLANES, cw), F32) for _ in range(groups)]
        loaded = {}
        for j in range(CONV_W):
            off = CONV_PAD - CONV_TAIL + j
            a, s = off // SUBLANES, off % SUBLANES
            wb = wb_ref[j, :, ch]
            for g in range(groups):
                if (s, a + g) not in loaded:
                    rows = slice(r + (a + g) * SUBLANES, r + (a + g + 1) * SUBLANES)
                    loaded[s, a + g] = up_ref[rows, ch] if s == 0 else shifted_ref[s - 1, rows, ch]
                acc[g] = acc[g] + loaded[s, a + g] * wb
        for g in range(groups):
            conv_ref[r + g * SUBLANES:r + (g + 1) * SUBLANES, ch] = acc[g]

    q = _dot(h_in, win_ref[:, :d_attn]) * (HEAD_DIM ** -0.5 * LOG2E)
    g_a = _dot(h_in, win_ref[:, d_attn + 2 * d_kv:u_off])
    g_c = _dot(h_in, win_ref[:, u_off + 2 * d_conv:])

    lo = lax.broadcasted_iota(jnp.int32, (WINDOW, LANES), 1) < HALF
    zero = jnp.zeros((WINDOW, LANES), F32)
    odd_head = lax.broadcasted_iota(jnp.int32, (1, two_w), 1) >= WINDOW

    def scores(blk, pair):
        r0 = blk * WINDOW
        first = jnp.where(t == 0, 1, 0) if blk == 0 else 0
        kvh = pair // PAIRS_PER_KV
        qp = q[r0:r0 + WINDOW, pair * LANES:(pair + 1) * LANES]
        rolled = pltpu.roll(qp, HALF, axis=1)
        if kvh == 0:
            q_even, q_odd = jnp.where(lo, qp, zero), jnp.where(lo, rolled, zero)
        else:
            q_even, q_odd = jnp.where(lo, zero, rolled), jnp.where(lo, zero, qp)
        qm = jnp.concatenate([q_even, q_odd], axis=0).astype(BF16)
        return _dot_nt(k_ref[r0:r0 + two_w, :], qm) + bias_ref[first, pair]

    def attend(blk, pair, s):
        r0 = blk * WINDOW
        kvh = pair // PAIRS_PER_KV
        sink = jnp.where(odd_head, sink_ref[2 * pair + 1], sink_ref[2 * pair]) * LOG2E
        m = jnp.maximum(jnp.max(s, axis=0, keepdims=True), sink)
        p = jnp.exp2(s - m)
        den = jnp.sum(p, axis=0, keepdims=True) + jnp.exp2(sink - m)
        vt_win = vt_ref[kvh * HEAD_DIM:(kvh + 1) * HEAD_DIM, r0:r0 + two_w]
        ot = _dot(vt_win, p.astype(BF16)) * (1.0 / den)
        o = jnp.concatenate([ot[:, :WINDOW], ot[:, WINDOW:]], axis=0).T
        attn_ref[r0:r0 + WINDOW, pair * LANES:(pair + 1) * LANES] = o

    steps = [(blk, pair) for blk in range(n_blocks) for pair in range(N_PAIRS)]
    s_next = scores(*steps[0])
    for i, step in enumerate(steps):
        s_cur = s_next
        if i + 1 < len(steps):
            s_next = scores(*steps[i + 1])
        attend(*step, s_cur)

    y_a = _dot(_gate(attn_ref[...], g_a), wout_ref[:d_attn, :])
    for cb in range(CONV_CH_BLOCKS):
        for r in range(0, tile, CONV_ROWS):
            conv_chunk(r, cb)

    c2 = _conv_branch_tail(conv_ref[...] + dwb_ref[...], lng_ref[...], lnb_ref[...], wpw_ref)
    y = x + y_a + _dot(_gate(c2, g_c), wout_ref[d_attn:, :])
    y_ref[...] = _rms_norm(y, fg_ref[...]) if final_norm else y

    k_ref[:WINDOW, :] = k_ref[tile:, :]
    vt_ref[:, :WINDOW] = vt_ref[:, tile:]
    up_ref[:CONV_PAD, :] = up_ref[tile:, :]


def _prompt_layer(x, sinks, ng, win, dww, dwb, lng, lnb, wpw, wout, fg, final_norm):
    B, T, D = x.shape
    tile = PROMPT_TILE
    assert T % tile == 0 and tile % WINDOW == 0 and tile % CONV_ROWS == 0
    d_conv = dww.shape[-1]
    d_kv = N_KV_HEADS * HEAD_DIM
    d_attn = N_HEADS * HEAD_DIM
    n_t = T // tile
    const = lambda shape: pl.BlockSpec(shape, lambda b, t: (0,) * len(shape))
    return pl.pallas_call(
        functools.partial(_prompt_kernel, tile=tile, final_norm=final_norm),
        grid=(B, n_t),
        in_specs=[
            pl.BlockSpec((None, tile, D), lambda b, t: (b, t, 0)),
            pl.BlockSpec(memory_space=pltpu.SMEM),
            const(ng.shape), const(win.shape), const(dww.shape), const(dwb.shape),
            const(lng.shape), const(lnb.shape), const(wpw.shape), const(wout.shape), const(fg.shape),
        ],
        out_specs=[
            pl.BlockSpec((None, tile, D), lambda b, t: (b, t, 0)),
            pl.BlockSpec((None, d_kv, WINDOW), lambda b, t: (b, 0, 0)),
            pl.BlockSpec((None, d_kv, WINDOW), lambda b, t: (b, 0, 0)),
            pl.BlockSpec((CONV_TAIL, 1, d_conv), lambda b, t: (0, b, 0)),
        ],
        scratch_shapes=[
            pltpu.VMEM((WINDOW + tile, d_kv), BF16),
            pltpu.VMEM((d_kv, WINDOW + tile), BF16),
            pltpu.VMEM((CONV_PAD + tile, d_conv), F32),
            pltpu.VMEM((SUBLANES - 1, CONV_PAD - SUBLANES + tile, d_conv), F32),
            pltpu.VMEM((CONV_W, SUBLANES, d_conv), F32),
            pltpu.VMEM((2, N_PAIRS, 2 * WINDOW, 2 * WINDOW), F32),
            pltpu.VMEM((tile, d_attn), F32),
            pltpu.VMEM((tile, d_conv), F32),
        ],
        out_shape=[
            jax.ShapeDtypeStruct((B, T, D), F32),
            jax.ShapeDtypeStruct((B, d_kv, WINDOW), F32),
            jax.ShapeDtypeStruct((B, d_kv, WINDOW), F32),
            jax.ShapeDtypeStruct((CONV_TAIL, B, d_conv), F32),
        ],
        compiler_params=pltpu.CompilerParams(
            dimension_semantics=("arbitrary", "arbitrary"), vmem_limit_bytes=VMEM_LIMIT_BYTES),
        name="prompt_layer",
    )(x, sinks, ng, win, dww, dwb, lng, lnb, wpw, wout, fg)


def _sample_kernel(x_ref, ck_ref, cv_ref, cs_ref, sink_ref, ng_ref, win_ref,
                   dww_ref, dwb_ref, lng_ref, lnb_ref, wpw_ref, wout_ref, fg_ref,
                   y_ref, ko_ref, vo_ref, co_ref,
                   q_ref, kn_ref, vn_ref, upt_ref, wb_ref, bias_ref, sinkcol_ref, attn_ref, conv_ref,
                   perm_ref, perm_t_ref, *, group, dec, final_norm):
    d_attn = N_HEADS * HEAD_DIM
    d_kv = N_KV_HEADS * HEAD_DIM
    d_conv = conv_ref.shape[1]
    n_keys = WINDOW + dec
    rows_q = N_HEADS * dec

    @pl.when(pl.program_id(0) == 0)
    def _init():
        shape = (SAMPLE_FUSE * rows_q, SAMPLE_FUSE * n_keys)
        n_cache = SAMPLE_FUSE * WINDOW
        log2 = lambda n: n.bit_length() - 1
        ri = lax.broadcasted_iota(jnp.int32, shape, 0)
        ci = lax.broadcasted_iota(jnp.int32, shape, 1)
        row_elem = lax.shift_right_logical(ri, log2(rows_q))
        row_head = lax.shift_right_logical(ri & (rows_q - 1), log2(dec))
        row_tok = ri & (dec - 1)
        is_cache = ci < n_cache
        cn = ci - n_cache
        col_elem = jnp.where(is_cache, lax.shift_right_logical(ci, log2(WINDOW)),
                             lax.shift_right_logical(cn, log2(dec)))
        dist = jnp.where(is_cache, WINDOW + row_tok - (ci & (WINDOW - 1)), row_tok - (cn & (dec - 1)))
        valid = (row_elem == col_elem) & (dist >= 0) & (dist < WINDOW)
        slope = jnp.zeros(shape, F32)
        for h in range(N_HEADS):
            slope = jnp.where(row_head == h, _alibi_slope(h), slope)
        bias_ref[...] = jnp.where(valid, -slope * dist.astype(F32), NEG_INF)
        rc = lax.broadcasted_iota(jnp.int32, (shape[0], 1), 0)
        rc_head = lax.shift_right_logical(rc & (rows_q - 1), log2(dec))
        sink_rows = jnp.zeros((shape[0], 1), F32)
        for h in range(N_HEADS):
            sink_rows = jnp.where(rc_head == h, sink_ref[h], sink_rows)
        sinkcol_ref[...] = sink_rows
        for j in range(CONV_W):
            wb_ref[j] = jnp.broadcast_to(dww_ref[j], (SUBLANES, d_conv))
        n_rows = group * dec
        pi = lax.broadcasted_iota(jnp.int32, (n_rows, n_rows), 0)
        pj = lax.broadcasted_iota(jnp.int32, (n_rows, n_rows), 1)
        shift = group.bit_length() - 1
        source = lambda i: (i & (group - 1)) * dec + lax.shift_right_logical(i, shift)
        perm_ref[...] = jnp.where(pj == source(pi), 1.0, 0.0).astype(BF16)
        perm_t_ref[...] = jnp.where(pi == source(pj), 1.0, 0.0).astype(BF16)

    x = x_ref[...]
    h_in = _rms_norm(x, ng_ref[...]).astype(BF16)
    q_ref[...] = _dot(h_in, win_ref[:, :d_attn]) * (HEAD_DIM ** -0.5)
    kv = _dot(h_in, win_ref[:, d_attn:d_attn + 2 * d_kv])
    kn_ref[...] = kv[:, :d_kv]
    vn_ref[...] = kv[:, d_kv:]
    g_a = _dot(h_in, win_ref[:, d_attn + 2 * d_kv:d_attn + 2 * d_kv + d_attn])

    h_tm = _dot(perm_ref[...], h_in).astype(BF16)
    u_off = d_attn + 2 * d_kv + d_attn
    u_ab = _dot(h_tm, win_ref[:, u_off:u_off + 2 * d_conv])
    u_tm = u_ab[:, :d_conv] * jax.nn.sigmoid(u_ab[:, d_conv:])
    g_c_tm = _dot(h_tm, win_ref[:, u_off + 2 * d_conv:])

    lo = lax.broadcasted_iota(jnp.int32, (dec, LANES), 1) < HALF
    zero = jnp.zeros((dec, LANES), F32)
    sink = sinkcol_ref[...]
    bias = bias_ref[...]

    def keys_of(bs, cache_ref, new_ref, state_ref):
        cached = [cache_ref[b].T for b in bs]
        new = [new_ref[pl.ds(pl.multiple_of(b * dec, dec), dec), :] for b in bs]
        for b, c, n in zip(bs, cached, new):
            state_ref[b] = jnp.concatenate([c[dec:, :], n], axis=0)
        return jnp.concatenate(cached + new, axis=0).astype(BF16)

    def scores(bs):
        rows = []
        for b in bs:
            q_b = q_ref[pl.ds(pl.multiple_of(b * dec, dec), dec), :]
            for h in range(N_HEADS):
                tile_q = q_b[:, (h // 2) * LANES:(h // 2 + 1) * LANES]
                in_lo = h % 2 == 0
                want_lo = h // GQA_GROUP == 0
                src = tile_q if in_lo == want_lo else pltpu.roll(tile_q, HALF, axis=1)
                rows.append(jnp.where(lo, src, zero) if want_lo else jnp.where(lo, zero, src))
        q_rows = jnp.concatenate(rows, axis=0).astype(BF16)
        return _dot_nt(q_rows, keys_of(bs, ck_ref, kn_ref, ko_ref)) + bias

    def attend(bs, s):
        p, rden = _softmax_sink(s, sink)
        o_all = _dot(p.astype(BF16), keys_of(bs, cv_ref, vn_ref, vo_ref)) * rden
        for e, b in enumerate(bs):
            o = o_all[e * rows_q:(e + 1) * rows_q, :]
            tiles = []
            for pair in range(N_PAIRS):
                a = o[(2 * pair) * dec:(2 * pair + 1) * dec, :]
                c = o[(2 * pair + 1) * dec:(2 * pair + 2) * dec, :]
                if pair // PAIRS_PER_KV == 0:
                    tiles.append(jnp.where(lo, a, pltpu.roll(c, HALF, axis=1)))
                else:
                    tiles.append(jnp.where(lo, pltpu.roll(a, HALF, axis=1), c))
            attn_ref[pl.ds(pl.multiple_of(b * dec, dec), dec), :] = jnp.concatenate(tiles, axis=1)

    upt_ref[:CONV_TAIL] = cs_ref[...]
    for tt in range(dec):
        upt_ref[CONV_TAIL + tt] = u_tm[tt * group:(tt + 1) * group, :]
    co_ref[...] = upt_ref[dec:]

    def conv_step(tt):
        for g0 in range(0, group, SUBLANES):
            acc = jnp.zeros((SUBLANES, d_conv), F32)
            for j in range(CONV_W):
                acc = acc + upt_ref[tt + j, g0:g0 + SUBLANES, :] * wb_ref[j]
            start = tt * group + g0
            if not isinstance(start, int):
                start = pl.multiple_of(start, SUBLANES)
            conv_ref[pl.ds(start, SUBLANES), :] = acc

    trips = group // SAMPLE_UNROLL
    conv_per_trip = dec // trips

    def attention_batch(i, carry):
        chains = [[i * SAMPLE_UNROLL + c * SAMPLE_FUSE + e for e in range(SAMPLE_FUSE)]
                  for c in range(SAMPLE_UNROLL // SAMPLE_FUSE)]
        ss = [scores(bs) for bs in chains]
        for bs, s in zip(chains, ss):
            attend(bs, s)
        for c in range(conv_per_trip):
            conv_step(i * conv_per_trip + c)
        return carry

    lax.fori_loop(0, trips, attention_batch, 0)
    for tt in range(trips * conv_per_trip, dec):
        conv_step(tt)

    y_a = _dot(_gate(attn_ref[...], g_a), wout_ref[:d_attn, :])
    c2_tm = _conv_branch_tail(conv_ref[...] + dwb_ref[...], lng_ref[...], lnb_ref[...], wpw_ref)
    m_c = _dot(perm_t_ref[...], _gate(c2_tm, g_c_tm)).astype(BF16)
    y = x + y_a + _dot(m_c, wout_ref[d_attn:, :])
    y_ref[...] = _rms_norm(y, fg_ref[...]) if final_norm else y


def _sample_layer(x, ck, cv, cs, sinks, ng, win, dww, dwb, lng, lnb, wpw, wout, fg, final_norm):
    Bs, dec, D = x.shape
    group = SAMPLE_GROUP
    assert Bs % group == 0 and dec % 8 == 0 and dec <= CONV_TAIL
    d_conv = dww.shape[-1]
    d_kv = N_KV_HEADS * HEAD_DIM
    d_attn = N_HEADS * HEAD_DIM
    rows = group * dec
    x2 = x.reshape(Bs * dec, D)
    assert all(n & (n - 1) == 0 for n in (group, dec, N_HEADS, WINDOW))
    assert group % SAMPLE_UNROLL == 0 and SAMPLE_UNROLL % SAMPLE_FUSE == 0
    const = lambda shape: pl.BlockSpec(shape, lambda g: (0,) * len(shape))
    per_group = lambda shape: pl.BlockSpec((group,) + shape, lambda g: (g,) + (0,) * len(shape))
    conv_state = pl.BlockSpec((CONV_TAIL, group, d_conv), lambda g: (0, g, 0))
    y, ko, vo, co = pl.pallas_call(
        functools.partial(_sample_kernel, group=group, dec=dec, final_norm=final_norm),
        grid=(Bs // group,),
        in_specs=[
            pl.BlockSpec((rows, D), lambda g: (g, 0)),
            per_group((d_kv, WINDOW)), per_group((d_kv, WINDOW)), conv_state,
            pl.BlockSpec(memory_space=pltpu.SMEM),
            const(ng.shape), const(win.shape), const(dww.shape), const(dwb.shape),
            const(lng.shape), const(lnb.shape), const(wpw.shape), const(wout.shape), const(fg.shape),
        ],
        out_specs=[
            pl.BlockSpec((rows, D), lambda g: (g, 0)),
            per_group((WINDOW, d_kv)), per_group((WINDOW, d_kv)), conv_state,
        ],
        out_shape=[
            jax.ShapeDtypeStruct((Bs * dec, D), F32),
            jax.ShapeDtypeStruct((Bs, WINDOW, d_kv), F32),
            jax.ShapeDtypeStruct((Bs, WINDOW, d_kv), F32),
            jax.ShapeDtypeStruct((CONV_TAIL, Bs, d_conv), F32),
        ],
        scratch_shapes=[
            pltpu.VMEM((rows, d_attn), F32),
            pltpu.VMEM((rows, d_kv), F32),
            pltpu.VMEM((rows, d_kv), F32),
            pltpu.VMEM((CONV_TAIL + dec, group, d_conv), F32),
            pltpu.VMEM((CONV_W, SUBLANES, d_conv), F32),
            pltpu.VMEM((SAMPLE_FUSE * N_HEADS * dec, SAMPLE_FUSE * (WINDOW + dec)), F32),
            pltpu.VMEM((SAMPLE_FUSE * N_HEADS * dec, 1), F32),
            pltpu.VMEM((rows, d_attn), F32),
            pltpu.VMEM((rows, d_conv), F32),
            pltpu.VMEM((rows, rows), BF16),
            pltpu.VMEM((rows, rows), BF16),
        ],
        compiler_params=pltpu.CompilerParams(
            dimension_semantics=("arbitrary",), vmem_limit_bytes=VMEM_LIMIT_BYTES),
        name="sample_layer",
    )(x2, ck, cv, cs, sinks, ng, win, dww, dwb, lng, lnb, wpw, wout, fg)
    return y.reshape(Bs, dec, D), ko, vo, co


def kernel(x_prompt, x_sample, cache_k, cache_v, state_conv, norm_g, w_in, attn_sinks, dw_w, dw_b,
           conv_ln_g, conv_ln_b, w_pw2, w_out, final_norm_g):
    depth = w_in.shape[0]
    B = x_prompt.shape[0]
    Bs, dec = x_sample.shape[0], x_sample.shape[1]
    d_kv = N_KV_HEADS * HEAD_DIM
    fg = final_norm_g.reshape(1, -1)
    hp, hs = x_prompt, x_sample
    pk, pv, pc, sk, sv, sc = [], [], [], [], [], []
    kv_major = lambda c: c.transpose(0, 2, 3, 1).reshape(Bs, d_kv, WINDOW)
    for l in range(depth):
        final_norm = l == depth - 1
        row = lambda a: a[l].reshape(1, -1)
        win, wpw, wout = _cast_weights(w_in[l], w_pw2[l], w_out[l])
        shared = (row(norm_g), win, dw_w[l][:, None, :], row(dw_b), row(conv_ln_g), row(conv_ln_b),
                  wpw, wout, fg)
        hp, k_p, v_p, c_p = _prompt_layer(hp, attn_sinks[l], *shared, final_norm)
        hs, k_s, v_s, c_s = _sample_layer(
            hs, kv_major(cache_k[l]), kv_major(cache_v[l]),
            state_conv[l].swapaxes(0, 1), attn_sinks[l], *shared, final_norm)
        key_major = lambda s: s.reshape(B, N_KV_HEADS, HEAD_DIM, WINDOW).transpose(0, 3, 1, 2)
        pk.append(key_major(k_p))
        pv.append(key_major(v_p))
        pc.append(c_p.swapaxes(0, 1))
        sk.append(k_s.reshape(Bs, WINDOW, N_KV_HEADS, HEAD_DIM))
        sv.append(v_s.reshape(Bs, WINDOW, N_KV_HEADS, HEAD_DIM))
        sc.append(c_s.swapaxes(0, 1))
    stack = lambda xs: xs[0][None] if len(xs) == 1 else jnp.stack(xs)
    return (hp, hs, stack(pk), stack(pv), stack(pc), stack(sk), stack(sv), stack(sc))
```

```python
import functools

import jax
import jax.numpy as jnp
from jax import lax
from jax.experimental import pallas as pl
from jax.experimental.pallas import tpu as pltpu

HEAD_DIM = 64
N_HEADS = 8
N_KV_HEADS = 2
GQA_GROUP = N_HEADS // N_KV_HEADS
N_PAIRS = N_HEADS // 2
PAIRS_PER_KV = N_PAIRS // N_KV_HEADS
WINDOW = 128
CONV_W = 31
CONV_TAIL = CONV_W - 1
EPS = 1e-5
LANES = 128
SUBLANES = 8
HALF = LANES // 2
NEG_INF = float("-inf")
LOG2E = 1.4426950408889634

PROMPT_TILE = 1024
CONV_ROWS = 64
CONV_CH_BLOCKS = 2
CONV_PAD = 32
CAST_ROWS = 128
FETCH_ROWS = 256
SAMPLE_GROUP = 32
SAMPLE_FUSE = 2
SAMPLE_UNROLL = 4
VMEM_LIMIT_BYTES = 60 * 1024 * 1024

F32 = jnp.float32
BF16 = jnp.bfloat16


def _alibi_slope(h):
    return 2.0 ** (-8.0 * (h + 1) / N_HEADS)


def _rms_norm(x, g):
    ms = jnp.mean(x * x, axis=-1, keepdims=True)
    return x * lax.rsqrt(ms + EPS) * g


def _silu(x):
    return x * jax.nn.sigmoid(x)


def _dot(a, b):
    return jnp.dot(a, b, preferred_element_type=F32)


def _dot_nt(a, b):
    return lax.dot_general(a, b, (((1,), (1,)), ((), ())), preferred_element_type=F32)


def _conv_branch_tail(c, lng, lnb, wpw_ref):
    mu = jnp.mean(c, axis=-1, keepdims=True)
    xc = c - mu
    var = jnp.mean(xc * xc, axis=-1, keepdims=True)
    y = xc * lax.rsqrt(var + EPS) * lng + lnb
    return _dot(_silu(y).astype(BF16), wpw_ref[...])


def _gate(val, g):
    return (val * _silu(g)).astype(BF16)


def _softmax_sink(s, sink):
    m = jnp.maximum(jnp.max(s, axis=-1, keepdims=True), sink)
    p = jnp.exp(s - m)
    den = jnp.sum(p, axis=-1, keepdims=True) + jnp.exp(sink - m)
    return p, 1.0 / den


def _cast_to(dst_ref, src_ref):
    rows = CAST_ROWS
    for r in range(0, src_ref.shape[0], rows):
        dst_ref[r:r + rows, :] = src_ref[r:r + rows, :].astype(BF16)


def _fetch_and_cast(pairs, sem):
    copies = []
    for dst_ref, stage_ref, src_ref in pairs:
        for r in range(0, src_ref.shape[0], FETCH_ROWS):
            n = min(FETCH_ROWS, src_ref.shape[0] - r)
            cp = pltpu.make_async_copy(src_ref.at[pl.ds(r, n), :], stage_ref.at[pl.ds(r, n), :],
                                       sem.at[len(copies)])
            cp.start()
            copies.append((cp, dst_ref, stage_ref, r, n))

    def finish():
        for cp, dst_ref, stage_ref, r, n in copies:
            cp.wait()
            for q in range(r, r + n, CAST_ROWS):
                dst_ref[q:q + CAST_ROWS, :] = stage_ref[q:q + CAST_ROWS, :].astype(BF16)
    return finish


def _prompt_kernel(x_ref, sink_ref, ng_ref, winf_ref, dww_ref, dwb_ref, lng_ref, lnb_ref,
                   wpwf_ref, woutf_ref, fg_ref,
                   y_ref, kst_ref, vst_ref, cst_ref,
                   k_ref, vt_ref, up_ref, shifted_ref, wb_ref, bias_ref, attn_ref, conv_ref,
                   win_ref, wpw_ref, wout_ref, wins_ref, wpws_ref, wouts_ref, wsem,
                   *, tile, final_norm):
    t = pl.program_id(1)
    d_attn = N_HEADS * HEAD_DIM
    d_kv = N_KV_HEADS * HEAD_DIM
    d_conv = up_ref.shape[1]
    n_blocks = tile // WINDOW
    two_w = 2 * WINDOW
    u_off = d_attn + 2 * d_kv + d_attn
    n_shift = shifted_ref.shape[1]

    @pl.when(t == 0)
    def _init():
        finish_weights = _fetch_and_cast(
            [(win_ref, wins_ref, winf_ref), (wpw_ref, wpws_ref, wpwf_ref), (wout_ref, wouts_ref, woutf_ref)], wsem)
        k_ref[:WINDOW, :] = jnp.zeros((WINDOW, d_kv), BF16)
        vt_ref[:, :WINDOW] = jnp.zeros((d_kv, WINDOW), BF16)
        up_ref[:CONV_PAD, :] = jnp.zeros((CONV_PAD, d_conv), F32)
        for j in range(CONV_W):
            wb_ref[j] = jnp.broadcast_to(dww_ref[j], (SUBLANES, d_conv))
        ji = lax.broadcasted_iota(jnp.int32, (two_w, two_w), 0)
        ci = lax.broadcasted_iota(jnp.int32, (two_w, two_w), 1)
        odd = ci >= WINDOW
        dist = jnp.where(odd, ci - WINDOW, ci) + WINDOW - ji
        valid = (dist >= 0) & (dist < WINDOW)
        valid_first = valid & (ji >= WINDOW)
        distf = dist.astype(F32) * LOG2E
        for pair in range(N_PAIRS):
            pen = -jnp.where(odd, _alibi_slope(2 * pair + 1), _alibi_slope(2 * pair)) * distf
            bias_ref[0, pair] = jnp.where(valid, pen, NEG_INF)
            bias_ref[1, pair] = jnp.where(valid_first, pen, NEG_INF)
        finish_weights()

    x = x_ref[...]
    h_in = _rms_norm(x, ng_ref[...]).astype(BF16)


    cw = d_conv // CONV_CH_BLOCKS
    for cb in range(CONV_CH_BLOCKS):
        ch = slice(cb * cw, (cb + 1) * cw)
        u_a = _dot(h_in, win_ref[:, u_off + cb * cw:u_off + (cb + 1) * cw])
        u_b = _dot(h_in, win_ref[:, u_off + d_conv + cb * cw:u_off + d_conv + (cb + 1) * cw])
        up_ref[CONV_PAD:, ch] = u_a * jax.nn.sigmoid(u_b)
        for s in range(1, SUBLANES):
            shifted_ref[s - 1, :, ch] = up_ref[s:s + n_shift, ch]
    for tau in range(CONV_TAIL):
        row = CONV_PAD + tile - CONV_TAIL + tau
        cst_ref[tau] = up_ref[row:row + 1, :]

    kv = _dot(h_in, win_ref[:, d_attn:d_attn + 2 * d_kv])
    k = kv[:, :d_kv]
    v = kv[:, d_kv:]
    k_ref[WINDOW:, :] = k.astype(BF16)
    v_t = v.T
    vt_ref[:, WINDOW:] = v_t.astype(BF16)
    kst_ref[...] = k[tile - WINDOW:, :].T
    vst_ref[...] = v_t[:, tile - WINDOW:]

    groups = CONV_ROWS // SUBLANES

    def conv_chunk(r, cb):
        ch = slice(cb * cw, (cb + 1) * cw)
        acc = [jnp.zeros((SUBLANES, cw), F32) for _ in range(groups)]
        loaded = {}
        for j in range(CONV_W):
            off = CONV_PAD - CONV_TAIL + j
            a, s = off // SUBLANES, off % SUBLANES
            wb = wb_ref[j, :, ch]
            for g in range(groups):
                if (s, a + g) not in loaded:
                    rows = slice(r + (a + g) * SUBLANES, r + (a + g + 1) * SUBLANES)
                    loaded[s, a + g] = up_ref[rows, ch] if s == 0 else shifted_ref[s - 1, rows, ch]
                acc[g] = acc[g] + loaded[s, a + g] * wb
        for g in range(groups):
            conv_ref[r + g * SUBLANES:r + (g + 1) * SUBLANES, ch] = acc[g]

    q = _dot(h_in, win_ref[:, :d_attn]) * (HEAD_DIM ** -0.5 * LOG2E)
    g_a = _dot(h_in, win_ref[:, d_attn + 2 * d_kv:u_off])
    g_c = _dot(h_in, win_ref[:, u_off + 2 * d_conv:])

    lo = lax.broadcasted_iota(jnp.int32, (WINDOW, LANES), 1) < HALF
    zero = jnp.zeros((WINDOW, LANES), F32)
    odd_head = lax.broadcasted_iota(jnp.int32, (1, two_w), 1) >= WINDOW

    def scores(blk, pair):
        r0 = blk * WINDOW
        first = jnp.where(t == 0, 1, 0) if blk == 0 else 0
        kvh = pair // PAIRS_PER_KV
        qp = q[r0:r0 + WINDOW, pair * LANES:(pair + 1) * LANES]
        rolled = pltpu.roll(qp, HALF, axis=1)
        if kvh == 0:
            q_even, q_odd = jnp.where(lo, qp, zero), jnp.where(lo, rolled, zero)
        else:
            q_even, q_odd = jnp.where(lo, zero, rolled), jnp.where(lo, zero, qp)
        qm = jnp.concatenate([q_even, q_odd], axis=0).astype(BF16)
        return _dot_nt(k_ref[r0:r0 + two_w, :], qm) + bias_ref[first, pair]

    def attend(blk, pair, s):
        r0 = blk * WINDOW
        kvh = pair // PAIRS_PER_KV
        sink = jnp.where(odd_head, sink_ref[2 * pair + 1], sink_ref[2 * pair]) * LOG2E
        m = jnp.maximum(jnp.max(s, axis=0, keepdims=True), sink)
        p = jnp.exp2(s - m)
        den = jnp.sum(p, axis=0, keepdims=True) + jnp.exp2(sink - m)
        vt_win = vt_ref[kvh * HEAD_DIM:(kvh + 1) * HEAD_DIM, r0:r0 + two_w]
        ot = _dot(vt_win, p.astype(BF16)) * (1.0 / den)
        o = jnp.concatenate([ot[:, :WINDOW], ot[:, WINDOW:]], axis=0).T
        attn_ref[r0:r0 + WINDOW, pair * LANES:(pair + 1) * LANES] = o

    steps = [(blk, pair) for blk in range(n_blocks) for pair in range(N_PAIRS)]
    s_next = scores(*steps[0])
    for i, step in enumerate(steps):
        s_cur = s_next
        if i + 1 < len(steps):
            s_next = scores(*steps[i + 1])
        attend(*step, s_cur)

    y_a = _dot(_gate(attn_ref[...], g_a), wout_ref[:d_attn, :])
    for cb in range(CONV_CH_BLOCKS):
        for r in range(0, tile, CONV_ROWS):
            conv_chunk(r, cb)

    c2 = _conv_branch_tail(conv_ref[...] + dwb_ref[...], lng_ref[...], lnb_ref[...], wpw_ref)
    y = x + y_a + _dot(_gate(c2, g_c), wout_ref[d_attn:, :])
    y_ref[...] = _rms_norm(y, fg_ref[...]) if final_norm else y

    k_ref[:WINDOW, :] = k_ref[tile:, :]
    vt_ref[:, :WINDOW] = vt_ref[:, tile:]
    up_ref[:CONV_PAD, :] = up_ref[tile:, :]


def _prompt_layer(x, sinks, ng, win, dww, dwb, lng, lnb, wpw, wout, fg, final_norm):
    B, T, D = x.shape
    tile = PROMPT_TILE
    assert T % tile == 0 and tile % WINDOW == 0 and tile % CONV_ROWS == 0
    d_conv = dww.shape[-1]
    d_kv = N_KV_HEADS * HEAD_DIM
    d_attn = N_HEADS * HEAD_DIM
    n_t = T // tile
    const = lambda shape: pl.BlockSpec(shape, lambda b, t: (0,) * len(shape))
    once = lambda shape: pl.BlockSpec(memory_space=pl.ANY)
    n_copies = sum(-(-w.shape[0] // FETCH_ROWS) for w in (win, wpw, wout))
    return pl.pallas_call(
        functools.partial(_prompt_kernel, tile=tile, final_norm=final_norm),
        grid=(B, n_t),
        in_specs=[
            pl.BlockSpec((None, tile, D), lambda b, t: (b, t, 0)),
            pl.BlockSpec(memory_space=pltpu.SMEM),
            const(ng.shape), once(win.shape), const(dww.shape), const(dwb.shape),
            const(lng.shape), const(lnb.shape), once(wpw.shape), once(wout.shape), const(fg.shape),
        ],
        out_specs=[
            pl.BlockSpec((None, tile, D), lambda b, t: (b, t, 0)),
            pl.BlockSpec((None, d_kv, WINDOW), lambda b, t: (b, 0, 0)),
            pl.BlockSpec((None, d_kv, WINDOW), lambda b, t: (b, 0, 0)),
            pl.BlockSpec((CONV_TAIL, 1, d_conv), lambda b, t: (0, b, 0)),
        ],
        scratch_shapes=[
            pltpu.VMEM((WINDOW + tile, d_kv), BF16),
            pltpu.VMEM((d_kv, WINDOW + tile), BF16),
            pltpu.VMEM((CONV_PAD + tile, d_conv), F32),
            pltpu.VMEM((SUBLANES - 1, CONV_PAD - SUBLANES + tile, d_conv), F32),
            pltpu.VMEM((CONV_W, SUBLANES, d_conv), F32),
            pltpu.VMEM((2, N_PAIRS, 2 * WINDOW, 2 * WINDOW), F32),
            pltpu.VMEM((tile, d_attn), F32),
            pltpu.VMEM((tile, d_conv), F32),
            pltpu.VMEM(win.shape, BF16), pltpu.VMEM(wpw.shape, BF16), pltpu.VMEM(wout.shape, BF16),
            pltpu.VMEM(win.shape, F32), pltpu.VMEM(wpw.shape, F32), pltpu.VMEM(wout.shape, F32),
            pltpu.SemaphoreType.DMA((n_copies,)),
        ],
        out_shape=[
            jax.ShapeDtypeStruct((B, T, D), F32),
            jax.ShapeDtypeStruct((B, d_kv, WINDOW), F32),
            jax.ShapeDtypeStruct((B, d_kv, WINDOW), F32),
            jax.ShapeDtypeStruct((CONV_TAIL, B, d_conv), F32),
        ],
        compiler_params=pltpu.CompilerParams(
            dimension_semantics=("arbitrary", "arbitrary"), vmem_limit_bytes=VMEM_LIMIT_BYTES),
        name="prompt_layer",
    )(x, sinks, ng, win, dww, dwb, lng, lnb, wpw, wout, fg)


def _sample_kernel(x_ref, ck_ref, cv_ref, cs_ref, sink_ref, ng_ref, winf_ref,
                   dww_ref, dwb_ref, lng_ref, lnb_ref, wpwf_ref, woutf_ref, fg_ref,
                   y_ref, ko_ref, vo_ref, co_ref,
                   q_ref, kn_ref, vn_ref, upt_ref, wb_ref, bias_ref, sinkcol_ref, attn_ref, conv_ref,
                   perm_ref, perm_t_ref, win_ref, wpw_ref, wout_ref, *, group, dec, final_norm):
    d_attn = N_HEADS * HEAD_DIM
    d_kv = N_KV_HEADS * HEAD_DIM
    d_conv = conv_ref.shape[1]
    n_keys = WINDOW + dec
    rows_q = N_HEADS * dec

    @pl.when(pl.program_id(0) == 0)
    def _init():
        _cast_to(win_ref, winf_ref)
        _cast_to(wpw_ref, wpwf_ref)
        _cast_to(wout_ref, woutf_ref)
        shape = (SAMPLE_FUSE * rows_q, SAMPLE_FUSE * n_keys)
        n_cache = SAMPLE_FUSE * WINDOW
        log2 = lambda n: n.bit_length() - 1
        ri = lax.broadcasted_iota(jnp.int32, shape, 0)
        ci = lax.broadcasted_iota(jnp.int32, shape, 1)
        row_elem = lax.shift_right_logical(ri, log2(rows_q))
        row_head = lax.shift_right_logical(ri & (rows_q - 1), log2(dec))
        row_tok = ri & (dec - 1)
        is_cache = ci < n_cache
        cn = ci - n_cache
        col_elem = jnp.where(is_cache, lax.shift_right_logical(ci, log2(WINDOW)),
                             lax.shift_right_logical(cn, log2(dec)))
        dist = jnp.where(is_cache, WINDOW + row_tok - (ci & (WINDOW - 1)), row_tok - (cn & (dec - 1)))
        valid = (row_elem == col_elem) & (dist >= 0) & (dist < WINDOW)
        slope = jnp.zeros(shape, F32)
        for h in range(N_HEADS):
            slope = jnp.where(row_head == h, _alibi_slope(h), slope)
        bias_ref[...] = jnp.where(valid, -slope * dist.astype(F32), NEG_INF)
        rc = lax.broadcasted_iota(jnp.int32, (shape[0], 1), 0)
        rc_head = lax.shift_right_logical(rc & (rows_q - 1), log2(dec))
        sink_rows = jnp.zeros((shape[0], 1), F32)
        for h in range(N_HEADS):
            sink_rows = jnp.where(rc_head == h, sink_ref[h], sink_rows)
        sinkcol_ref[...] = sink_rows
        for j in range(CONV_W):
            wb_ref[j] = jnp.broadcast_to(dww_ref[j], (SUBLANES, d_conv))
        n_rows = group * dec
        pi = lax.broadcasted_iota(jnp.int32, (n_rows, n_rows), 0)
        pj = lax.broadcasted_iota(jnp.int32, (n_rows, n_rows), 1)
        shift = group.bit_length() - 1
        source = lambda i: (i & (group - 1)) * dec + lax.shift_right_logical(i, shift)
        perm_ref[...] = jnp.where(pj == source(pi), 1.0, 0.0).astype(BF16)
        perm_t_ref[...] = jnp.where(pi == source(pj), 1.0, 0.0).astype(BF16)

    x = x_ref[...]
    h_in = _rms_norm(x, ng_ref[...]).astype(BF16)
    q_ref[...] = _dot(h_in, win_ref[:, :d_attn]) * (HEAD_DIM ** -0.5)
    kv = _dot(h_in, win_ref[:, d_attn:d_attn + 2 * d_kv])
    kn_ref[...] = kv[:, :d_kv]
    vn_ref[...] = kv[:, d_kv:]
    g_a = _dot(h_in, win_ref[:, d_attn + 2 * d_kv:d_attn + 2 * d_kv + d_attn])

    h_tm = _dot(perm_ref[...], h_in).astype(BF16)
    u_off = d_attn + 2 * d_kv + d_attn
    u_ab = _dot(h_tm, win_ref[:, u_off:u_off + 2 * d_conv])
    u_tm = u_ab[:, :d_conv] * jax.nn.sigmoid(u_ab[:, d_conv:])
    g_c_tm = _dot(h_tm, win_ref[:, u_off + 2 * d_conv:])

    lo = lax.broadcasted_iota(jnp.int32, (dec, LANES), 1) < HALF
    zero = jnp.zeros((dec, LANES), F32)
    sink = sinkcol_ref[...]
    bias = bias_ref[...]

    def keys_of(bs, cache_ref, new_ref, state_ref):
        cached = [cache_ref[b].T for b in bs]
        new = [new_ref[pl.ds(pl.multiple_of(b * dec, dec), dec), :] for b in bs]
        for b, c, n in zip(bs, cached, new):
            state_ref[b] = jnp.concatenate([c[dec:, :], n], axis=0)
        return jnp.concatenate(cached + new, axis=0).astype(BF16)

    def scores(bs):
        rows = []
        for b in bs:
            q_b = q_ref[pl.ds(pl.multiple_of(b * dec, dec), dec), :]
            for h in range(N_HEADS):
                tile_q = q_b[:, (h // 2) * LANES:(h // 2 + 1) * LANES]
                in_lo = h % 2 == 0
                want_lo = h // GQA_GROUP == 0
                src = tile_q if in_lo == want_lo else pltpu.roll(tile_q, HALF, axis=1)
                rows.append(jnp.where(lo, src, zero) if want_lo else jnp.where(lo, zero, src))
        q_rows = jnp.concatenate(rows, axis=0).astype(BF16)
        return _dot_nt(q_rows, keys_of(bs, ck_ref, kn_ref, ko_ref)) + bias

    def attend(bs, s):
        p, rden = _softmax_sink(s, sink)
        o_all = _dot(p.astype(BF16), keys_of(bs, cv_ref, vn_ref, vo_ref)) * rden
        for e, b in enumerate(bs):
            o = o_all[e * rows_q:(e + 1) * rows_q, :]
            tiles = []
            for pair in range(N_PAIRS):
                a = o[(2 * pair) * dec:(2 * pair + 1) * dec, :]
                c = o[(2 * pair + 1) * dec:(2 * pair + 2) * dec, :]
                if pair // PAIRS_PER_KV == 0:
                    tiles.append(jnp.where(lo, a, pltpu.roll(c, HALF, axis=1)))
                else:
                    tiles.append(jnp.where(lo, pltpu.roll(a, HALF, axis=1), c))
            attn_ref[pl.ds(pl.multiple_of(b * dec, dec), dec), :] = jnp.concatenate(tiles, axis=1)

    upt_ref[:CONV_TAIL] = cs_ref[...]
    for tt in range(dec):
        upt_ref[CONV_TAIL + tt] = u_tm[tt * group:(tt + 1) * group, :]
    co_ref[...] = upt_ref[dec:]

    def conv_step(tt):
        for g0 in range(0, group, SUBLANES):
            acc = jnp.zeros((SUBLANES, d_conv), F32)
            for j in range(CONV_W):
                acc = acc + upt_ref[tt + j, g0:g0 + SUBLANES, :] * wb_ref[j]
            start = tt * group + g0
            if not isinstance(start, int):
                start = pl.multiple_of(start, SUBLANES)
            conv_ref[pl.ds(start, SUBLANES), :] = acc

    trips = group // SAMPLE_UNROLL
    conv_per_trip = dec // trips

    def attention_batch(i, carry):
        chains = [[i * SAMPLE_UNROLL + c * SAMPLE_FUSE + e for e in range(SAMPLE_FUSE)]
                  for c in range(SAMPLE_UNROLL // SAMPLE_FUSE)]
        ss = [scores(bs) for bs in chains]
        for bs, s in zip(chains, ss):
            attend(bs, s)
        for c in range(conv_per_trip):
            conv_step(i * conv_per_trip + c)
        return carry

    lax.fori_loop(0, trips, attention_batch, 0)
    for tt in range(trips * conv_per_trip, dec):
        conv_step(tt)

    y_a = _dot(_gate(attn_ref[...], g_a), wout_ref[:d_attn, :])
    c2_tm = _conv_branch_tail(conv_ref[...] + dwb_ref[...], lng_ref[...], lnb_ref[...], wpw_ref)
    m_c = _dot(perm_t_ref[...], _gate(c2_tm, g_c_tm)).astype(BF16)
    y = x + y_a + _dot(m_c, wout_ref[d_attn:, :])
    y_ref[...] = _rms_norm(y, fg_ref[...]) if final_norm else y


def _sample_layer(x, ck, cv, cs, sinks, ng, win, dww, dwb, lng, lnb, wpw, wout, fg, final_norm):
    Bs, dec, D = x.shape
    group = SAMPLE_GROUP
    assert Bs % group == 0 and dec % 8 == 0 and dec <= CONV_TAIL
    d_conv = dww.shape[-1]
    d_kv = N_KV_HEADS * HEAD_DIM
    d_attn = N_HEADS * HEAD_DIM
    rows = group * dec
    x2 = x.reshape(Bs * dec, D)
    assert all(n & (n - 1) == 0 for n in (group, dec, N_HEADS, WINDOW))
    assert group % SAMPLE_UNROLL == 0 and SAMPLE_UNROLL % SAMPLE_FUSE == 0
    const = lambda shape: pl.BlockSpec(shape, lambda g: (0,) * len(shape))
    once = lambda shape: pl.BlockSpec(shape, lambda g: (0,) * len(shape), pipeline_mode=pl.Buffered(1))
    per_group = lambda shape: pl.BlockSpec((group,) + shape, lambda g: (g,) + (0,) * len(shape))
    conv_state = pl.BlockSpec((CONV_TAIL, group, d_conv), lambda g: (0, g, 0))
    y, ko, vo, co = pl.pallas_call(
        functools.partial(_sample_kernel, group=group, dec=dec, final_norm=final_norm),
        grid=(Bs // group,),
        in_specs=[
            pl.BlockSpec((rows, D), lambda g: (g, 0)),
            per_group((d_kv, WINDOW)), per_group((d_kv, WINDOW)), conv_state,
            pl.BlockSpec(memory_space=pltpu.SMEM),
            const(ng.shape), once(win.shape), const(dww.shape), const(dwb.shape),
            const(lng.shape), const(lnb.shape), once(wpw.shape), once(wout.shape), const(fg.shape),
        ],
        out_specs=[
            pl.BlockSpec((rows, D), lambda g: (g, 0)),
            per_group((WINDOW, d_kv)), per_group((WINDOW, d_kv)), conv_state,
        ],
        out_shape=[
            jax.ShapeDtypeStruct((Bs * dec, D), F32),
            jax.ShapeDtypeStruct((Bs, WINDOW, d_kv), F32),
            jax.ShapeDtypeStruct((Bs, WINDOW, d_kv), F32),
            jax.ShapeDtypeStruct((CONV_TAIL, Bs, d_conv), F32),
        ],
        scratch_shapes=[
            pltpu.VMEM((rows, d_attn), F32),
            pltpu.VMEM((rows, d_kv), F32),
            pltpu.VMEM((rows, d_kv), F32),
            pltpu.VMEM((CONV_TAIL + dec, group, d_conv), F32),
            pltpu.VMEM((CONV_W, SUBLANES, d_conv), F32),
            pltpu.VMEM((SAMPLE_FUSE * N_HEADS * dec, SAMPLE_FUSE * (WINDOW + dec)), F32),
            pltpu.VMEM((SAMPLE_FUSE * N_HEADS * dec, 1), F32),
            pltpu.VMEM((rows, d_attn), F32),
            pltpu.VMEM((rows, d_conv), F32),
            pltpu.VMEM((rows, rows), BF16),
            pltpu.VMEM((rows, rows), BF16),
            pltpu.VMEM(win.shape, BF16), pltpu.VMEM(wpw.shape, BF16), pltpu.VMEM(wout.shape, BF16),
        ],
        compiler_params=pltpu.CompilerParams(
            dimension_semantics=("arbitrary",), vmem_limit_bytes=VMEM_LIMIT_BYTES),
        name="sample_layer",
    )(x2, ck, cv, cs, sinks, ng, win, dww, dwb, lng, lnb, wpw, wout, fg)
    return y.reshape(Bs, dec, D), ko, vo, co


def kernel(x_prompt, x_sample, cache_k, cache_v, state_conv, norm_g, w_in, attn_sinks, dw_w, dw_b,
           conv_ln_g, conv_ln_b, w_pw2, w_out, final_norm_g):
    depth = w_in.shape[0]
    B = x_prompt.shape[0]
    Bs, dec = x_sample.shape[0], x_sample.shape[1]
    d_kv = N_KV_HEADS * HEAD_DIM
    fg = final_norm_g.reshape(1, -1)
    hp, hs = x_prompt, x_sample
    pk, pv, pc, sk, sv, sc = [], [], [], [], [], []
    kv_major = lambda c: c.transpose(0, 2, 3, 1).reshape(Bs, d_kv, WINDOW)
    for l in range(depth):
        final_norm = l == depth - 1
        row = lambda a: a[l].reshape(1, -1)
        win, wpw, wout = w_in[l], w_pw2[l], w_out[l]
        shared = (row(norm_g), win, dw_w[l][:, None, :], row(dw_b), row(conv_ln_g), row(conv_ln_b),
                  wpw, wout, fg)
        hp, k_p, v_p, c_p = _prompt_layer(hp, attn_sinks[l], *shared, final_norm)
        hs, k_s, v_s, c_s = _sample_layer(
            hs, kv_major(cache_k[l]), kv_major(cache_v[l]),
            state_conv[l].swapaxes(0, 1), attn_sinks[l], *shared, final_norm)
        key_major = lambda s: s.reshape(B, N_KV_HEADS, HEAD_DIM, WINDOW).transpose(0, 3, 1, 2)
        pk.append(key_major(k_p))
        pv.append(key_major(v_p))
        pc.append(c_p.swapaxes(0, 1))
        sk.append(k_s.reshape(Bs, WINDOW, N_KV_HEADS, HEAD_DIM))
        sv.append(v_s.reshape(Bs, WINDOW, N_KV_HEADS, HEAD_DIM))
        sc.append(c_s.swapaxes(0, 1))
    stack = lambda xs: xs[0][None] if len(xs) == 1 else jnp.stack(xs)
    return (hp, hs, stack(pk), stack(pv), stack(pc), stack(sk), stack(sv), stack(sc))
```
